```python
import math
import jax
import jax.numpy as jnp
from jax import lax
import numpy as np

D_MODEL = 1024
BATCH = 16
SEQ = 256
DEPTH = 2
DEC_BATCH = 4
DEC_SEQ = 4096
PAST_LEN = 512

GRID_W = 64
ROPE_BASE = 10000.0
NORM_EPS = 1e-6
QBLOCK = 128

A_HD = 32
A_HEADS = (D_MODEL // 4) // (2 * A_HD)
A_WIDTH = A_HEADS * 2 * A_HD
B_GROUPS = 4
B_WIDTH = D_MODEL // 4
B_GC = B_WIDTH // B_GROUPS
POOL_WINDOWS = (2, 4, 8, 16)
C_HD = 64
C_HEADS = (D_MODEL // 2) // C_HD
C_KV = C_HEADS // 4
C_GROUP = C_HEADS // C_KV
C_WIDTH = C_HEADS * C_HD
WINDOW = 128
MIX_WIDTH = A_WIDTH + B_WIDTH + C_WIDTH
IN_WIDTH = 3 * A_WIDTH + B_WIDTH + (C_HEADS + 2 * C_KV) * C_HD
N_EXPERTS = 32
TOP_K = 4
D_FF = D_MODEL
SWIGLU_LIMIT = 7.0
SWIGLU_ALPHA = 1.702
MOE_BLOCK = 128

kernel_name = 'hybrid_diffusion_prefix_trunk_step'


def _rmsnorm(x, g):
    xf = x.astype(jnp.float32)
    y = xf * lax.rsqrt(jnp.mean(xf * xf, axis=-1, keepdims=True) + NORM_EPS)
    return (y * g.astype(jnp.float32)).astype(x.dtype)


def _modulate(x, g, shift, scale):
    return _rmsnorm(x, g) * (1.0 + scale) + shift


def _mod_vectors(cvec, w_mod, b_mod):
    m = jnp.einsum('...d,de->...e', jax.nn.silu(cvec), w_mod) + b_mod
    return jnp.split(m, 6, axis=-1)


def _axial_tables(n, head_dim):
    rows = n // GRID_W
    row = jnp.repeat(jnp.arange(rows, dtype=jnp.float32), GRID_W)
    col = jnp.tile(jnp.arange(GRID_W, dtype=jnp.float32), rows)
    n_freq = head_dim // 4
    inv = ROPE_BASE ** (-jnp.arange(n_freq, dtype=jnp.float32) / n_freq)
    ang = jnp.concatenate([row[:, None] * inv, col[:, None] * inv], axis=-1)
    return jnp.cos(ang), jnp.sin(ang)


def _rope2d(x, cos, sin):
    b, n, h, d = x.shape
    nf = d // 4
    xs = x.astype(jnp.float32).reshape(b, n, h, 2, 2, nf)
    c = cos.reshape(n, 1, 2, nf)
    s = sin.reshape(n, 1, 2, nf)
    x1, x2 = xs[..., 0, :], xs[..., 1, :]
    out = jnp.stack([x1 * c - x2 * s, x2 * c + x1 * s], axis=-2)
    return out.reshape(b, n, h, d).astype(x.dtype)


def _project(h, w_in):
    z = jnp.einsum('bld,de->ble', h, w_in)
    b, l, _ = z.shape
    cuts = np.cumsum([A_WIDTH, A_WIDTH, A_WIDTH, B_WIDTH, C_HEADS * C_HD, C_KV * C_HD]).tolist()
    qa, ka, va, u, qc, kc, vc = jnp.split(z, cuts, axis=-1)
    return (qa.reshape(b, l, 2 * A_HEADS, A_HD), ka.reshape(b, l, 2 * A_HEADS, A_HD),
            va.reshape(b, l, A_HEADS, 2 * A_HD), u,
            qc.reshape(b, l, C_HEADS, C_HD), kc.reshape(b, l, C_KV, C_HD), vc.reshape(b, l, C_KV, C_HD))


def _diff_lambda(lam_p, lam_init):
    lp = lam_p.astype(jnp.float32)
    return jnp.exp(jnp.sum(lp[0] * lp[1])) - jnp.exp(jnp.sum(lp[2] * lp[3])) + lam_init


def _diff_attention(q, k, v, lam, subln_g, lam_init):
    b, t, h2, d = q.shape
    nb = t // QBLOCK
    scale = d ** -0.5
    qb = jnp.moveaxis(q.reshape(b, nb, QBLOCK, h2, d), 1, 0)

    def block(qblk):
        s = jnp.einsum('bqhd,bshd->bhqs', qblk, k).astype(jnp.float32) * scale
        p = jax.nn.softmax(s, axis=-1).reshape(b, A_HEADS, 2, QBLOCK, -1)
        w = (p[:, :, 0] - lam * p[:, :, 1]).astype(v.dtype)
        return jnp.einsum('bhqs,bshe->bqhe', w, v)

    o = jnp.moveaxis(lax.map(block, qb), 0, 1).reshape(b, t, A_HEADS, 2 * d)
    o = _rmsnorm(o, subln_g) * (1.0 - lam_init)
    return o.reshape(b, t, A_HEADS * 2 * d)


def _pool_mixer(u, w_pool, pool_scale):
    b, l, _ = u.shape
    ug = u.reshape(b, l, B_GROUPS, B_GC)
    cs = jnp.pad(jnp.cumsum(ug.astype(jnp.float32), axis=1), ((0, 0), (1, 0), (0, 0), (0, 0)))
    half = jnp.array(POOL_WINDOWS, jnp.int32) // 2
    t = jnp.arange(l, dtype=jnp.int32)[:, None]
    lo = jnp.clip(t - half, 0, l)
    hi = jnp.clip(t + half, 0, l)
    gidx = jnp.arange(B_GROUPS)[None, :]
    win_sum = cs[:, hi, gidx] - cs[:, lo, gidx]
    mean = win_sum / (hi - lo).astype(jnp.float32)[None, :, :, None]
    r = (mean - ug.astype(jnp.float32)).astype(u.dtype)
    y = jnp.einsum('blgc,gce->blge', r, w_pool).reshape(b, l, B_WIDTH)
    return y * pool_scale


def _sink_softmax(s, sink_g):
    sk = jnp.broadcast_to(sink_g[None, :, :, None, None], s.shape[:-1] + (1,))
    p = jax.nn.softmax(jnp.concatenate([s, sk], axis=-1), axis=-1)
    return p[..., :-1]


def _global_sink_gqa(q, k, v, sink):
    b, t, h, d = q.shape
    nb = t // QBLOCK
    scale = d ** -0.5
    sink_g = sink.astype(jnp.float32).reshape(C_KV, C_GROUP)
    qb = jnp.moveaxis(q.reshape(b, nb, QBLOCK, C_KV, C_GROUP, d), 1, 0)

    def block(qblk):
        s = jnp.einsum('bqkgd,bskd->bkgqs', qblk, k).astype(jnp.float32) * scale
        p = _sink_softmax(s, sink_g).astype(v.dtype)
        return jnp.einsum('bkgqs,bskd->bqkgd', p, v)

    o = lax.map(block, qb)
    return jnp.moveaxis(o, 0, 1).reshape(b, t, h * d)


def _window_sink_gqa(q, k, v, k_ctx, v_ctx, sink):
    b, t, h, d = q.shape
    nb = t // QBLOCK
    scale = d ** -0.5
    sink_g = sink.astype(jnp.float32).reshape(C_KV, C_GROUP)
    kp = jnp.pad(k, ((0, 0), (QBLOCK, QBLOCK), (0, 0), (0, 0)))
    vp = jnp.pad(v, ((0, 0), (QBLOCK, QBLOCK), (0, 0), (0, 0)))
    qb = jnp.moveaxis(q.reshape(b, nb, QBLOCK, C_KV, C_GROUP, d), 1, 0)
    q_loc = jnp.arange(QBLOCK)
    k_loc = jnp.arange(3 * QBLOCK) - QBLOCK
    in_band = jnp.abs(k_loc[None, :] - q_loc[:, None]) <= WINDOW

    def block(args):
        i, qblk = args
        kb = lax.dynamic_slice_in_dim(kp, i * QBLOCK, 3 * QBLOCK, axis=1)
        vb = lax.dynamic_slice_in_dim(vp, i * QBLOCK, 3 * QBLOCK, axis=1)
        k_abs = i * QBLOCK + k_loc
        valid = in_band & ((k_abs >= 0) & (k_abs < t))[None, :]
        s_loc = jnp.einsum('bqkgd,bskd->bkgqs', qblk, kb).astype(jnp.float32) * scale
        s_loc = jnp.where(valid, s_loc, -jnp.inf)
        s_ctx = jnp.einsum('bqkgd,bskd->bkgqs', qblk, k_ctx).astype(jnp.float32) * scale
        p = _sink_softmax(jnp.concatenate([s_loc, s_ctx], axis=-1), sink_g)
        p_loc = p[..., :3 * QBLOCK].astype(v.dtype)
        p_ctx = p[..., 3 * QBLOCK:].astype(v.dtype)
        return (jnp.einsum('bkgqs,bskd->bqkgd', p_loc, vb)
                + jnp.einsum('bkgqs,bskd->bqkgd', p_ctx, v_ctx))

    o = lax.map(block, (jnp.arange(nb), qb))
    return jnp.moveaxis(o, 0, 1).reshape(b, t, h * d)


def _moe(h, w_router, b_router, w1, b1, w2, b2):
    b, l, d = h.shape
    n = b * l
    x = h.reshape(n, d)
    logits = jnp.einsum('td,de->te', x, w_router).astype(jnp.float32) + b_router.astype(jnp.float32)
    top_logit, top_idx = lax.top_k(logits, TOP_K)
    gate = jax.nn.softmax(top_logit, axis=-1)
    n_assign = n * TOP_K
    flat_e = top_idx.reshape(n_assign)
    order = jnp.argsort(flat_e)
    sorted_e = flat_e[order]
    sorted_tok = (order // TOP_K).astype(jnp.int32)
    sorted_gate = gate.reshape(n_assign)[order]
    counts = jnp.bincount(flat_e, length=N_EXPERTS)
    padded = (counts + MOE_BLOCK - 1) // MOE_BLOCK * MOE_BLOCK
    pad_end = jnp.cumsum(padded)
    pad_start = pad_end - padded
    start = jnp.cumsum(counts) - counts
    dest = pad_start[sorted_e] + jnp.arange(n_assign) - start[sorted_e]
    n_blocks = -(-n_assign // MOE_BLOCK) + N_EXPERTS
    n_rows = n_blocks * MOE_BLOCK
    row_tok = jnp.zeros((n_rows,), jnp.int32).at[dest].set(sorted_tok)
    row_gate = jnp.zeros((n_rows,), jnp.float32).at[dest].set(sorted_gate)
    block_expert = jnp.minimum(
        jnp.searchsorted(pad_end, jnp.arange(n_blocks) * MOE_BLOCK, side='right'), N_EXPERTS - 1)

    def block(args):
        e, tok, g = args
        gu = x[tok] @ w1[e] + b1[e]
        gt = jnp.minimum(gu[:, :D_FF], SWIGLU_LIMIT)
        up = jnp.clip(gu[:, D_FF:], -SWIGLU_LIMIT, SWIGLU_LIMIT)
        hid = (up + 1.0) * gt * jax.nn.sigmoid(SWIGLU_ALPHA * gt)
        out = hid @ w2[e] + b2[e]
        return out * g[:, None].astype(out.dtype)

    outs = lax.map(block, (block_expert, row_tok.reshape(n_blocks, MOE_BLOCK),
                           row_gate.reshape(n_blocks, MOE_BLOCK)))
    y = jax.ops.segment_sum(outs.reshape(n_rows, d), row_tok, num_segments=n)
    return y.reshape(b, l, d)


def _finish_layer(x, oa, ob, oc, g1, sh2, sc2, g2, n2, wo, wr, br, e1, eb1, e2, eb2):
    x = x + g1 * jnp.einsum('blm,md->bld', jnp.concatenate([oa, ob, oc], axis=-1), wo)
    h2 = _modulate(x, n2, sh2, sc2)
    return x + g2 * _moe(h2, wr, br, e1, eb1, e2, eb2)


def _context_layer(x, cvec, P, lam_init):
    (n1, n2, wm, bm, wi, dl, dsg, wp, ps, sk, wo, wr, br, e1, eb1, e2, eb2) = P
    sh1, sc1, g1, sh2, sc2, g2 = _mod_vectors(cvec, wm, bm)
    qa, ka, va, u, qc, kc, vc = _project(_modulate(x, n1, sh1, sc1), wi)
    lam = _diff_lambda(dl, lam_init)
    oa = _diff_attention(qa, ka, va, lam, dsg, lam_init)
    ob = _pool_mixer(u, wp, ps)
    oc = _global_sink_gqa(qc, kc, vc, sk)
    x = _finish_layer(x, oa, ob, oc, g1, sh2, sc2, g2, n2, wo, wr, br, e1, eb1, e2, eb2)
    return x, (ka, va, kc, vc)


def _latent_layer(x, cvec, P, lam_init, ctx, rope_a, rope_c):
    (n1, n2, wm, bm, wi, dl, dsg, wp, ps, sk, wo, wr, br, e1, eb1, e2, eb2) = P
    dk, dv, wk, wv = ctx
    sh1, sc1, g1, sh2, sc2, g2 = _mod_vectors(cvec, wm, bm)
    qa, ka, va, u, qc, kc, vc = _project(_modulate(x, n1, sh1, sc1), wi)
    qa, ka = _rope2d(qa, *rope_a), _rope2d(ka, *rope_a)
    qc, kc = _rope2d(qc, *rope_c), _rope2d(kc, *rope_c)
    lam = _diff_lambda(dl, lam_init)
    oa = _diff_attention(qa, jnp.concatenate([ka, dk], axis=1), jnp.concatenate([va, dv], axis=1),
                         lam, dsg, lam_init)
    ob = _pool_mixer(u, wp, ps)
    oc = _window_sink_gqa(qc, kc, vc, wk, wv, sk)
    return _finish_layer(x, oa, ob, oc, g1, sh2, sc2, g2, n2, wo, wr, br, e1, eb1, e2, eb2)


def setup_inputs(seed: int = 0) -> dict:
    key = jax.random.key(seed)
    ks = jax.random.split(key, 26)

    def nrm(i, shape, scale):
        return jax.random.normal(ks[i], shape, jnp.float32) * scale

    return {
        'x_prompt': nrm(0, (BATCH, SEQ, D_MODEL), 1.0),
        'x_sample': nrm(1, (DEC_BATCH, DEC_SEQ, D_MODEL), 1.0),
        'c': nrm(2, (DEC_BATCH, D_MODEL), 1.0),
        'cache_diff_k': nrm(3, (DEC_BATCH, DEPTH, PAST_LEN, 2 * A_HEADS, A_HD), 1.0),
        'cache_diff_v': nrm(4, (DEC_BATCH, DEPTH, PAST_LEN, A_HEADS, 2 * A_HD), 1.0),
        'cache_win_k': nrm(5, (DEC_BATCH, DEPTH, PAST_LEN, C_KV, C_HD), 1.0),
        'cache_win_v': nrm(6, (DEC_BATCH, DEPTH, PAST_LEN, C_KV, C_HD), 1.0),
        'c_ctx': nrm(7, (D_MODEL,), 1.0),
        'norm1_g': 1.0 + nrm(8, (DEPTH, D_MODEL), 0.05),
        'norm2_g': 1.0 + nrm(9, (DEPTH, D_MODEL), 0.05),
        'w_mod': nrm(10, (DEPTH, D_MODEL, 6 * D_MODEL), D_MODEL ** -0.5),
        'b_mod': nrm(11, (DEPTH, 6 * D_MODEL), 0.02),
        'w_in': nrm(12, (DEPTH, D_MODEL, IN_WIDTH), D_MODEL ** -0.5),
        'diff_lambda': nrm(13, (DEPTH, 4, A_HD), 0.1),
        'diff_subln_g': 1.0 + nrm(14, (DEPTH, 2 * A_HD), 0.05),
        'w_pool': nrm(15, (DEPTH, B_GROUPS, B_GC, B_GC), B_GC ** -0.5),
        'pool_scale': 1.0 + nrm(16, (DEPTH, B_WIDTH), 0.1),
        'sink': nrm(17, (DEPTH, C_HEADS), 0.5),
        'w_out': nrm(18, (DEPTH, MIX_WIDTH, D_MODEL), MIX_WIDTH ** -0.5),
        'w_router': nrm(19, (DEPTH, D_MODEL, N_EXPERTS), D_MODEL ** -0.5),
        'b_router': nrm(20, (DEPTH, N_EXPERTS), 0.01),
        'w1': nrm(21, (DEPTH, N_EXPERTS, D_MODEL, 2 * D_FF), D_MODEL ** -0.5),
        'b1': nrm(22, (DEPTH, N_EXPERTS, 2 * D_FF), 0.02),
        'w2': nrm(23, (DEPTH, N_EXPERTS, D_FF, D_MODEL), D_FF ** -0.5),
        'b2': nrm(24, (DEPTH, N_EXPERTS, D_MODEL), 0.02),
        'final_g': 1.0 + nrm(25, (D_MODEL,), 0.05),
    }


def reference(x_prompt, x_sample, c, cache_diff_k, cache_diff_v, cache_win_k, cache_win_v, c_ctx,
              norm1_g, norm2_g, w_mod, b_mod, w_in, diff_lambda, diff_subln_g, w_pool, pool_scale,
              sink, w_out, w_router, b_router, w1, b1, w2, b2, final_g):
    n_lat = x_sample.shape[1]
    rope_a = _axial_tables(n_lat, A_HD)
    rope_c = _axial_tables(n_lat, C_HD)
    cvec_ctx = c_ctx[None, None, :]
    cvec_lat = c[:, None, :]
    xp, xs = x_prompt, x_sample
    dks, dvs, wks, wvs = [], [], [], []
    for i in range(DEPTH):
        lam_init = 0.8 - 0.6 * math.exp(-0.3 * i)
        P = (norm1_g[i], norm2_g[i], w_mod[i], b_mod[i], w_in[i], diff_lambda[i], diff_subln_g[i],
             w_pool[i], pool_scale[i], sink[i], w_out[i], w_router[i], b_router[i],
             w1[i], b1[i], w2[i], b2[i])
        xp, (ka, va, kc, vc) = _context_layer(xp, cvec_ctx, P, lam_init)
        dks.append(ka)
        dvs.append(va)
        wks.append(kc)
        wvs.append(vc)
        ctx = (cache_diff_k[:, i], cache_diff_v[:, i], cache_win_k[:, i], cache_win_v[:, i])
        xs = _latent_layer(xs, cvec_lat, P, lam_init, ctx, rope_a, rope_c)
    y_prompt = _rmsnorm(xp, final_g)
    y_sample = _rmsnorm(xs, final_g)
    new_diff_k = jnp.stack(dks, axis=1)
    new_diff_v = jnp.stack(dvs, axis=1)
    new_win_k = jnp.stack(wks, axis=1)
    new_win_v = jnp.stack(wvs, axis=1)
    return (y_prompt, y_sample, new_diff_k, new_diff_v, new_win_k, new_win_v)
```

```python
import functools
import math

import numpy as np
import jax
import jax.numpy as jnp
from jax import lax
from jax.experimental import pallas as pl
from jax.experimental.pallas import tpu as pltpu

F32 = jnp.float32
BF16 = jnp.bfloat16
I32 = jnp.int32

D_MODEL = 1024
GRID_W = 64
ROPE_BASE = 10000.0
NORM_EPS = 1e-6
A_HD = 32
A_HEADS = 4
A_WIDTH = 256
B_WIDTH = 256
B_GROUPS = 4
B_GC = 64
C_HD = 64
C_HEADS = 8
C_KV = 2
C_GROUP = 4
C_WIDTH = 512
WINDOW = 128
QBLOCK = 128
N_EXPERTS = 32
TOP_K = 4
D_FF = 1024
SWIGLU_LIMIT = 7.0
SWIGLU_ALPHA = 1.702

LANES = 128
ROW_TILE = 256
MOE_ROWS = 256
KEY_CHUNK = 512
NEG = -1e30
VMEM_LIMIT = 56 * 1024 * 1024


def _params(*sem):
    return pltpu.CompilerParams(dimension_semantics=sem, vmem_limit_bytes=VMEM_LIMIT)


def _split(x):
    hi = x.astype(BF16)
    lo = (x - hi.astype(F32)).astype(BF16)
    return hi, lo


def _dot3(a, b):
    ah, al = _split(a)
    bh, bl = _split(b)
    return (jnp.dot(ah, bh, preferred_element_type=F32)
            + (jnp.dot(ah, bl, preferred_element_type=F32)
               + jnp.dot(al, bh, preferred_element_type=F32)))


def _nt_dot(a, b):
    return lax.dot_general(a, b, (((1,), (1,)), ((), ())), preferred_element_type=F32)


def _rms(x):
    return x * lax.rsqrt(jnp.mean(x * x, axis=-1, keepdims=True) + NORM_EPS)


def _mod_kernel(c_ref, w_ref, b_ref, o_ref):
    c = c_ref[...]
    a = c * (1.0 / (1.0 + jnp.exp(-c)))
    o_ref[0] = _dot3(a, w_ref[0]) + b_ref[0]


def _mod_vectors(cmat, w_mod, b_mod):
    depth, d, e = w_mod.shape
    rows = cmat.shape[0]
    tn = 512
    return pl.pallas_call(
        _mod_kernel,
        grid=(depth, e // tn),
        in_specs=[pl.BlockSpec((rows, d), lambda l, j: (0, 0)),
                  pl.BlockSpec((1, d, tn), lambda l, j: (l, 0, j)),
                  pl.BlockSpec((1, 1, tn), lambda l, j: (l, 0, j))],
        out_specs=pl.BlockSpec((1, rows, tn), lambda l, j: (l, 0, j)),
        out_shape=jax.ShapeDtypeStruct((depth, rows, e), F32),
        compiler_params=_params("arbitrary", "arbitrary"),
    )(cmat, w_mod, b_mod.reshape(depth, 1, e))


def _rope(z, col_ref, row_ref, nf, r0):
    tm, w = z.shape
    outs = []
    for g in range(tm // GRID_W):
        zs = z[GRID_W * g:GRID_W * (g + 1), :]
        c = col_ref[0] + row_ref[0, pl.ds(r0 + g, 1), :]
        sm = col_ref[1] + row_ref[1, pl.ds(r0 + g, 1), :]
        sp = col_ref[2] + row_ref[2, pl.ds(r0 + g, 1), :]
        outs.append(zs * c + pltpu.roll(zs, w - nf, 1) * sm + pltpu.roll(zs, nf, 1) * sp)
    return jnp.concatenate(outs, axis=0)


def _inproj_kernel(*refs, rope, tm):
    x_ref, sh_ref, sc_ref, n1_ref, w_ref = refs[:5]
    if rope:
        ta_col, ta_row, tc_col, tc_row = refs[5:9]
        qa_o, ka_o, va_o, u_o, qc_o, kr_o, vr_o = refs[9:]
    else:
        qa_o, ka_o, va_o, u_o, qc_o, kr_o, vr_o, ka32_o, va32_o, kc32_o, vc32_o = refs[5:]
    x = x_ref[0]
    h = (_rms(x) * n1_ref[...]) * (1.0 + sc_ref[0]) + sh_ref[0]
    hb = h.astype(BF16)

    def seg(a, b):
        return jnp.dot(hb, w_ref[:, a:b], preferred_element_type=F32)

    qa, ka, va, u = seg(0, 256), seg(256, 512), seg(512, 768), seg(768, 1024)
    qc, kr, vr = seg(1024, 1536), seg(1536, 2048), seg(2048, 2560)
    if rope:
        r0 = pl.program_id(1) * (tm // GRID_W)
        qa = _rope(qa, ta_col, ta_row, A_HD // 4, r0)
        ka = _rope(ka, ta_col, ta_row, A_HD // 4, r0)
        qc = _rope(qc, tc_col, tc_row, C_HD // 4, r0)
        kr = _rope(kr, tc_col, tc_row, C_HD // 4, r0)
    else:
        ka32_o[0] = ka
        va32_o[0] = va
        kc32_o[0] = seg(2560, 2688)
        vc32_o[0] = seg(2688, 2816)
    qa_o[0] = (qa * (A_HD ** -0.5)).astype(BF16)
    ka_o[0] = ka.astype(BF16)
    va_o[0] = va.astype(BF16)
    u_o[0] = u
    qc_o[0] = (qc * (C_HD ** -0.5)).astype(BF16)
    kr_o[0] = kr.astype(BF16)
    vr_o[0] = vr.astype(BF16)


def _inproj(x, sh, sc, n1, w, tables):
    b, l, d = x.shape
    tm = min(ROW_TILE, l)
    rope = tables is not None
    per_batch = sh.shape[0] > 1
    bidx = (lambda i, t: (i, 0, 0)) if per_batch else (lambda i, t: (0, 0, 0))
    tok = lambda width: pl.BlockSpec((1, tm, width), lambda i, t: (i, t, 0))
    in_specs = [tok(d),
                pl.BlockSpec((1, 1, d), bidx), pl.BlockSpec((1, 1, d), bidx),
                pl.BlockSpec((1, d), lambda i, t: (0, 0)),
                pl.BlockSpec(w.shape, lambda i, t: (0, 0))]
    args = [x, sh, sc, n1, w]
    widths = [(256, BF16), (256, BF16), (256, BF16), (256, F32), (512, BF16), (512, BF16), (512, BF16)]
    if rope:
        for tab in tables:
            in_specs.append(pl.BlockSpec(tab.shape, lambda i, t: (0, 0, 0)))
            args.append(tab)
    else:
        widths += [(256, F32), (256, F32), (128, F32), (128, F32)]
    return pl.pallas_call(
        functools.partial(_inproj_kernel, rope=rope, tm=tm),
        grid=(b, l // tm),
        in_specs=in_specs,
        out_specs=[tok(wd) for wd, _ in widths],
        out_shape=[jax.ShapeDtypeStruct((b, l, wd), dt) for wd, dt in widths],
        compiler_params=_params("arbitrary", "arbitrary"),
    )(*args)


def _rope_tables(n_lat, head_dim, width):
    rows = n_lat // GRID_W
    nf = head_dim // 4
    inv = ROPE_BASE ** (-jnp.arange(nf, dtype=F32) / nf)
    lane = np.arange(width) % head_dim
    half = lane // (2 * nf)
    pair = (lane // nf) % 2
    f = lane % nf

    def part(pos, which):
        ang = pos[:, None] * inv[f][None, :]
        on = jnp.asarray(half == which, F32)[None, :]
        c = jnp.cos(ang) * on
        s = jnp.sin(ang) * on
        sm = -s * jnp.asarray(pair == 0, F32)[None, :]
        sp = s * jnp.asarray(pair == 1, F32)[None, :]
        return jnp.stack([c, sm, sp])

    return part(jnp.arange(GRID_W, dtype=F32), 1), part(jnp.arange(rows, dtype=F32), 0)


POOL_PAD = 8
POOL_CHUNK = 256


def _pool_kernel(u_ref, w_ref, ps_ref, o_ref, pad_ref, *, l):
    zeros = jnp.zeros((POOL_PAD, B_WIDTH), F32)
    pad_ref[0:POOL_PAD, :] = zeros
    pad_ref[POOL_PAD + l:2 * POOL_PAD + l, :] = zeros
    pad_ref[POOL_PAD:POOL_PAD + l, :] = u_ref[0]
    ch = min(POOL_CHUNK, l)
    lane = lax.broadcasted_iota(I32, (ch, B_WIDTH), 1)
    grp = lane >> 6
    half = jnp.where(grp == 0, 1, jnp.where(grp == 1, 2, jnp.where(grp == 2, 4, 8)))
    row = lax.broadcasted_iota(I32, (ch, B_WIDTH), 0)
    for c in range(0, l, ch):
        ld = lambda k: pad_ref[c + POOL_PAD + k:c + POOL_PAD + k + ch, :]
        cur = ld(0)
        s2 = ld(-1) + cur
        s4 = s2 + (ld(-2) + ld(1))
        s8 = s4 + ((ld(-4) + ld(-3)) + (ld(2) + ld(3)))
        s16 = s8 + (((ld(-8) + ld(-7)) + (ld(-6) + ld(-5))) + ((ld(4) + ld(5)) + (ld(6) + ld(7))))
        win = jnp.where(grp == 0, s2, jnp.where(grp == 1, s4, jnp.where(grp == 2, s8, s16)))
        t = row + c
        cnt = (jnp.minimum(t + half, l) - jnp.maximum(t - half, 0)).astype(F32)
        r = (win / cnt - cur).astype(BF16)
        y = jnp.dot(r, w_ref[...], preferred_element_type=F32) * ps_ref[...]
        o_ref[0, c:c + ch, :] = y.astype(BF16)


def _pool(u, wbd, ps):
    b, l, w = u.shape
    return pl.pallas_call(
        functools.partial(_pool_kernel, l=l),
        grid=(b,),
        in_specs=[pl.BlockSpec((1, l, w), lambda i: (i, 0, 0)),
                  pl.BlockSpec((w, w), lambda i: (0, 0)),
                  pl.BlockSpec((1, w), lambda i: (0, 0))],
        out_specs=pl.BlockSpec((1, l, w), lambda i: (i, 0, 0)),
        out_shape=jax.ShapeDtypeStruct((b, l, w), BF16),
        scratch_shapes=[pltpu.VMEM((l + 2 * POOL_PAD, w), F32)],
        compiler_params=_params("arbitrary"),
    )(u, wbd, ps)


def _flash_update(qm, kc, vc, m, l, acc):
    s = _nt_dot(qm, kc)
    m_new = jnp.maximum(m, jnp.max(s, axis=-1, keepdims=True))
    alpha = jnp.exp(m - m_new)
    p = jnp.exp(s - m_new)
    l = alpha * l + jnp.sum(p, axis=-1, keepdims=True)
    acc = alpha * acc + jnp.dot(p.astype(BF16), vc, preferred_element_type=F32)
    return m_new, l, acc


def _diff_attn_kernel(*refs, n_src, tq, lam_init):
    q_ref = refs[0]
    srcs = [(refs[1 + 2 * i], refs[2 + 2 * i]) for i in range(n_src)]
    lamp_ref, g_ref, o_ref = refs[1 + 2 * n_src:]
    q32 = q_ref[0].astype(F32)
    lp = lamp_ref[...]
    lam = (jnp.exp(jnp.sum(lp[0:1] * lp[1:2], axis=-1, keepdims=True))
           - jnp.exp(jnp.sum(lp[2:3] * lp[3:4], axis=-1, keepdims=True)) + lam_init)
    lane_row = lax.broadcasted_iota(I32, (1, A_WIDTH), 1)
    lane = lax.broadcasted_iota(I32, (tq, A_WIDTH), 1)
    out = jnp.zeros((tq, A_WIDTH), F32)
    for h in range(A_HEADS):
        maps = []
        for mp in range(2):
            qm = (q32 * ((lane_row >> 5) == 2 * h + mp).astype(F32)).astype(BF16)
            state = (jnp.full((tq, 1), NEG, F32), jnp.zeros((tq, 1), F32), jnp.zeros((tq, A_WIDTH), F32))
            for k_ref, v_ref in srcs:
                s_len = k_ref.shape[1]
                tk = min(s_len, KEY_CHUNK)
                if s_len == tk:
                    state = _flash_update(qm, k_ref[0], v_ref[0], *state)
                else:
                    def body(c, carry, k_ref=k_ref, v_ref=v_ref, tk=tk, qm=qm):
                        st = pl.multiple_of(c * tk, tk)
                        return _flash_update(qm, k_ref[0, pl.ds(st, tk), :], v_ref[0, pl.ds(st, tk), :], *carry)
                    state = lax.fori_loop(0, s_len // tk, body, state)
            maps.append(state[2] / state[1])
        out = out + jnp.where((lane >> 6) == h, maps[0] - lam * maps[1], 0.0)
    sq = out * out
    rs = jnp.zeros((tq, A_WIDTH), F32)
    for h in range(A_HEADS):
        msk = (lane >> 6) == h
        ms = jnp.sum(jnp.where(msk, sq, 0.0), axis=-1, keepdims=True) * (1.0 / (2 * A_HD))
        rs = rs + jnp.where(msk, lax.rsqrt(ms + NORM_EPS), 0.0)
    o_ref[0] = (((out * rs) * g_ref[...]) * (1.0 - lam_init)).astype(BF16)


def _diff_attn(q, srcs, lam_p, g_tiled, lam_init):
    b, l, w = q.shape
    tq = min(ROW_TILE, l)
    in_specs = [pl.BlockSpec((1, tq, w), lambda i, t: (i, t, 0))]
    args = [q]
    for k, v in srcs:
        for a in (k, v):
            in_specs.append(pl.BlockSpec((1,) + a.shape[1:], lambda i, t: (i, 0, 0)))
            args.append(a)
    in_specs += [pl.BlockSpec(lam_p.shape, lambda i, t: (0, 0)),
                 pl.BlockSpec(g_tiled.shape, lambda i, t: (0, 0))]
    args += [lam_p, g_tiled]
    return pl.pallas_call(
        functools.partial(_diff_attn_kernel, n_src=len(srcs), tq=tq, lam_init=lam_init),
        grid=(b, l // tq),
        in_specs=in_specs,
        out_specs=pl.BlockSpec((1, tq, w), lambda i, t: (i, t, 0)),
        out_shape=jax.ShapeDtypeStruct((b, l, w), BF16),
        compiler_params=_params("arbitrary", "arbitrary"),
    )(*args)


GQA_SLAB = C_GROUP * C_HD


def _gqa_kernel(*refs, windowed, tq, l):
    if windowed:
        sink_ref, q_ref, k_ref, v_ref, kc_ref, vc_ref, o_ref = refs
    else:
        sink_ref, q_ref, k_ref, v_ref, o_ref = refs
    i = pl.program_id(1)
    rows = C_GROUP * tq
    shift = int(math.log2(tq))
    q32 = q_ref[0].astype(F32)
    lane_row = lax.broadcasted_iota(I32, (1, GQA_SLAB), 1)
    lane = lax.broadcasted_iota(I32, (tq, GQA_SLAB), 1)
    rid = lax.broadcasted_iota(I32, (rows, 1), 0)
    for g in range(C_KV):
        sl = slice(GQA_SLAB * g, GQA_SLAB * (g + 1))
        qg = q32[:, sl]
        qs = jnp.concatenate(
            [(qg * ((lane_row >> 6) == hh).astype(F32)).astype(BF16) for hh in range(C_GROUP)], axis=0)
        sk = jnp.zeros((rows, 1), F32)
        for hh in range(C_GROUP):
            sk = jnp.where((rid >> shift) == hh, sink_ref[C_GROUP * g + hh], sk)
        if windowed:
            ws = pl.multiple_of(jnp.clip((i - 1) * tq, 0, l - 3 * tq), tq)
            s_loc = _nt_dot(qs, k_ref[0, pl.ds(ws, 3 * tq), sl])
            qpos = i * tq + (rid & (tq - 1))
            kpos = ws + lax.broadcasted_iota(I32, (1, 3 * tq), 1)
            s_loc = jnp.where(jnp.abs(kpos - qpos) <= WINDOW, s_loc, NEG)
            s_ctx = _nt_dot(qs, kc_ref[0, :, sl])
            m = jnp.maximum(sk, jnp.maximum(jnp.max(s_loc, axis=-1, keepdims=True),
                                            jnp.max(s_ctx, axis=-1, keepdims=True)))
            p_loc = jnp.exp(s_loc - m)
            p_ctx = jnp.exp(s_ctx - m)
            den = (jnp.exp(sk - m) + jnp.sum(p_loc, axis=-1, keepdims=True)
                   + jnp.sum(p_ctx, axis=-1, keepdims=True))
            pv = (jnp.dot(p_loc.astype(BF16), v_ref[0, pl.ds(ws, 3 * tq), sl], preferred_element_type=F32)
                  + jnp.dot(p_ctx.astype(BF16), vc_ref[0, :, sl], preferred_element_type=F32))
        else:
            s = _nt_dot(qs, k_ref[0, :, sl])
            m = jnp.maximum(sk, jnp.max(s, axis=-1, keepdims=True))
            p = jnp.exp(s - m)
            den = jnp.exp(sk - m) + jnp.sum(p, axis=-1, keepdims=True)
            pv = jnp.dot(p.astype(BF16), v_ref[0, :, sl], preferred_element_type=F32)
        o = pv / den
        og = jnp.zeros((tq, GQA_SLAB), F32)
        for hh in range(C_GROUP):
            og = og + jnp.where((lane >> 6) == hh, o[hh * tq:(hh + 1) * tq, :], 0.0)
        o_ref[0, :, sl] = og.astype(BF16)


def _gqa(sink, q, k, v, cache=None):
    b, l, w = q.shape
    windowed = cache is not None
    tq = QBLOCK
    whole = lambda a: pl.BlockSpec((1,) + a.shape[1:], lambda i, t: (i, 0, 0))
    in_specs = [pl.BlockSpec(memory_space=pltpu.SMEM),
                pl.BlockSpec((1, tq, w), lambda i, t: (i, t, 0)), whole(k), whole(v)]
    args = [sink, q, k, v]
    if windowed:
        in_specs += [whole(cache[0]), whole(cache[1])]
        args += list(cache)
    return pl.pallas_call(
        functools.partial(_gqa_kernel, windowed=windowed, tq=tq, l=l),
        grid=(b, l // tq),
        in_specs=in_specs,
        out_specs=pl.BlockSpec((1, tq, w), lambda i, t: (i, t, 0)),
        out_shape=jax.ShapeDtypeStruct((b, l, w), BF16),
        compiler_params=_params("arbitrary", "arbitrary"),
    )(*args)


def _finish_kernel(x_ref, oa_ref, ob_ref, oc_ref, wo_ref, g1_ref, sh_ref, sc_ref, n2_ref, wr_ref, br_ref,
                   base_ref, x1_o, h2_o, idx_o, gate_o, rank_o, cnt_o, run_ref, *, tm):
    first = (pl.program_id(0) == 0) & (pl.program_id(1) == 0)

    @pl.when(first)
    def _():
        run_ref[...] = base_ref[...]

    y = (jnp.dot(oa_ref[0], wo_ref[0:256, :], preferred_element_type=F32)
         + jnp.dot(ob_ref[0], wo_ref[256:512, :], preferred_element_type=F32)
         + jnp.dot(oc_ref[0], wo_ref[512:1024, :], preferred_element_type=F32))
    x1 = x_ref[0] + g1_ref[0] * y
    x1_o[0] = x1
    h2 = (_rms(x1) * n2_ref[...]) * (1.0 + sc_ref[0]) + sh_ref[0]
    h2_o[0] = h2
    lane = lax.broadcasted_iota(I32, (tm, LANES), 1)
    lanef = lane.astype(F32)
    logits = jnp.where(lane < N_EXPERTS, _dot3(h2, wr_ref[...]) + br_ref[...], NEG)
    work = logits
    tops, sels = [], []
    for k in range(TOP_K):
        mk = jnp.max(work, axis=-1, keepdims=True)
        ik = jnp.min(jnp.where(work == mk, lanef, float(LANES)), axis=-1, keepdims=True)
        sel = lanef == ik
        work = jnp.where(sel, 2.0 * NEG, work)
        tops.append((mk, ik))
        sels.append(sel)
    es = [jnp.exp(mk - tops[0][0]) for mk, _ in tops]
    den = (es[0] + es[1]) + (es[2] + es[3])
    multi = jnp.zeros((tm, LANES), F32)
    for sel in sels:
        multi = multi + sel.astype(F32)
    r_i = lax.broadcasted_iota(I32, (tm, tm), 0)
    c_i = lax.broadcasted_iota(I32, (tm, tm), 1)
    before = (c_i < r_i).astype(BF16)
    prior = jnp.dot(before, multi.astype(BF16), preferred_element_type=F32) + run_ref[...]
    idx = jnp.zeros((tm, LANES), F32)
    gate = jnp.zeros((tm, LANES), F32)
    rank = jnp.zeros((tm, LANES), F32)
    for k in range(TOP_K):
        slot = lane == k
        idx = jnp.where(slot, tops[k][1], idx)
        gate = jnp.where(slot, es[k] / den, gate)
        rk = jnp.sum(jnp.where(sels[k], prior, 0.0), axis=-1, keepdims=True)
        rank = jnp.where(slot, rk, rank)
    idx_o[0] = idx.astype(I32)
    gate_o[0] = gate
    rank_o[0] = rank.astype(I32)
    total = run_ref[...] + jnp.sum(multi, axis=0, keepdims=True)
    run_ref[...] = total
    cnt_o[...] = total


def _finish(x, oa, ob, oc, wo, g1, sh2, sc2, n2, wr, br, base):
    b, l, d = x.shape
    tm = min(ROW_TILE, l)
    per_batch = g1.shape[0] > 1
    bidx = (lambda i, t: (i, 0, 0)) if per_batch else (lambda i, t: (0, 0, 0))
    tok = lambda width: pl.BlockSpec((1, tm, width), lambda i, t: (i, t, 0))
    const2 = lambda a: pl.BlockSpec(a.shape, lambda i, t: (0, 0))
    vec = pl.BlockSpec((1, 1, d), bidx)
    return pl.pallas_call(
        functools.partial(_finish_kernel, tm=tm),
        grid=(b, l // tm),
        in_specs=[tok(d), tok(256), tok(256), tok(512), const2(wo), vec, vec, vec,
                  const2(n2), const2(wr), const2(br), const2(base)],
        out_specs=[tok(d), tok(d), tok(LANES), tok(LANES), tok(LANES),
                   pl.BlockSpec((1, LANES), lambda i, t: (0, 0))],
        out_shape=[jax.ShapeDtypeStruct((b, l, d), F32), jax.ShapeDtypeStruct((b, l, d), F32),
                   jax.ShapeDtypeStruct((b, l, LANES), I32), jax.ShapeDtypeStruct((b, l, LANES), F32),
                   jax.ShapeDtypeStruct((b, l, LANES), I32), jax.ShapeDtypeStruct((1, LANES), F32)],
        scratch_shapes=[pltpu.VMEM((1, LANES), F32)],
        compiler_params=_params("arbitrary", "arbitrary"),
    )(x, oa, ob, oc, wo, g1, sh2, sc2, n2, wr, br, base)


def _dispatch_kernel(dest_ref, h_ref, xs_in_ref, xs_ref, sem, *, tm):
    del xs_in_ref
    base = pl.program_id(0) * (tm * TOP_K)

    def body(t, carry):
        for k in range(TOP_K):
            d = dest_ref[base + t * TOP_K + k]
            pltpu.make_async_copy(h_ref.at[pl.ds(t, 1)], xs_ref.at[pl.ds(d, 1)], sem).start()
        return carry

    lax.fori_loop(0, tm, body, 0)
    for _ in range(TOP_K):
        pltpu.make_async_copy(h_ref, xs_ref.at[pl.ds(0, tm)], sem).wait()


def _dispatch(dest_flat, h2, xs):
    n, d = h2.shape
    tm = ROW_TILE
    return pl.pallas_call(
        functools.partial(_dispatch_kernel, tm=tm),
        grid_spec=pltpu.PrefetchScalarGridSpec(
            num_scalar_prefetch=1,
            grid=(n // tm,),
            in_specs=[pl.BlockSpec((tm, d), lambda i, dest: (i, 0)),
                      pl.BlockSpec(memory_space=pl.ANY)],
            out_specs=pl.BlockSpec(memory_space=pl.ANY),
            scratch_shapes=[pltpu.SemaphoreType.DMA]),
        out_shape=jax.ShapeDtypeStruct(xs.shape, xs.dtype),
        input_output_aliases={2: 0},
        compiler_params=_params("arbitrary"),
    )(dest_flat, h2, xs)


def _expert_kernel(be_ref, nu_ref, xs_ref, w1_ref, b1_ref, w2_ref, b2_ref, o_ref, w1b, w2b):
    i = pl.program_id(0)
    e = be_ref[i]
    prev = be_ref[jnp.maximum(i - 1, 0)]

    @pl.when((i == 0) | (e != prev))
    def _():
        w1b[...] = w1_ref[0].astype(BF16)
        w2b[...] = w2_ref[0].astype(BF16)

    @pl.when(i < nu_ref[0])
    def _():
        gu = jnp.dot(xs_ref[...].astype(BF16), w1b[...], preferred_element_type=F32) + b1_ref[0]
        gt = jnp.minimum(gu[:, :D_FF], SWIGLU_LIMIT)
        up = jnp.clip(gu[:, D_FF:], -SWIGLU_LIMIT, SWIGLU_LIMIT)
        hid = (up + 1.0) * gt * (1.0 / (1.0 + jnp.exp(-SWIGLU_ALPHA * gt)))
        o_ref[...] = jnp.dot(hid.astype(BF16), w2b[...], preferred_element_type=F32) + b2_ref[0]

    @pl.when(i >= nu_ref[0])
    def _():
        o_ref[...] = jnp.zeros_like(o_ref)


def _experts(block_expert, n_used, xs, w1, b1, w2, b2):
    n_rows, d = xs.shape
    n_blocks = n_rows // MOE_ROWS
    ne, _, f2 = w1.shape
    row = lambda i, be, nu: (jnp.minimum(i, nu[0] - 1), 0)
    return pl.pallas_call(
        _expert_kernel,
        grid_spec=pltpu.PrefetchScalarGridSpec(
            num_scalar_prefetch=2,
            grid=(n_blocks,),
            in_specs=[pl.BlockSpec((MOE_ROWS, d), row),
                      pl.BlockSpec((1, d, f2), lambda i, be, nu: (be[i], 0, 0)),
                      pl.BlockSpec((1, 1, f2), lambda i, be, nu: (be[i], 0, 0)),
                      pl.BlockSpec((1, D_FF, d), lambda i, be, nu: (be[i], 0, 0)),
                      pl.BlockSpec((1, 1, d), lambda i, be, nu: (be[i], 0, 0))],
            out_specs=pl.BlockSpec((MOE_ROWS, d), lambda i, be, nu: (i, 0)),
            scratch_shapes=[pltpu.VMEM((d, f2), BF16), pltpu.VMEM((D_FF, d), BF16)]),
        out_shape=jax.ShapeDtypeStruct((n_rows, d), F32),
        compiler_params=_params("arbitrary"),
    )(block_expert, n_used, xs, w1, b1.reshape(ne, 1, f2), w2, b2.reshape(ne, 1, d))


COMBINE_TILE = 128


def _combine_kernel(dest_ref, outs_ref, gate_ref, x1_ref, g2_ref, fg_ref, o_ref, buf, sem, *, tm, tiles, final):
    base = (pl.program_id(0) * tiles + pl.program_id(1)) * (tm * TOP_K)

    def body(t, carry):
        for k in range(TOP_K):
            d = dest_ref[base + t * TOP_K + k]
            pltpu.make_async_copy(outs_ref.at[pl.ds(d, 1)], buf.at[k, pl.ds(t, 1)], sem).start()
        return carry

    lax.fori_loop(0, tm, body, 0)
    for k in range(TOP_K):
        pltpu.make_async_copy(outs_ref.at[pl.ds(0, tm)], buf.at[k], sem).wait()
    gate = gate_ref[0]
    y = (gate[:, 0:1] * buf[0] + gate[:, 1:2] * buf[1]) + (gate[:, 2:3] * buf[2] + gate[:, 3:4] * buf[3])
    x2 = x1_ref[0] + g2_ref[0] * y
    if final:
        x2 = _rms(x2) * fg_ref[...]
    o_ref[0] = x2


def _combine(dest_flat, outs, gate, x1, g2, fg, final):
    b, l, d = x1.shape
    tm = COMBINE_TILE
    per_batch = g2.shape[0] > 1
    bidx = (lambda i, t, dest: (i, 0, 0)) if per_batch else (lambda i, t, dest: (0, 0, 0))
    tok = lambda width: pl.BlockSpec((1, tm, width), lambda i, t, dest: (i, t, 0))
    return pl.pallas_call(
        functools.partial(_combine_kernel, tm=tm, tiles=l // tm, final=final),
        grid_spec=pltpu.PrefetchScalarGridSpec(
            num_scalar_prefetch=1,
            grid=(b, l // tm),
            in_specs=[pl.BlockSpec(memory_space=pl.ANY), tok(LANES), tok(d),
                      pl.BlockSpec((1, 1, d), bidx),
                      pl.BlockSpec((1, d), lambda i, t, dest: (0, 0))],
            out_specs=tok(d),
            scratch_shapes=[pltpu.VMEM((TOP_K, tm, d), F32), pltpu.SemaphoreType.DMA]),
        out_shape=jax.ShapeDtypeStruct((b, l, d), F32),
        compiler_params=_params("arbitrary", "arbitrary"),
    )(dest_flat, outs, gate, x1, g2, fg)


def _layer_weights(w_in_l):
    cuts = np.cumsum([A_WIDTH, A_WIDTH, A_WIDTH, B_WIDTH, C_WIDTH, C_KV * C_HD]).tolist()
    front = w_in_l[:, :cuts[4]]
    kc = w_in_l[:, cuts[4]:cuts[5]]
    vc = w_in_l[:, cuts[5]:]

    def rep(w):
        return jnp.concatenate([w[:, C_HD * (j // C_GROUP):C_HD * (j // C_GROUP + 1)] for j in range(C_HEADS)], axis=1)

    lat = jnp.concatenate([front, rep(kc), rep(vc)], axis=1).astype(BF16)
    ctx = jnp.concatenate([front, rep(kc), rep(vc), kc, vc], axis=1).astype(BF16)
    return ctx, lat


def _rep_heads(a):
    return jnp.repeat(a, C_GROUP, axis=2).reshape(a.shape[0], a.shape[1], C_HEADS * C_HD)


def kernel(x_prompt, x_sample, c, cache_diff_k, cache_diff_v, cache_win_k, cache_win_v, c_ctx, norm1_g, norm2_g, w_mod, b_mod, w_in, diff_lambda, diff_subln_g, w_pool, pool_scale, sink, w_out, w_router, b_router, w1, b1, w2, b2, final_g):
    depth = w_in.shape[0]
    bc, lc, d = x_prompt.shape
    bl, ll, _ = x_sample.shape
    n_ctx, n_lat = bc * lc, bl * ll
    n_tok = n_ctx + n_lat
    n_blocks = -(-n_tok * TOP_K // MOE_ROWS) + N_EXPERTS
    n_rows = n_blocks * MOE_ROWS

    mod_rows = -(-(1 + bl) // 8) * 8
    cmat = jnp.zeros((mod_rows, d), F32).at[0].set(c_ctx).at[1:1 + bl].set(c)
    mod = _mod_vectors(cmat, w_mod, b_mod)
    tabs_a = _rope_tables(ll, A_HD, A_WIDTH)
    tabs_c = _rope_tables(ll, C_HD, C_WIDTH)
    tables = (tabs_a[0], tabs_a[1], tabs_c[0], tabs_c[1])
    fg = final_g.reshape(1, d)

    xp, xs_lat = x_prompt, x_sample
    new_cache = [[], [], [], []]
    for i in range(depth):
        lam_init = 0.8 - 0.6 * math.exp(-0.3 * i)
        mv = lambda rows, j: mod[i, rows, j * d:(j + 1) * d].reshape(-1, 1, d)
        ctx_rows, lat_rows = slice(0, 1), slice(1, 1 + bl)
        w_ctx, w_lat = _layer_weights(w_in[i])
        n1 = norm1_g[i].reshape(1, d)
        n2 = norm2_g[i].reshape(1, d)
        wbd = jax.scipy.linalg.block_diag(*[w_pool[i, g] for g in range(B_GROUPS)]).astype(BF16)
        ps = pool_scale[i].reshape(1, B_WIDTH)
        g_tiled = jnp.tile(diff_subln_g[i], A_HEADS).reshape(1, A_WIDTH)
        wo = w_out[i].astype(BF16)
        wr = jnp.zeros((d, LANES), F32).at[:, :N_EXPERTS].set(w_router[i])
        br = jnp.zeros((1, LANES), F32).at[0, :N_EXPERTS].set(b_router[i])

        qa, ka, va, u, qc, kr, vr, ka32, va32, kc32, vc32 = _inproj(
            xp, mv(ctx_rows, 0), mv(ctx_rows, 1), n1, w_ctx, None)
        new_cache[0].append(ka32.reshape(bc, lc, 2 * A_HEADS, A_HD))
        new_cache[1].append(va32.reshape(bc, lc, A_HEADS, 2 * A_HD))
        new_cache[2].append(kc32.reshape(bc, lc, C_KV, C_HD))
        new_cache[3].append(vc32.reshape(bc, lc, C_KV, C_HD))
        ob = _pool(u, wbd, ps)
        oa = _diff_attn(qa, [(ka, va)], diff_lambda[i], g_tiled, lam_init)
        oc = _gqa(sink[i], qc, kr, vr)
        x1_c, h2_c, idx_c, gate_c, rank_c, cnt_c = _finish(
            xp, oa, ob, oc, wo, mv(ctx_rows, 2), mv(ctx_rows, 3), mv(ctx_rows, 4), n2, wr, br,
            jnp.zeros((1, LANES), F32))

        qa, ka, va, u, qc, kr, vr = _inproj(xs_lat, mv(lat_rows, 0), mv(lat_rows, 1), n1, w_lat, tables)
        ob = _pool(u, wbd, ps)
        dk = cache_diff_k[:, i].reshape(bl, -1, A_WIDTH).astype(BF16)
        dv = cache_diff_v[:, i].reshape(bl, -1, A_WIDTH).astype(BF16)
        oa = _diff_attn(qa, [(ka, va), (dk, dv)], diff_lambda[i], g_tiled, lam_init)
        wk = _rep_heads(cache_win_k[:, i]).astype(BF16)
        wv = _rep_heads(cache_win_v[:, i]).astype(BF16)
        oc = _gqa(sink[i], qc, kr, vr, (wk, wv))
        x1_l, h2_l, idx_l, gate_l, rank_l, cnt = _finish(
            xs_lat, oa, ob, oc, wo, mv(lat_rows, 2), mv(lat_rows, 3), mv(lat_rows, 4), n2, wr, br, cnt_c)

        counts = cnt[0, :N_EXPERTS].astype(I32)
        padded = (counts + MOE_ROWS - 1) // MOE_ROWS * MOE_ROWS
        pad_end = jnp.cumsum(padded)
        pad_start = pad_end - padded
        dest_c = (pad_start[idx_c[..., :TOP_K]] + rank_c[..., :TOP_K]).reshape(-1)
        dest_l = (pad_start[idx_l[..., :TOP_K]] + rank_l[..., :TOP_K]).reshape(-1)
        block_expert = jnp.minimum(
            jnp.searchsorted(pad_end, jnp.arange(n_blocks, dtype=I32) * MOE_ROWS, side='right'),
            N_EXPERTS - 1).astype(I32)
        n_used = (pad_end[-1:] // MOE_ROWS).astype(I32)

        rows_in = jnp.zeros((n_rows, d), F32)
        rows_in = _dispatch(dest_c, h2_c.reshape(n_ctx, d), rows_in)
        rows_in = _dispatch(dest_l, h2_l.reshape(n_lat, d), rows_in)
        rows_out = _experts(block_expert, n_used, rows_in, w1[i], b1[i], w2[i], b2[i])
        final = i == depth - 1
        xp = _combine(dest_c, rows_out, gate_c, x1_c, mv(ctx_rows, 5), fg, final)
        xs_lat = _combine(dest_l, rows_out, gate_l, x1_l, mv(lat_rows, 5), fg, final)

    return (xp, xs_lat) + tuple(jnp.stack(parts, axis=1) for parts in new_cache)
```

```python
import functools
import math

import numpy as np
import jax
import jax.numpy as jnp
from jax import lax
from jax.experimental import pallas as pl
from jax.experimental.pallas import tpu as pltpu

F32 = jnp.float32
BF16 = jnp.bfloat16
I32 = jnp.int32

D_MODEL = 1024
GRID_W = 64
ROPE_BASE = 10000.0
NORM_EPS = 1e-6
A_HD = 32
A_HEADS = 4
A_WIDTH = 256
B_WIDTH = 256
B_GROUPS = 4
B_GC = 64
C_HD = 64
C_HEADS = 8
C_KV = 2
C_GROUP = 4
C_WIDTH = 512
WINDOW = 128
QBLOCK = 128
N_EXPERTS = 32
TOP_K = 4
D_FF = 1024
SWIGLU_LIMIT = 7.0
SWIGLU_ALPHA = 1.702

LANES = 128
ROW_TILE = 256
MOE_ROWS = 256
KEY_CHUNK = 512
NEG = -1e30
LOG2E = math.log2(math.e)
VMEM_LIMIT = 56 * 1024 * 1024


def _params(*sem):
    return pltpu.CompilerParams(dimension_semantics=sem, vmem_limit_bytes=VMEM_LIMIT)


def _split(x):
    hi = x.astype(BF16)
    lo = (x - hi.astype(F32)).astype(BF16)
    return hi, lo


def _dot3(a, b):
    ah, al = _split(a)
    bh, bl = _split(b)
    return (jnp.dot(ah, bh, preferred_element_type=F32)
            + (jnp.dot(ah, bl, preferred_element_type=F32)
               + jnp.dot(al, bh, preferred_element_type=F32)))


def _nt_dot(a, b):
    return lax.dot_general(a, b, (((1,), (1,)), ((), ())), preferred_element_type=F32)


def _rms(x):
    return x * lax.rsqrt(jnp.mean(x * x, axis=-1, keepdims=True) + NORM_EPS)


def _mod_kernel(c_ref, w_ref, b_ref, o_ref):
    c = c_ref[...]
    a = c * (1.0 / (1.0 + jnp.exp(-c)))
    o_ref[0] = _dot3(a, w_ref[0]) + b_ref[0]


def _mod_vectors(cmat, w_mod, b_mod):
    depth, d, e = w_mod.shape
    rows = cmat.shape[0]
    tn = 512
    return pl.pallas_call(
        _mod_kernel,
        grid=(depth, e // tn),
        in_specs=[pl.BlockSpec((rows, d), lambda l, j: (0, 0)),
                  pl.BlockSpec((1, d, tn), lambda l, j: (l, 0, j)),
                  pl.BlockSpec((1, 1, tn), lambda l, j: (l, 0, j))],
        out_specs=pl.BlockSpec((1, rows, tn), lambda l, j: (l, 0, j)),
        out_shape=jax.ShapeDtypeStruct((depth, rows, e), F32),
        compiler_params=_params("arbitrary", "arbitrary"),
    )(cmat, w_mod, b_mod.reshape(depth, 1, e))


def _rope(z, col_ref, row_ref, nf, r0):
    tm, w = z.shape
    outs = []
    for g in range(tm // GRID_W):
        zs = z[GRID_W * g:GRID_W * (g + 1), :]
        c = col_ref[0] + row_ref[0, pl.ds(r0 + g, 1), :]
        sm = col_ref[1] + row_ref[1, pl.ds(r0 + g, 1), :]
        sp = col_ref[2] + row_ref[2, pl.ds(r0 + g, 1), :]
        outs.append(zs * c + pltpu.roll(zs, w - nf, 1) * sm + pltpu.roll(zs, nf, 1) * sp)
    return jnp.concatenate(outs, axis=0)


def _inproj_kernel(*refs, rope, tm):
    x_ref, sh_ref, sc_ref, n1_ref, w_ref = refs[:5]
    if rope:
        ta_col, ta_row, tc_col, tc_row = refs[5:9]
        qa_o, ka_o, va_o, u_o, qc_o, kr_o, vr_o = refs[9:]
    else:
        qa_o, ka_o, va_o, u_o, qc_o, kr_o, vr_o, ka32_o, va32_o, kc32_o, vc32_o = refs[5:]
    x = x_ref[0]
    h = (_rms(x) * n1_ref[...]) * (1.0 + sc_ref[0]) + sh_ref[0]
    hb = h.astype(BF16)

    def seg(a, b):
        return jnp.dot(hb, w_ref[:, a:b], preferred_element_type=F32)

    qa, ka, va, u = seg(0, 256), seg(256, 512), seg(512, 768), seg(768, 1024)
    qc, kr, vr = seg(1024, 1536), seg(1536, 2048), seg(2048, 2560)
    if rope:
        r0 = pl.program_id(1) * (tm // GRID_W)
        qa = _rope(qa, ta_col, ta_row, A_HD // 4, r0)
        ka = _rope(ka, ta_col, ta_row, A_HD // 4, r0)
        qc = _rope(qc, tc_col, tc_row, C_HD // 4, r0)
        kr = _rope(kr, tc_col, tc_row, C_HD // 4, r0)
    else:
        ka32_o[0] = ka
        va32_o[0] = va
        kc32_o[0] = seg(2560, 2688)
        vc32_o[0] = seg(2688, 2816)
    qa_o[0] = (qa * (A_HD ** -0.5 * LOG2E)).astype(BF16)
    ka_o[0] = ka.astype(BF16)
    va_o[0] = va.astype(BF16)
    u_o[0] = u
    qc_o[0] = (qc * (C_HD ** -0.5)).astype(BF16)
    kr_o[0] = kr.astype(BF16)
    vr_o[0] = vr.astype(BF16)


def _inproj(x, sh, sc, n1, w, tables):
    b, l, d = x.shape
    tm = min(ROW_TILE, l)
    rope = tables is not None
    per_batch = sh.shape[0] > 1
    bidx = (lambda i, t: (i, 0, 0)) if per_batch else (lambda i, t: (0, 0, 0))
    tok = lambda width: pl.BlockSpec((1, tm, width), lambda i, t: (i, t, 0))
    in_specs = [tok(d),
                pl.BlockSpec((1, 1, d), bidx), pl.BlockSpec((1, 1, d), bidx),
                pl.BlockSpec((1, d), lambda i, t: (0, 0)),
                pl.BlockSpec(w.shape, lambda i, t: (0, 0))]
    args = [x, sh, sc, n1, w]
    widths = [(256, BF16), (256, BF16), (256, BF16), (256, F32), (512, BF16), (512, BF16), (512, BF16)]
    if rope:
        for tab in tables:
            in_specs.append(pl.BlockSpec(tab.shape, lambda i, t: (0, 0, 0)))
            args.append(tab)
    else:
        widths += [(256, F32), (256, F32), (128, F32), (128, F32)]
    return pl.pallas_call(
        functools.partial(_inproj_kernel, rope=rope, tm=tm),
        grid=(b, l // tm),
        in_specs=in_specs,
        out_specs=[tok(wd) for wd, _ in widths],
        out_shape=[jax.ShapeDtypeStruct((b, l, wd), dt) for wd, dt in widths],
        compiler_params=_params("arbitrary", "arbitrary"),
    )(*args)


def _rope_tables(n_lat, head_dim, width):
    rows = n_lat // GRID_W
    nf = head_dim // 4
    inv = ROPE_BASE ** (-jnp.arange(nf, dtype=F32) / nf)
    lane = np.arange(width) % head_dim
    half = lane // (2 * nf)
    pair = (lane // nf) % 2
    f = lane % nf

    def part(pos, which):
        ang = pos[:, None] * inv[f][None, :]
        on = jnp.asarray(half == which, F32)[None, :]
        c = jnp.cos(ang) * on
        s = jnp.sin(ang) * on
        sm = -s * jnp.asarray(pair == 0, F32)[None, :]
        sp = s * jnp.asarray(pair == 1, F32)[None, :]
        return jnp.stack([c, sm, sp])

    return part(jnp.arange(GRID_W, dtype=F32), 1), part(jnp.arange(rows, dtype=F32), 0)


POOL_PAD = 8
POOL_CHUNK = 256


def _pool_kernel(u_ref, w_ref, ps_ref, o_ref, pad_ref, *, l):
    zeros = jnp.zeros((POOL_PAD, B_WIDTH), F32)
    pad_ref[0:POOL_PAD, :] = zeros
    pad_ref[POOL_PAD + l:2 * POOL_PAD + l, :] = zeros
    pad_ref[POOL_PAD:POOL_PAD + l, :] = u_ref[0]
    ch = min(POOL_CHUNK, l)
    lane = lax.broadcasted_iota(I32, (ch, B_WIDTH), 1)
    grp = lane >> 6
    half = jnp.where(grp == 0, 1, jnp.where(grp == 1, 2, jnp.where(grp == 2, 4, 8)))
    row = lax.broadcasted_iota(I32, (ch, B_WIDTH), 0)
    for c in range(0, l, ch):
        ld = lambda k: pad_ref[c + POOL_PAD + k:c + POOL_PAD + k + ch, :]
        cur = ld(0)
        s2 = ld(-1) + cur
        s4 = s2 + (ld(-2) + ld(1))
        s8 = s4 + ((ld(-4) + ld(-3)) + (ld(2) + ld(3)))
        s16 = s8 + (((ld(-8) + ld(-7)) + (ld(-6) + ld(-5))) + ((ld(4) + ld(5)) + (ld(6) + ld(7))))
        win = jnp.where(grp == 0, s2, jnp.where(grp == 1, s4, jnp.where(grp == 2, s8, s16)))
        t = row + c
        cnt = (jnp.minimum(t + half, l) - jnp.maximum(t - half, 0)).astype(F32)
        r = (win / cnt - cur).astype(BF16)
        y = jnp.dot(r, w_ref[...], preferred_element_type=F32) * ps_ref[...]
        o_ref[0, c:c + ch, :] = y.astype(BF16)


def _pool(u, wbd, ps):
    b, l, w = u.shape
    return pl.pallas_call(
        functools.partial(_pool_kernel, l=l),
        grid=(b,),
        in_specs=[pl.BlockSpec((1, l, w), lambda i: (i, 0, 0)),
                  pl.BlockSpec((w, w), lambda i: (0, 0)),
                  pl.BlockSpec((1, w), lambda i: (0, 0))],
        out_specs=pl.BlockSpec((1, l, w), lambda i: (i, 0, 0)),
        out_shape=jax.ShapeDtypeStruct((b, l, w), BF16),
        scratch_shapes=[pltpu.VMEM((l + 2 * POOL_PAD, w), F32)],
        compiler_params=_params("arbitrary"),
    )(u, wbd, ps)


def _diff_attn_kernel(q_ref, k_ref, v_ref, lamp_ref, g_ref, o_ref, s_scr, p_scr, acc_scr, *, tq, s_len, lam_init):
    tk = min(s_len, KEY_CHUNK)
    q32 = q_ref[0].astype(F32)
    lp = lamp_ref[...]
    lam = (jnp.exp(jnp.sum(lp[0:1] * lp[1:2], axis=-1, keepdims=True))
           - jnp.exp(jnp.sum(lp[2:3] * lp[3:4], axis=-1, keepdims=True)) + lam_init)
    lane_row = lax.broadcasted_iota(I32, (1, A_WIDTH), 1)
    lane = lax.broadcasted_iota(I32, (tq, A_WIDTH), 1)
    acc_scr[...] = jnp.zeros_like(acc_scr)

    def head(h, carry):
        row_max = []
        for mp in range(2):
            qm = (q32 * ((lane_row >> 5) == 2 * h + mp).astype(F32)).astype(BF16)
            part = jnp.full((tq, LANES), NEG, F32)
            for c in range(0, s_len, tk):
                s = _nt_dot(qm, k_ref[0, c:c + tk, :])
                s_scr[mp, :, c:c + tk] = s
                for j in range(0, tk, LANES):
                    part = jnp.maximum(part, s[:, j:j + LANES])
            row_max.append(jnp.max(part, axis=-1, keepdims=True))
        maps = []
        for mp in range(2):
            part = jnp.zeros((tq, LANES), F32)
            for c in range(0, s_len, tk):
                p = jnp.exp2(s_scr[mp, :, c:c + tk] - row_max[mp])
                for j in range(0, tk, LANES):
                    part = part + p[:, j:j + LANES]
                p_scr[mp, :, c:c + tk] = p.astype(BF16)
            den = jnp.sum(part, axis=-1, keepdims=True)
            maps.append(jnp.dot(p_scr[mp], v_ref[0], preferred_element_type=F32) / den)
        acc_scr[...] += jnp.where((lane >> 6) == h, maps[0] - lam * maps[1], 0.0)
        return carry

    lax.fori_loop(0, A_HEADS, head, 0)
    out = acc_scr[...]
    sq = out * out
    rs = jnp.zeros((tq, A_WIDTH), F32)
    for h in range(A_HEADS):
        msk = (lane >> 6) == h
        ms = jnp.sum(jnp.where(msk, sq, 0.0), axis=-1, keepdims=True) * (1.0 / (2 * A_HD))
        rs = rs + jnp.where(msk, lax.rsqrt(ms + NORM_EPS), 0.0)
    o_ref[0] = (((out * rs) * g_ref[...]) * (1.0 - lam_init)).astype(BF16)


def _diff_attn(q, k, v, lam_p, g_tiled, lam_init):
    b, l, w = q.shape
    s_len = k.shape[1]
    tq = min(ROW_TILE, l)
    whole = pl.BlockSpec((1, s_len, w), lambda i, t: (i, 0, 0))
    return pl.pallas_call(
        functools.partial(_diff_attn_kernel, tq=tq, s_len=s_len, lam_init=lam_init),
        grid=(b, l // tq),
        in_specs=[pl.BlockSpec((1, tq, w), lambda i, t: (i, t, 0)), whole, whole,
                  pl.BlockSpec(lam_p.shape, lambda i, t: (0, 0)),
                  pl.BlockSpec(g_tiled.shape, lambda i, t: (0, 0))],
        out_specs=pl.BlockSpec((1, tq, w), lambda i, t: (i, t, 0)),
        out_shape=jax.ShapeDtypeStruct((b, l, w), BF16),
        scratch_shapes=[pltpu.VMEM((2, tq, s_len), F32), pltpu.VMEM((2, tq, s_len), BF16),
                        pltpu.VMEM((tq, w), F32)],
        compiler_params=_params("arbitrary", "arbitrary"),
    )(q, k, v, lam_p, g_tiled)


GQA_SLAB = C_GROUP * C_HD


def _gqa_kernel(*refs, windowed, tq, l):
    if windowed:
        sink_ref, q_ref, k_ref, v_ref, kc_ref, vc_ref, o_ref = refs
    else:
        sink_ref, q_ref, k_ref, v_ref, o_ref = refs
    i = pl.program_id(1)
    rows = C_GROUP * tq
    shift = int(math.log2(tq))
    q32 = q_ref[0].astype(F32)
    lane_row = lax.broadcasted_iota(I32, (1, GQA_SLAB), 1)
    lane = lax.broadcasted_iota(I32, (tq, GQA_SLAB), 1)
    rid = lax.broadcasted_iota(I32, (rows, 1), 0)
    for g in range(C_KV):
        sl = slice(GQA_SLAB * g, GQA_SLAB * (g + 1))
        qg = q32[:, sl]
        qs = jnp.concatenate(
            [(qg * ((lane_row >> 6) == hh).astype(F32)).astype(BF16) for hh in range(C_GROUP)], axis=0)
        sk = jnp.zeros((rows, 1), F32)
        for hh in range(C_GROUP):
            sk = jnp.where((rid >> shift) == hh, sink_ref[C_GROUP * g + hh], sk)
        if windowed:
            ws = pl.multiple_of(jnp.clip((i - 1) * tq, 0, l - 3 * tq), tq)
            s_loc = _nt_dot(qs, k_ref[0, pl.ds(ws, 3 * tq), sl])
            qpos = i * tq + (rid & (tq - 1))
            kpos = ws + lax.broadcasted_iota(I32, (1, 3 * tq), 1)
            s_loc = jnp.where(jnp.abs(kpos - qpos) <= WINDOW, s_loc, NEG)
            s_ctx = _nt_dot(qs, kc_ref[0, :, sl])
            m = jnp.maximum(sk, jnp.maximum(jnp.max(s_loc, axis=-1, keepdims=True),
                                            jnp.max(s_ctx, axis=-1, keepdims=True)))
            p_loc = jnp.exp(s_loc - m)
            p_ctx = jnp.exp(s_ctx - m)
            den = (jnp.exp(sk - m) + jnp.sum(p_loc, axis=-1, keepdims=True)
                   + jnp.sum(p_ctx, axis=-1, keepdims=True))
            pv = (jnp.dot(p_loc.astype(BF16), v_ref[0, pl.ds(ws, 3 * tq), sl], preferred_element_type=F32)
                  + jnp.dot(p_ctx.astype(BF16), vc_ref[0, :, sl], preferred_element_type=F32))
        else:
            s = _nt_dot(qs, k_ref[0, :, sl])
            m = jnp.maximum(sk, jnp.max(s, axis=-1, keepdims=True))
            p = jnp.exp(s - m)
            den = jnp.exp(sk - m) + jnp.sum(p, axis=-1, keepdims=True)
            pv = jnp.dot(p.astype(BF16), v_ref[0, :, sl], preferred_element_type=F32)
        o = pv / den
        og = jnp.zeros((tq, GQA_SLAB), F32)
        for hh in range(C_GROUP):
            og = og + jnp.where((lane >> 6) == hh, o[hh * tq:(hh + 1) * tq, :], 0.0)
        o_ref[0, :, sl] = og.astype(BF16)


def _gqa(sink, q, k, v, cache=None):
    b, l, w = q.shape
    windowed = cache is not None
    tq = QBLOCK
    whole = lambda a: pl.BlockSpec((1,) + a.shape[1:], lambda i, t: (i, 0, 0))
    in_specs = [pl.BlockSpec(memory_space=pltpu.SMEM),
                pl.BlockSpec((1, tq, w), lambda i, t: (i, t, 0)), whole(k), whole(v)]
    args = [sink, q, k, v]
    if windowed:
        in_specs += [whole(cache[0]), whole(cache[1])]
        args += list(cache)
    return pl.pallas_call(
        functools.partial(_gqa_kernel, windowed=windowed, tq=tq, l=l),
        grid=(b, l // tq),
        in_specs=in_specs,
        out_specs=pl.BlockSpec((1, tq, w), lambda i, t: (i, t, 0)),
        out_shape=jax.ShapeDtypeStruct((b, l, w), BF16),
        compiler_params=_params("arbitrary", "arbitrary"),
    )(*args)


def _finish_kernel(x_ref, oa_ref, ob_ref, oc_ref, wo_ref, g1_ref, sh_ref, sc_ref, n2_ref, wr_ref, br_ref,
                   base_ref, x1_o, h2_o, idx_o, gate_o, rank_o, cnt_o, run_ref, *, tm):
    first = (pl.program_id(0) == 0) & (pl.program_id(1) == 0)

    @pl.when(first)
    def _():
        run_ref[...] = base_ref[...]

    y = (jnp.dot(oa_ref[0], wo_ref[0:256, :], preferred_element_type=F32)
         + jnp.dot(ob_ref[0], wo_ref[256:512, :], preferred_element_type=F32)
         + jnp.dot(oc_ref[0], wo_ref[512:1024, :], preferred_element_type=F32))
    x1 = x_ref[0] + g1_ref[0] * y
    x1_o[0] = x1
    h2 = (_rms(x1) * n2_ref[...]) * (1.0 + sc_ref[0]) + sh_ref[0]
    h2_o[0] = h2
    lane = lax.broadcasted_iota(I32, (tm, LANES), 1)
    lanef = lane.astype(F32)
    logits = jnp.where(lane < N_EXPERTS, _dot3(h2, wr_ref[...]) + br_ref[...], NEG)
    work = logits
    tops, sels = [], []
    for k in range(TOP_K):
        mk = jnp.max(work, axis=-1, keepdims=True)
        ik = jnp.min(jnp.where(work == mk, lanef, float(LANES)), axis=-1, keepdims=True)
        sel = lanef == ik
        work = jnp.where(sel, 2.0 * NEG, work)
        tops.append((mk, ik))
        sels.append(sel)
    es = [jnp.exp(mk - tops[0][0]) for mk, _ in tops]
    den = (es[0] + es[1]) + (es[2] + es[3])
    multi = jnp.zeros((tm, LANES), F32)
    for sel in sels:
        multi = multi + sel.astype(F32)
    r_i = lax.broadcasted_iota(I32, (tm, tm), 0)
    c_i = lax.broadcasted_iota(I32, (tm, tm), 1)
    before = (c_i < r_i).astype(BF16)
    prior = jnp.dot(before, multi.astype(BF16), preferred_element_type=F32) + run_ref[...]
    idx = jnp.zeros((tm, LANES), F32)
    gate = jnp.zeros((tm, LANES), F32)
    rank = jnp.zeros((tm, LANES), F32)
    for k in range(TOP_K):
        slot = lane == k
        idx = jnp.where(slot, tops[k][1], idx)
        gate = jnp.where(slot, es[k] / den, gate)
        rk = jnp.sum(jnp.where(sels[k], prior, 0.0), axis=-1, keepdims=True)
        rank = jnp.where(slot, rk, rank)
    idx_o[0] = idx.astype(I32)
    gate_o[0] = gate
    rank_o[0] = rank.astype(I32)
    total = run_ref[...] + jnp.sum(multi, axis=0, keepdims=True)
    run_ref[...] = total
    cnt_o[...] = total


def _finish(x, oa, ob, oc, wo, g1, sh2, sc2, n2, wr, br, base):
    b, l, d = x.shape
    tm = min(ROW_TILE, l)
    per_batch = g1.shape[0] > 1
    bidx = (lambda i, t: (i, 0, 0)) if per_batch else (lambda i, t: (0, 0, 0))
    tok = lambda width: pl.BlockSpec((1, tm, width), lambda i, t: (i, t, 0))
    const2 = lambda a: pl.BlockSpec(a.shape, lambda i, t: (0, 0))
    vec = pl.BlockSpec((1, 1, d), bidx)
    return pl.pallas_call(
        functools.partial(_finish_kernel, tm=tm),
        grid=(b, l // tm),
        in_specs=[tok(d), tok(256), tok(256), tok(512), const2(wo), vec, vec, vec,
                  const2(n2), const2(wr), const2(br), const2(base)],
        out_specs=[tok(d), tok(d), tok(LANES), tok(LANES), tok(LANES),
                   pl.BlockSpec((1, LANES), lambda i, t: (0, 0))],
        out_shape=[jax.ShapeDtypeStruct((b, l, d), F32), jax.ShapeDtypeStruct((b, l, d), F32),
                   jax.ShapeDtypeStruct((b, l, LANES), I32), jax.ShapeDtypeStruct((b, l, LANES), F32),
                   jax.ShapeDtypeStruct((b, l, LANES), I32), jax.ShapeDtypeStruct((1, LANES), F32)],
        scratch_shapes=[pltpu.VMEM((1, LANES), F32)],
        compiler_params=_params("arbitrary", "arbitrary"),
    )(x, oa, ob, oc, wo, g1, sh2, sc2, n2, wr, br, base)


def _dispatch_kernel(dest_ref, h_ref, xs_in_ref, xs_ref, sem, *, tm):
    del xs_in_ref
    base = pl.program_id(0) * (tm * TOP_K)

    def body(t, carry):
        for k in range(TOP_K):
            d = dest_ref[base + t * TOP_K + k]
            pltpu.make_async_copy(h_ref.at[pl.ds(t, 1)], xs_ref.at[pl.ds(d, 1)], sem).start()
        return carry

    lax.fori_loop(0, tm, body, 0)
    for _ in range(TOP_K):
        pltpu.make_async_copy(h_ref, xs_ref.at[pl.ds(0, tm)], sem).wait()


def _dispatch(dest_flat, h2, xs):
    n, d = h2.shape
    tm = ROW_TILE
    return pl.pallas_call(
        functools.partial(_dispatch_kernel, tm=tm),
        grid_spec=pltpu.PrefetchScalarGridSpec(
            num_scalar_prefetch=1,
            grid=(n // tm,),
            in_specs=[pl.BlockSpec((tm, d), lambda i, dest: (i, 0)),
                      pl.BlockSpec(memory_space=pl.ANY)],
            out_specs=pl.BlockSpec(memory_space=pl.ANY),
            scratch_shapes=[pltpu.SemaphoreType.DMA]),
        out_shape=jax.ShapeDtypeStruct(xs.shape, xs.dtype),
        input_output_aliases={2: 0},
        compiler_params=_params("arbitrary"),
    )(dest_flat, h2, xs)


def _expert_kernel(be_ref, nu_ref, xs_ref, w1_ref, b1_ref, w2_ref, b2_ref, o_ref, w1b, w2b):
    i = pl.program_id(0)
    e = be_ref[i]
    prev = be_ref[jnp.maximum(i - 1, 0)]

    @pl.when((i == 0) | (e != prev))
    def _():
        w1b[...] = w1_ref[0].astype(BF16)
        w2b[...] = w2_ref[0].astype(BF16)

    @pl.when(i < nu_ref[0])
    def _():
        gu = jnp.dot(xs_ref[...].astype(BF16), w1b[...], preferred_element_type=F32) + b1_ref[0]
        gt = jnp.minimum(gu[:, :D_FF], SWIGLU_LIMIT)
        up = jnp.clip(gu[:, D_FF:], -SWIGLU_LIMIT, SWIGLU_LIMIT)
        hid = (up + 1.0) * gt * (1.0 / (1.0 + jnp.exp(-SWIGLU_ALPHA * gt)))
        o_ref[...] = jnp.dot(hid.astype(BF16), w2b[...], preferred_element_type=F32) + b2_ref[0]

    @pl.when(i >= nu_ref[0])
    def _():
        o_ref[...] = jnp.zeros_like(o_ref)


def _experts(block_expert, n_used, xs, w1, b1, w2, b2):
    n_rows, d = xs.shape
    n_blocks = n_rows // MOE_ROWS
    f2 = w1.shape[-1]
    ne = w1.shape[0] * w1.shape[1]
    w1 = w1.reshape(ne, d, f2)
    w2 = w2.reshape(ne, D_FF, d)
    row = lambda i, be, nu: (jnp.minimum(i, nu[0] - 1), 0)
    return pl.pallas_call(
        _expert_kernel,
        grid_spec=pltpu.PrefetchScalarGridSpec(
            num_scalar_prefetch=2,
            grid=(n_blocks,),
            in_specs=[pl.BlockSpec((MOE_ROWS, d), row),
                      pl.BlockSpec((1, d, f2), lambda i, be, nu: (be[i], 0, 0)),
                      pl.BlockSpec((1, 1, f2), lambda i, be, nu: (be[i], 0, 0)),
                      pl.BlockSpec((1, D_FF, d), lambda i, be, nu: (be[i], 0, 0)),
                      pl.BlockSpec((1, 1, d), lambda i, be, nu: (be[i], 0, 0))],
            out_specs=pl.BlockSpec((MOE_ROWS, d), lambda i, be, nu: (i, 0)),
            scratch_shapes=[pltpu.VMEM((d, f2), BF16), pltpu.VMEM((D_FF, d), BF16)]),
        out_shape=jax.ShapeDtypeStruct((n_rows, d), F32),
        compiler_params=_params("arbitrary"),
    )(block_expert, n_used, xs, w1, b1.reshape(ne, 1, f2), w2, b2.reshape(ne, 1, d))


COMBINE_TILE = 128


def _combine_kernel(dest_ref, outs_ref, gate_ref, x1_ref, g2_ref, fg_ref, o_ref, buf, sem, *, tm, tiles, final):
    base = (pl.program_id(0) * tiles + pl.program_id(1)) * (tm * TOP_K)

    def body(t, carry):
        for k in range(TOP_K):
            d = dest_ref[base + t * TOP_K + k]
            pltpu.make_async_copy(outs_ref.at[pl.ds(d, 1)], buf.at[k, pl.ds(t, 1)], sem).start()
        return carry

    lax.fori_loop(0, tm, body, 0)
    for k in range(TOP_K):
        pltpu.make_async_copy(outs_ref.at[pl.ds(0, tm)], buf.at[k], sem).wait()
    gate = gate_ref[0]
    y = (gate[:, 0:1] * buf[0] + gate[:, 1:2] * buf[1]) + (gate[:, 2:3] * buf[2] + gate[:, 3:4] * buf[3])
    x2 = x1_ref[0] + g2_ref[0] * y
    if final:
        x2 = _rms(x2) * fg_ref[...]
    o_ref[0] = x2


def _combine(dest_flat, outs, gate, x1, g2, fg, final):
    b, l, d = x1.shape
    tm = COMBINE_TILE
    per_batch = g2.shape[0] > 1
    bidx = (lambda i, t, dest: (i, 0, 0)) if per_batch else (lambda i, t, dest: (0, 0, 0))
    tok = lambda width: pl.BlockSpec((1, tm, width), lambda i, t, dest: (i, t, 0))
    return pl.pallas_call(
        functools.partial(_combine_kernel, tm=tm, tiles=l // tm, final=final),
        grid_spec=pltpu.PrefetchScalarGridSpec(
            num_scalar_prefetch=1,
            grid=(b, l // tm),
            in_specs=[pl.BlockSpec(memory_space=pl.ANY), tok(LANES), tok(d),
                      pl.BlockSpec((1, 1, d), bidx),
                      pl.BlockSpec((1, d), lambda i, t, dest: (0, 0))],
            out_specs=tok(d),
            scratch_shapes=[pltpu.VMEM((TOP_K, tm, d), F32), pltpu.SemaphoreType.DMA]),
        out_shape=jax.ShapeDtypeStruct((b, l, d), F32),
        compiler_params=_params("arbitrary", "arbitrary"),
    )(dest_flat, outs, gate, x1, g2, fg)


def _layer_weights(w_in_l):
    cuts = np.cumsum([A_WIDTH, A_WIDTH, A_WIDTH, B_WIDTH, C_WIDTH, C_KV * C_HD]).tolist()
    front = w_in_l[:, :cuts[4]]
    kc = w_in_l[:, cuts[4]:cuts[5]]
    vc = w_in_l[:, cuts[5]:]

    def rep(w):
        return jnp.concatenate([w[:, C_HD * (j // C_GROUP):C_HD * (j // C_GROUP + 1)] for j in range(C_HEADS)], axis=1)

    lat = jnp.concatenate([front, rep(kc), rep(vc)], axis=1).astype(BF16)
    ctx = jnp.concatenate([front, rep(kc), rep(vc), kc, vc], axis=1).astype(BF16)
    return ctx, lat


def _rep_heads(a):
    return jnp.repeat(a, C_GROUP, axis=2).reshape(a.shape[0], a.shape[1], C_HEADS * C_HD)


def kernel(x_prompt, x_sample, c, cache_diff_k, cache_diff_v, cache_win_k, cache_win_v, c_ctx, norm1_g, norm2_g, w_mod, b_mod, w_in, diff_lambda, diff_subln_g, w_pool, pool_scale, sink, w_out, w_router, b_router, w1, b1, w2, b2, final_g):
    depth = w_in.shape[0]
    bc, lc, d = x_prompt.shape
    bl, ll, _ = x_sample.shape
    n_ctx, n_lat = bc * lc, bl * ll
    n_tok = n_ctx + n_lat
    n_blocks = -(-n_tok * TOP_K // MOE_ROWS) + N_EXPERTS
    n_rows = n_blocks * MOE_ROWS

    mod_rows = -(-(1 + bl) // 8) * 8
    cmat = jnp.zeros((mod_rows, d), F32).at[0].set(c_ctx).at[1:1 + bl].set(c)
    mod = _mod_vectors(cmat, w_mod, b_mod)
    tabs_a = _rope_tables(ll, A_HD, A_WIDTH)
    tabs_c = _rope_tables(ll, C_HD, C_WIDTH)
    tables = (tabs_a[0], tabs_a[1], tabs_c[0], tabs_c[1])
    fg = final_g.reshape(1, d)

    xp, xs_lat = x_prompt, x_sample
    new_cache = [[], [], [], []]
    for i in range(depth):
        lam_init = 0.8 - 0.6 * math.exp(-0.3 * i)
        mv = lambda rows, j: mod[i, rows, j * d:(j + 1) * d].reshape(-1, 1, d)
        ctx_rows, lat_rows = slice(0, 1), slice(1, 1 + bl)
        w_ctx, w_lat = _layer_weights(w_in[i])
        n1 = norm1_g[i].reshape(1, d)
        n2 = norm2_g[i].reshape(1, d)
        wbd = jax.scipy.linalg.block_diag(*[w_pool[i, g] for g in range(B_GROUPS)]).astype(BF16)
        ps = pool_scale[i].reshape(1, B_WIDTH)
        g_tiled = jnp.tile(diff_subln_g[i], A_HEADS).reshape(1, A_WIDTH)
        wo = w_out[i].astype(BF16)
        wr = jnp.zeros((d, LANES), F32).at[:, :N_EXPERTS].set(w_router[i])
        br = jnp.zeros((1, LANES), F32).at[0, :N_EXPERTS].set(b_router[i])

        qa, ka, va, u, qc, kr, vr, ka32, va32, kc32, vc32 = _inproj(
            xp, mv(ctx_rows, 0), mv(ctx_rows, 1), n1, w_ctx, None)
        new_cache[0].append(ka32.reshape(bc, lc, 2 * A_HEADS, A_HD))
        new_cache[1].append(va32.reshape(bc, lc, A_HEADS, 2 * A_HD))
        new_cache[2].append(kc32.reshape(bc, lc, C_KV, C_HD))
        new_cache[3].append(vc32.reshape(bc, lc, C_KV, C_HD))
        ob = _pool(u, wbd, ps)
        oa = _diff_attn(qa, ka, va, diff_lambda[i], g_tiled, lam_init)
        oc = _gqa(sink[i], qc, kr, vr)
        x1_c, h2_c, idx_c, gate_c, rank_c, cnt_c = _finish(
            xp, oa, ob, oc, wo, mv(ctx_rows, 2), mv(ctx_rows, 3), mv(ctx_rows, 4), n2, wr, br,
            jnp.zeros((1, LANES), F32))

        qa, ka, va, u, qc, kr, vr = _inproj(xs_lat, mv(lat_rows, 0), mv(lat_rows, 1), n1, w_lat, tables)
        ob = _pool(u, wbd, ps)
        dk = cache_diff_k[:, i].reshape(bl, -1, A_WIDTH).astype(BF16)
        dv = cache_diff_v[:, i].reshape(bl, -1, A_WIDTH).astype(BF16)
        oa = _diff_attn(qa, jnp.concatenate([ka, dk], axis=1), jnp.concatenate([va, dv], axis=1),
                        diff_lambda[i], g_tiled, lam_init)
        wk = _rep_heads(cache_win_k[:, i]).astype(BF16)
        wv = _rep_heads(cache_win_v[:, i]).astype(BF16)
        oc = _gqa(sink[i], qc, kr, vr, (wk, wv))
        x1_l, h2_l, idx_l, gate_l, rank_l, cnt = _finish(
            xs_lat, oa, ob, oc, wo, mv(lat_rows, 2), mv(lat_rows, 3), mv(lat_rows, 4), n2, wr, br, cnt_c)

        counts = cnt[0, :N_EXPERTS].astype(I32)
        padded = (counts + MOE_ROWS - 1) // MOE_ROWS * MOE_ROWS
        pad_end = jnp.cumsum(padded)
        pad_start = pad_end - padded
        experts = jnp.arange(N_EXPERTS, dtype=I32)

        def dest_rows(idx, rank):
            first = jnp.sum(jnp.where(idx[..., :TOP_K, None] == experts, pad_start, 0), axis=-1)
            return (first + rank[..., :TOP_K]).reshape(-1)

        dest_c = dest_rows(idx_c, rank_c)
        dest_l = dest_rows(idx_l, rank_l)
        block_row = jnp.arange(n_blocks, dtype=I32)[:, None] * MOE_ROWS
        block_expert = jnp.minimum(jnp.sum((pad_end[None, :] <= block_row).astype(I32), axis=-1), N_EXPERTS - 1)
        n_used = (pad_end[-1:] // MOE_ROWS).astype(I32)

        rows_in = jnp.zeros((n_rows, d), F32)
        rows_in = _dispatch(dest_c, h2_c.reshape(n_ctx, d), rows_in)
        rows_in = _dispatch(dest_l, h2_l.reshape(n_lat, d), rows_in)
        rows_out = _experts(block_expert + i * N_EXPERTS, n_used, rows_in, w1, b1, w2, b2)
        final = i == depth - 1
        xp = _combine(dest_c, rows_out, gate_c, x1_c, mv(ctx_rows, 5), fg, final)
        xs_lat = _combine(dest_l, rows_out, gate_l, x1_l, mv(lat_rows, 5), fg, final)

    return (xp, xs_lat) + tuple(jnp.stack(parts, axis=1) for parts in new_cache)
```

```python
import functools
import math

import numpy as np
import jax
import jax.numpy as jnp
from jax import lax
from jax.experimental import pallas as pl
from jax.experimental.pallas import tpu as pltpu

F32 = jnp.float32
BF16 = jnp.bfloat16
I32 = jnp.int32

D_MODEL = 1024
GRID_W = 64
ROPE_BASE = 10000.0
NORM_EPS = 1e-6
A_HD = 32
A_HEADS = 4
A_WIDTH = 256
B_WIDTH = 256
B_GROUPS = 4
B_GC = 64
C_HD = 64
C_HEADS = 8
C_KV = 2
C_GROUP = 4
C_WIDTH = 512
WINDOW = 128
QBLOCK = 128
N_EXPERTS = 32
TOP_K = 4
D_FF = 1024
SWIGLU_LIMIT = 7.0
SWIGLU_ALPHA = 1.702

LANES = 128
ROW_TILE = 256
MOE_ROWS = 256
KEY_CHUNK = 512
NEG = -1e30
LOG2E = math.log2(math.e)
VMEM_LIMIT = 56 * 1024 * 1024


def _params(*sem):
    return pltpu.CompilerParams(dimension_semantics=sem, vmem_limit_bytes=VMEM_LIMIT)


def _split(x):
    hi = x.astype(BF16)
    lo = (x - hi.astype(F32)).astype(BF16)
    return hi, lo


def _dot3(a, b):
    ah, al = _split(a)
    bh, bl = _split(b)
    return (jnp.dot(ah, bh, preferred_element_type=F32)
            + (jnp.dot(ah, bl, preferred_element_type=F32)
               + jnp.dot(al, bh, preferred_element_type=F32)))


def _nt_dot(a, b):
    return lax.dot_general(a, b, (((1,), (1,)), ((), ())), preferred_element_type=F32)


def _rms(x):
    return x * lax.rsqrt(jnp.mean(x * x, axis=-1, keepdims=True) + NORM_EPS)


def _mod_kernel(c_ref, w_ref, b_ref, o_ref):
    c = c_ref[...]
    a = c * (1.0 / (1.0 + jnp.exp(-c)))
    o_ref[0] = _dot3(a, w_ref[0]) + b_ref[0]


def _mod_vectors(cmat, w_mod, b_mod):
    depth, d, e = w_mod.shape
    rows = cmat.shape[0]
    tn = 512
    return pl.pallas_call(
        _mod_kernel,
        grid=(depth, e // tn),
        in_specs=[pl.BlockSpec((rows, d), lambda l, j: (0, 0)),
                  pl.BlockSpec((1, d, tn), lambda l, j: (l, 0, j)),
                  pl.BlockSpec((1, 1, tn), lambda l, j: (l, 0, j))],
        out_specs=pl.BlockSpec((1, rows, tn), lambda l, j: (l, 0, j)),
        out_shape=jax.ShapeDtypeStruct((depth, rows, e), F32),
        compiler_params=_params("arbitrary", "arbitrary"),
    )(cmat, w_mod, b_mod.reshape(depth, 1, e))


def _rope(z, col_ref, row_ref, nf, r0):
    tm, w = z.shape
    outs = []
    for g in range(tm // GRID_W):
        zs = z[GRID_W * g:GRID_W * (g + 1), :]
        c = col_ref[0] + row_ref[0, pl.ds(r0 + g, 1), :]
        sm = col_ref[1] + row_ref[1, pl.ds(r0 + g, 1), :]
        sp = col_ref[2] + row_ref[2, pl.ds(r0 + g, 1), :]
        outs.append(zs * c + pltpu.roll(zs, w - nf, 1) * sm + pltpu.roll(zs, nf, 1) * sp)
    return jnp.concatenate(outs, axis=0)


def _inproj_kernel(*refs, rope, tm):
    x_ref, sh_ref, sc_ref, n1_ref, w_ref = refs[:5]
    if rope:
        ta_col, ta_row, tc_col, tc_row = refs[5:9]
        qa_o, ka_o, va_o, u_o, qc_o, kr_o, vr_o = refs[9:]
    else:
        qa_o, ka_o, va_o, u_o, qc_o, kr_o, vr_o, ka32_o, va32_o, kc32_o, vc32_o = refs[5:]
    x = x_ref[0]
    h = (_rms(x) * n1_ref[...]) * (1.0 + sc_ref[0]) + sh_ref[0]
    hb = h.astype(BF16)

    def seg(a, b):
        return jnp.dot(hb, w_ref[:, a:b], preferred_element_type=F32)

    qa, ka, va, u = seg(0, 256), seg(256, 512), seg(512, 768), seg(768, 1024)
    qc, kr, vr = seg(1024, 1536), seg(1536, 2048), seg(2048, 2560)
    if rope:
        r0 = pl.program_id(1) * (tm // GRID_W)
        qa = _rope(qa, ta_col, ta_row, A_HD // 4, r0)
        ka = _rope(ka, ta_col, ta_row, A_HD // 4, r0)
        qc = _rope(qc, tc_col, tc_row, C_HD // 4, r0)
        kr = _rope(kr, tc_col, tc_row, C_HD // 4, r0)
    else:
        ka32_o[0] = ka
        va32_o[0] = va
        kc32_o[0] = seg(2560, 2688)
        vc32_o[0] = seg(2688, 2816)
    qa_o[0] = (qa * (A_HD ** -0.5 * LOG2E)).astype(BF16)
    ka_o[0] = ka.astype(BF16)
    va_o[0] = va.astype(BF16)
    u_o[0] = u
    qc_o[0] = (qc * (C_HD ** -0.5)).astype(BF16)
    kr_o[0] = kr.astype(BF16)
    vr_o[0] = vr.astype(BF16)


def _inproj(x, sh, sc, n1, w, tables):
    b, l, d = x.shape
    tm = min(ROW_TILE, l)
    rope = tables is not None
    per_batch = sh.shape[0] > 1
    bidx = (lambda i, t: (i, 0, 0)) if per_batch else (lambda i, t: (0, 0, 0))
    tok = lambda width: pl.BlockSpec((1, tm, width), lambda i, t: (i, t, 0))
    in_specs = [tok(d),
                pl.BlockSpec((1, 1, d), bidx), pl.BlockSpec((1, 1, d), bidx),
                pl.BlockSpec((1, d), lambda i, t: (0, 0)),
                pl.BlockSpec(w.shape, lambda i, t: (0, 0))]
    args = [x, sh, sc, n1, w]
    widths = [(256, BF16), (256, BF16), (256, BF16), (256, F32), (512, BF16), (512, BF16), (512, BF16)]
    if rope:
        for tab in tables:
            in_specs.append(pl.BlockSpec(tab.shape, lambda i, t: (0, 0, 0)))
            args.append(tab)
    else:
        widths += [(256, F32), (256, F32), (128, F32), (128, F32)]
    return pl.pallas_call(
        functools.partial(_inproj_kernel, rope=rope, tm=tm),
        grid=(b, l // tm),
        in_specs=in_specs,
        out_specs=[tok(wd) for wd, _ in widths],
        out_shape=[jax.ShapeDtypeStruct((b, l, wd), dt) for wd, dt in widths],
        compiler_params=_params("arbitrary", "arbitrary"),
    )(*args)


def _rope_tables(n_lat, head_dim, width):
    rows = n_lat // GRID_W
    nf = head_dim // 4
    inv = ROPE_BASE ** (-jnp.arange(nf, dtype=F32) / nf)
    lane = np.arange(width) % head_dim
    half = lane // (2 * nf)
    pair = (lane // nf) % 2
    f = lane % nf

    def part(pos, which):
        ang = pos[:, None] * inv[f][None, :]
        on = jnp.asarray(half == which, F32)[None, :]
        c = jnp.cos(ang) * on
        s = jnp.sin(ang) * on
        sm = -s * jnp.asarray(pair == 0, F32)[None, :]
        sp = s * jnp.asarray(pair == 1, F32)[None, :]
        return jnp.stack([c, sm, sp])

    return part(jnp.arange(GRID_W, dtype=F32), 1), part(jnp.arange(rows, dtype=F32), 0)


POOL_PAD = 8
POOL_CHUNK = 256


def _pool_kernel(u_ref, w_ref, ps_ref, o_ref, pad_ref, *, l):
    zeros = jnp.zeros((POOL_PAD, B_WIDTH), F32)
    pad_ref[0:POOL_PAD, :] = zeros
    pad_ref[POOL_PAD + l:2 * POOL_PAD + l, :] = zeros
    pad_ref[POOL_PAD:POOL_PAD + l, :] = u_ref[0]
    ch = min(POOL_CHUNK, l)
    lane = lax.broadcasted_iota(I32, (ch, B_WIDTH), 1)
    grp = lane >> 6
    half = jnp.where(grp == 0, 1, jnp.where(grp == 1, 2, jnp.where(grp == 2, 4, 8)))
    row = lax.broadcasted_iota(I32, (ch, B_WIDTH), 0)
    for c in range(0, l, ch):
        ld = lambda k: pad_ref[c + POOL_PAD + k:c + POOL_PAD + k + ch, :]
        cur = ld(0)
        s2 = ld(-1) + cur
        s4 = s2 + (ld(-2) + ld(1))
        s8 = s4 + ((ld(-4) + ld(-3)) + (ld(2) + ld(3)))
        s16 = s8 + (((ld(-8) + ld(-7)) + (ld(-6) + ld(-5))) + ((ld(4) + ld(5)) + (ld(6) + ld(7))))
        win = jnp.where(grp == 0, s2, jnp.where(grp == 1, s4, jnp.where(grp == 2, s8, s16)))
        t = row + c
        cnt = (jnp.minimum(t + half, l) - jnp.maximum(t - half, 0)).astype(F32)
        r = (win / cnt - cur).astype(BF16)
        y = jnp.dot(r, w_ref[...], preferred_element_type=F32) * ps_ref[...]
        o_ref[0, c:c + ch, :] = y.astype(BF16)


def _pool(u, wbd, ps):
    b, l, w = u.shape
    return pl.pallas_call(
        functools.partial(_pool_kernel, l=l),
        grid=(b,),
        in_specs=[pl.BlockSpec((1, l, w), lambda i: (i, 0, 0)),
                  pl.BlockSpec((w, w), lambda i: (0, 0)),
                  pl.BlockSpec((1, w), lambda i: (0, 0))],
        out_specs=pl.BlockSpec((1, l, w), lambda i: (i, 0, 0)),
        out_shape=jax.ShapeDtypeStruct((b, l, w), BF16),
        scratch_shapes=[pltpu.VMEM((l + 2 * POOL_PAD, w), F32)],
        compiler_params=_params("arbitrary"),
    )(u, wbd, ps)


def _diff_attn_kernel(q_ref, k_ref, v_ref, lamp_ref, g_ref, o_ref, s_scr, p_scr, acc_scr, *, tq, s_len, lam_init):
    tk = min(s_len, KEY_CHUNK)
    q32 = q_ref[0].astype(F32)
    lp = lamp_ref[...]
    lam = (jnp.exp(jnp.sum(lp[0:1] * lp[1:2], axis=-1, keepdims=True))
           - jnp.exp(jnp.sum(lp[2:3] * lp[3:4], axis=-1, keepdims=True)) + lam_init)
    lane_row = lax.broadcasted_iota(I32, (1, A_WIDTH), 1)
    lane = lax.broadcasted_iota(I32, (tq, A_WIDTH), 1)
    acc_scr[...] = jnp.zeros_like(acc_scr)

    def head(h, carry):
        row_max = []
        for mp in range(2):
            qm = (q32 * ((lane_row >> 5) == 2 * h + mp).astype(F32)).astype(BF16)
            part = jnp.full((tq, LANES), NEG, F32)
            for c in range(0, s_len, tk):
                s = _nt_dot(qm, k_ref[0, c:c + tk, :])
                s_scr[mp, :, c:c + tk] = s
                for j in range(0, tk, LANES):
                    part = jnp.maximum(part, s[:, j:j + LANES])
            row_max.append(jnp.max(part, axis=-1, keepdims=True))
        maps = []
        for mp in range(2):
            part = jnp.zeros((tq, LANES), F32)
            for c in range(0, s_len, tk):
                p = jnp.exp2(s_scr[mp, :, c:c + tk] - row_max[mp])
                for j in range(0, tk, LANES):
                    part = part + p[:, j:j + LANES]
                p_scr[mp, :, c:c + tk] = p.astype(BF16)
            den = jnp.sum(part, axis=-1, keepdims=True)
            maps.append(jnp.dot(p_scr[mp], v_ref[0], preferred_element_type=F32) / den)
        acc_scr[...] += jnp.where((lane >> 6) == h, maps[0] - lam * maps[1], 0.0)
        return carry

    lax.fori_loop(0, A_HEADS, head, 0)
    out = acc_scr[...]
    sq = out * out
    rs = jnp.zeros((tq, A_WIDTH), F32)
    for h in range(A_HEADS):
        msk = (lane >> 6) == h
        ms = jnp.sum(jnp.where(msk, sq, 0.0), axis=-1, keepdims=True) * (1.0 / (2 * A_HD))
        rs = rs + jnp.where(msk, lax.rsqrt(ms + NORM_EPS), 0.0)
    o_ref[0] = (((out * rs) * g_ref[...]) * (1.0 - lam_init)).astype(BF16)


def _diff_attn(q, k, v, lam_p, g_tiled, lam_init):
    b, l, w = q.shape
    s_len = k.shape[1]
    tq = min(ROW_TILE, l)
    whole = pl.BlockSpec((1, s_len, w), lambda i, t: (i, 0, 0))
    return pl.pallas_call(
        functools.partial(_diff_attn_kernel, tq=tq, s_len=s_len, lam_init=lam_init),
        grid=(b, l // tq),
        in_specs=[pl.BlockSpec((1, tq, w), lambda i, t: (i, t, 0)), whole, whole,
                  pl.BlockSpec(lam_p.shape, lambda i, t: (0, 0)),
                  pl.BlockSpec(g_tiled.shape, lambda i, t: (0, 0))],
        out_specs=pl.BlockSpec((1, tq, w), lambda i, t: (i, t, 0)),
        out_shape=jax.ShapeDtypeStruct((b, l, w), BF16),
        scratch_shapes=[pltpu.VMEM((2, tq, s_len), F32), pltpu.VMEM((2, tq, s_len), BF16),
                        pltpu.VMEM((tq, w), F32)],
        compiler_params=_params("arbitrary", "arbitrary"),
    )(q, k, v, lam_p, g_tiled)


GQA_SLAB = C_GROUP * C_HD


def _gqa_kernel(*refs, windowed, tq, l):
    if windowed:
        sink_ref, q_ref, k_ref, v_ref, kc_ref, vc_ref, o_ref = refs
    else:
        sink_ref, q_ref, k_ref, v_ref, o_ref = refs
    i = pl.program_id(1)
    rows = C_GROUP * tq
    shift = int(math.log2(tq))
    q32 = q_ref[0].astype(F32)
    lane_row = lax.broadcasted_iota(I32, (1, GQA_SLAB), 1)
    lane = lax.broadcasted_iota(I32, (tq, GQA_SLAB), 1)
    rid = lax.broadcasted_iota(I32, (rows, 1), 0)
    for g in range(C_KV):
        sl = slice(GQA_SLAB * g, GQA_SLAB * (g + 1))
        qg = q32[:, sl]
        qs = jnp.concatenate(
            [(qg * ((lane_row >> 6) == hh).astype(F32)).astype(BF16) for hh in range(C_GROUP)], axis=0)
        sk = jnp.zeros((rows, 1), F32)
        for hh in range(C_GROUP):
            sk = jnp.where((rid >> shift) == hh, sink_ref[C_GROUP * g + hh], sk)
        if windowed:
            ws = pl.multiple_of(jnp.clip((i - 1) * tq, 0, l - 3 * tq), tq)
            s_loc = _nt_dot(qs, k_ref[0, pl.ds(ws, 3 * tq), sl])
            qpos = i * tq + (rid & (tq - 1))
            kpos = ws + lax.broadcasted_iota(I32, (1, 3 * tq), 1)
            s_loc = jnp.where(jnp.abs(kpos - qpos) <= WINDOW, s_loc, NEG)
            s_ctx = _nt_dot(qs, kc_ref[0, :, sl])
            m = jnp.maximum(sk, jnp.maximum(jnp.max(s_loc, axis=-1, keepdims=True),
                                            jnp.max(s_ctx, axis=-1, keepdims=True)))
            p_loc = jnp.exp(s_loc - m)
            p_ctx = jnp.exp(s_ctx - m)
            den = (jnp.exp(sk - m) + jnp.sum(p_loc, axis=-1, keepdims=True)
                   + jnp.sum(p_ctx, axis=-1, keepdims=True))
            pv = (jnp.dot(p_loc.astype(BF16), v_ref[0, pl.ds(ws, 3 * tq), sl], preferred_element_type=F32)
                  + jnp.dot(p_ctx.astype(BF16), vc_ref[0, :, sl], preferred_element_type=F32))
        else:
            s = _nt_dot(qs, k_ref[0, :, sl])
            m = jnp.maximum(sk, jnp.max(s, axis=-1, keepdims=True))
            p = jnp.exp(s - m)
            den = jnp.exp(sk - m) + jnp.sum(p, axis=-1, keepdims=True)
            pv = jnp.dot(p.astype(BF16), v_ref[0, :, sl], preferred_element_type=F32)
        o = pv / den
        og = jnp.zeros((tq, GQA_SLAB), F32)
        for hh in range(C_GROUP):
            og = og + jnp.where((lane >> 6) == hh, o[hh * tq:(hh + 1) * tq, :], 0.0)
        o_ref[0, :, sl] = og.astype(BF16)


def _gqa(sink, q, k, v, cache=None):
    b, l, w = q.shape
    windowed = cache is not None
    tq = QBLOCK
    whole = lambda a: pl.BlockSpec((1,) + a.shape[1:], lambda i, t: (i, 0, 0))
    in_specs = [pl.BlockSpec(memory_space=pltpu.SMEM),
                pl.BlockSpec((1, tq, w), lambda i, t: (i, t, 0)), whole(k), whole(v)]
    args = [sink, q, k, v]
    if windowed:
        in_specs += [whole(cache[0]), whole(cache[1])]
        args += list(cache)
    return pl.pallas_call(
        functools.partial(_gqa_kernel, windowed=windowed, tq=tq, l=l),
        grid=(b, l // tq),
        in_specs=in_specs,
        out_specs=pl.BlockSpec((1, tq, w), lambda i, t: (i, t, 0)),
        out_shape=jax.ShapeDtypeStruct((b, l, w), BF16),
        compiler_params=_params("arbitrary", "arbitrary"),
    )(*args)


def _finish_kernel(x_ref, oa_ref, ob_ref, oc_ref, wo_ref, g1_ref, sh_ref, sc_ref, n2_ref, wr_ref, br_ref,
                   x1_o, h2_o, idx_o, gate_o, rank_o, cnt_o, *, tm):
    y = (jnp.dot(oa_ref[0], wo_ref[0:256, :], preferred_element_type=F32)
         + jnp.dot(ob_ref[0], wo_ref[256:512, :], preferred_element_type=F32)
         + jnp.dot(oc_ref[0], wo_ref[512:1024, :], preferred_element_type=F32))
    x1 = x_ref[0] + g1_ref[0] * y
    x1_o[0] = x1
    h2 = (_rms(x1) * n2_ref[...]) * (1.0 + sc_ref[0]) + sh_ref[0]
    h2_o[0] = h2
    lane = lax.broadcasted_iota(I32, (tm, LANES), 1)
    lanef = lane.astype(F32)
    logits = jnp.where(lane < N_EXPERTS, _dot3(h2, wr_ref[...]) + br_ref[...], NEG)
    work = logits
    tops, sels = [], []
    for k in range(TOP_K):
        mk = jnp.max(work, axis=-1, keepdims=True)
        ik = jnp.min(jnp.where(work == mk, lanef, float(LANES)), axis=-1, keepdims=True)
        sel = lanef == ik
        work = jnp.where(sel, 2.0 * NEG, work)
        tops.append((mk, ik))
        sels.append(sel)
    es = [jnp.exp(mk - tops[0][0]) for mk, _ in tops]
    den = (es[0] + es[1]) + (es[2] + es[3])
    multi = jnp.zeros((tm, LANES), F32)
    for sel in sels:
        multi = multi + sel.astype(F32)
    r_i = lax.broadcasted_iota(I32, (tm, tm), 0)
    c_i = lax.broadcasted_iota(I32, (tm, tm), 1)
    before = (c_i < r_i).astype(BF16)
    prior = jnp.dot(before, multi.astype(BF16), preferred_element_type=F32)
    idx = jnp.zeros((tm, LANES), F32)
    gate = jnp.zeros((tm, LANES), F32)
    rank = jnp.zeros((tm, LANES), F32)
    for k in range(TOP_K):
        slot = lane == k
        idx = jnp.where(slot, tops[k][1], idx)
        gate = jnp.where(slot, es[k] / den, gate)
        rk = jnp.sum(jnp.where(sels[k], prior, 0.0), axis=-1, keepdims=True)
        rank = jnp.where(slot, rk, rank)
    idx_o[0] = idx.astype(I32)
    gate_o[0] = gate
    rank_o[0] = rank.astype(I32)
    cnt_o[0] = jnp.sum(multi, axis=0, keepdims=True)


def _finish(x, oa, ob, oc, wo, g1, sh2, sc2, n2, wr, br):
    b, l, d = x.shape
    tm = min(ROW_TILE, l)
    nt = l // tm
    per_batch = g1.shape[0] > 1
    bidx = (lambda i, t: (i, 0, 0)) if per_batch else (lambda i, t: (0, 0, 0))
    tok = lambda width: pl.BlockSpec((1, tm, width), lambda i, t: (i, t, 0))
    const2 = lambda a: pl.BlockSpec(a.shape, lambda i, t: (0, 0))
    vec = pl.BlockSpec((1, 1, d), bidx)
    return pl.pallas_call(
        functools.partial(_finish_kernel, tm=tm),
        grid=(b, l // tm),
        in_specs=[tok(d), tok(256), tok(256), tok(512), const2(wo), vec, vec, vec,
                  const2(n2), const2(wr), const2(br)],
        out_specs=[tok(d), tok(d), tok(LANES), tok(LANES), tok(LANES),
                   pl.BlockSpec((1, 1, LANES), lambda i, t: (i * nt + t, 0, 0))],
        out_shape=[jax.ShapeDtypeStruct((b, l, d), F32), jax.ShapeDtypeStruct((b, l, d), F32),
                   jax.ShapeDtypeStruct((b, l, LANES), I32), jax.ShapeDtypeStruct((b, l, LANES), F32),
                   jax.ShapeDtypeStruct((b, l, LANES), I32), jax.ShapeDtypeStruct((b * nt, 1, LANES), F32)],
        compiler_params=_params("arbitrary", "arbitrary"),
    )(x, oa, ob, oc, wo, g1, sh2, sc2, n2, wr, br)


CHUNK = 8
SORT_ROWS = ROW_TILE * TOP_K + N_EXPERTS * CHUNK
MAX_CHUNKS = SORT_ROWS // CHUNK


def _slot_values(idx, rank, seg, k):
    lane = lax.broadcasted_iota(I32, idx.shape, 1)
    return jnp.where(lane == idx[:, k:k + 1], seg + rank[:, k:k + 1].astype(F32), 0.0)


def _dispatch_kernel(nch_ref, dst_ref, h_ref, idx_ref, rank_ref, seg_ref, xs_in_ref, xs_ref, srt, sem, *, tm, t0):
    del xs_in_ref
    tile = t0 + pl.program_id(0)
    idx, rank, seg = idx_ref[...], rank_ref[...], seg_ref[0]
    row = lax.broadcasted_iota(I32, (SORT_ROWS, tm), 0).astype(F32)
    ones = jnp.ones((CHUNK, LANES), BF16)
    perm = jnp.zeros((SORT_ROWS, tm), F32)
    for k in range(TOP_K):
        hi, lo = _split(_slot_values(idx, rank, seg, k))
        pos = (_nt_dot(ones, hi) + _nt_dot(ones, lo))[0:1, :]
        perm = perm + (row == pos).astype(F32)
    srt[...] = jnp.dot(perm.astype(BF16), h_ref[...].astype(BF16), preferred_element_type=F32)
    n = nch_ref[tile]

    def chunk_copy(j):
        src = pl.multiple_of(j * CHUNK, CHUNK)
        dst = pl.multiple_of(dst_ref[tile * MAX_CHUNKS + j], CHUNK)
        return pltpu.make_async_copy(srt.at[pl.ds(src, CHUNK)], xs_ref.at[pl.ds(dst, CHUNK)], sem)

    def issue(j, carry):
        chunk_copy(j).start()
        return carry

    def drain(j, carry):
        chunk_copy(j).wait()
        return carry

    lax.fori_loop(0, n, issue, 0)
    lax.fori_loop(0, n, drain, 0)


def _dispatch(nch, dst, h2, idx, rank, seg, xs, t0):
    n, d = h2.shape
    tm = ROW_TILE
    tok = lambda width: pl.BlockSpec((tm, width), lambda i, a, b: (i, 0))
    return pl.pallas_call(
        functools.partial(_dispatch_kernel, tm=tm, t0=t0),
        grid_spec=pltpu.PrefetchScalarGridSpec(
            num_scalar_prefetch=2,
            grid=(n // tm,),
            in_specs=[tok(d), tok(LANES), tok(LANES),
                      pl.BlockSpec((1, 1, LANES), lambda i, a, b: (t0 + i, 0, 0)),
                      pl.BlockSpec(memory_space=pl.ANY)],
            out_specs=pl.BlockSpec(memory_space=pl.ANY),
            scratch_shapes=[pltpu.VMEM((SORT_ROWS, d), F32), pltpu.SemaphoreType.DMA]),
        out_shape=jax.ShapeDtypeStruct(xs.shape, xs.dtype),
        input_output_aliases={6: 0},
        compiler_params=_params("arbitrary"),
    )(nch, dst, h2, idx, rank, seg, xs)


def _expert_kernel(be_ref, nu_ref, xs_ref, w1_ref, b1_ref, w2_ref, b2_ref, o_ref, w1b, w2b):
    i = pl.program_id(0)
    e = be_ref[i]
    prev = be_ref[jnp.maximum(i - 1, 0)]

    @pl.when((i == 0) | (e != prev))
    def _():
        w1b[...] = w1_ref[0].astype(BF16)
        w2b[...] = w2_ref[0].astype(BF16)

    @pl.when(i < nu_ref[0])
    def _():
        gu = jnp.dot(xs_ref[...].astype(BF16), w1b[...], preferred_element_type=F32) + b1_ref[0]
        gt = jnp.minimum(gu[:, :D_FF], SWIGLU_LIMIT)
        up = jnp.clip(gu[:, D_FF:], -SWIGLU_LIMIT, SWIGLU_LIMIT)
        hid = (up + 1.0) * gt * (1.0 / (1.0 + jnp.exp(-SWIGLU_ALPHA * gt)))
        o_ref[...] = jnp.dot(hid.astype(BF16), w2b[...], preferred_element_type=F32) + b2_ref[0]

    @pl.when(i >= nu_ref[0])
    def _():
        o_ref[...] = jnp.zeros_like(o_ref)


def _experts(block_expert, n_used, xs, w1, b1, w2, b2):
    n_rows, d = xs.shape
    n_blocks = n_rows // MOE_ROWS
    f2 = w1.shape[-1]
    ne = w1.shape[0] * w1.shape[1]
    w1 = w1.reshape(ne, d, f2)
    w2 = w2.reshape(ne, D_FF, d)
    row = lambda i, be, nu: (jnp.minimum(i, nu[0] - 1), 0)
    return pl.pallas_call(
        _expert_kernel,
        grid_spec=pltpu.PrefetchScalarGridSpec(
            num_scalar_prefetch=2,
            grid=(n_blocks,),
            in_specs=[pl.BlockSpec((MOE_ROWS, d), row),
                      pl.BlockSpec((1, d, f2), lambda i, be, nu: (be[i], 0, 0)),
                      pl.BlockSpec((1, 1, f2), lambda i, be, nu: (be[i], 0, 0)),
                      pl.BlockSpec((1, D_FF, d), lambda i, be, nu: (be[i], 0, 0)),
                      pl.BlockSpec((1, 1, d), lambda i, be, nu: (be[i], 0, 0))],
            out_specs=pl.BlockSpec((MOE_ROWS, d), lambda i, be, nu: (i, 0)),
            scratch_shapes=[pltpu.VMEM((d, f2), BF16), pltpu.VMEM((D_FF, d), BF16)]),
        out_shape=jax.ShapeDtypeStruct((n_rows, d), F32),
        compiler_params=_params("arbitrary"),
    )(block_expert, n_used, xs, w1, b1.reshape(ne, 1, f2), w2, b2.reshape(ne, 1, d))


def _combine_kernel(nch_ref, dst_ref, outs_ref, idx_ref, rank_ref, gate_ref, seg_ref, x1_ref, g2_ref, fg_ref,
                    o_ref, buf, sem, *, tm, nt, t0, final):
    first = (pl.program_id(0) == 0) & (pl.program_id(1) == 0)
    tile = t0 + pl.program_id(0) * nt + pl.program_id(1)

    @pl.when(first)
    def _():
        buf[...] = jnp.zeros_like(buf)

    n = nch_ref[tile]

    def chunk_copy(j):
        src = pl.multiple_of(dst_ref[tile * MAX_CHUNKS + j], CHUNK)
        dst = pl.multiple_of(j * CHUNK, CHUNK)
        return pltpu.make_async_copy(outs_ref.at[pl.ds(src, CHUNK)], buf.at[pl.ds(dst, CHUNK)], sem)

    def issue(j, carry):
        chunk_copy(j).start()
        return carry

    def drain(j, carry):
        chunk_copy(j).wait()
        return carry

    lax.fori_loop(0, n, issue, 0)
    idx, rank, gate, seg = idx_ref[0], rank_ref[0], gate_ref[0], seg_ref[0]
    col = lax.broadcasted_iota(I32, (tm, SORT_ROWS), 1).astype(F32)
    weights = jnp.zeros((tm, SORT_ROWS), F32)
    for k in range(TOP_K):
        pos = jnp.sum(_slot_values(idx, rank, seg, k), axis=-1, keepdims=True)
        weights = weights + jnp.where(col == pos, gate[:, k:k + 1], 0.0)
    wh, wl = _split(weights)
    lax.fori_loop(0, n, drain, 0)
    rows = buf[...].astype(BF16)
    y = jnp.dot(wh, rows, preferred_element_type=F32) + jnp.dot(wl, rows, preferred_element_type=F32)
    x2 = x1_ref[0] + g2_ref[0] * y
    if final:
        x2 = _rms(x2) * fg_ref[...]
    o_ref[0] = x2


def _combine(nch, dst, outs, idx, rank, gate, seg, x1, g2, fg, t0, final):
    b, l, d = x1.shape
    tm = ROW_TILE
    nt = l // tm
    per_batch = g2.shape[0] > 1
    bidx = (lambda i, t, a, c: (i, 0, 0)) if per_batch else (lambda i, t, a, c: (0, 0, 0))
    tok = lambda width: pl.BlockSpec((1, tm, width), lambda i, t, a, c: (i, t, 0))
    return pl.pallas_call(
        functools.partial(_combine_kernel, tm=tm, nt=nt, t0=t0, final=final),
        grid_spec=pltpu.PrefetchScalarGridSpec(
            num_scalar_prefetch=2,
            grid=(b, nt),
            in_specs=[pl.BlockSpec(memory_space=pl.ANY), tok(LANES), tok(LANES), tok(LANES),
                      pl.BlockSpec((1, 1, LANES), lambda i, t, a, c: (t0 + i * nt + t, 0, 0)),
                      tok(d), pl.BlockSpec((1, 1, d), bidx),
                      pl.BlockSpec((1, d), lambda i, t, a, c: (0, 0))],
            out_specs=tok(d),
            scratch_shapes=[pltpu.VMEM((SORT_ROWS, d), F32), pltpu.SemaphoreType.DMA]),
        out_shape=jax.ShapeDtypeStruct((b, l, d), F32),
        compiler_params=_params("arbitrary", "arbitrary"),
    )(nch, dst, outs, idx, rank, gate, seg, x1, g2, fg)


def _layer_weights(w_in_l):
    cuts = np.cumsum([A_WIDTH, A_WIDTH, A_WIDTH, B_WIDTH, C_WIDTH, C_KV * C_HD]).tolist()
    front = w_in_l[:, :cuts[4]]
    kc = w_in_l[:, cuts[4]:cuts[5]]
    vc = w_in_l[:, cuts[5]:]

    def rep(w):
        return jnp.concatenate([w[:, C_HD * (j // C_GROUP):C_HD * (j // C_GROUP + 1)] for j in range(C_HEADS)], axis=1)

    lat = jnp.concatenate([front, rep(kc), rep(vc)], axis=1).astype(BF16)
    ctx = jnp.concatenate([front, rep(kc), rep(vc), kc, vc], axis=1).astype(BF16)
    return ctx, lat


def _rep_heads(a):
    return jnp.repeat(a, C_GROUP, axis=2).reshape(a.shape[0], a.shape[1], C_HEADS * C_HD)


def kernel(x_prompt, x_sample, c, cache_diff_k, cache_diff_v, cache_win_k, cache_win_v, c_ctx, norm1_g, norm2_g, w_mod, b_mod, w_in, diff_lambda, diff_subln_g, w_pool, pool_scale, sink, w_out, w_router, b_router, w1, b1, w2, b2, final_g):
    depth = w_in.shape[0]
    bc, lc, d = x_prompt.shape
    bl, ll, _ = x_sample.shape
    n_ctx, n_lat = bc * lc, bl * ll
    n_tok = n_ctx + n_lat
    n_tiles = n_tok // ROW_TILE
    n_blocks = -(-(n_tok * TOP_K + n_tiles * N_EXPERTS * (CHUNK - 1)) // MOE_ROWS) + N_EXPERTS
    n_rows = n_blocks * MOE_ROWS

    mod_rows = -(-(1 + bl) // 8) * 8
    cmat = jnp.zeros((mod_rows, d), F32).at[0].set(c_ctx).at[1:1 + bl].set(c)
    mod = _mod_vectors(cmat, w_mod, b_mod)
    tabs_a = _rope_tables(ll, A_HD, A_WIDTH)
    tabs_c = _rope_tables(ll, C_HD, C_WIDTH)
    tables = (tabs_a[0], tabs_a[1], tabs_c[0], tabs_c[1])
    fg = final_g.reshape(1, d)

    xp, xs_lat = x_prompt, x_sample
    new_cache = [[], [], [], []]
    for i in range(depth):
        lam_init = 0.8 - 0.6 * math.exp(-0.3 * i)
        mv = lambda rows, j: mod[i, rows, j * d:(j + 1) * d].reshape(-1, 1, d)
        ctx_rows, lat_rows = slice(0, 1), slice(1, 1 + bl)
        w_ctx, w_lat = _layer_weights(w_in[i])
        n1 = norm1_g[i].reshape(1, d)
        n2 = norm2_g[i].reshape(1, d)
        wbd = jax.scipy.linalg.block_diag(*[w_pool[i, g] for g in range(B_GROUPS)]).astype(BF16)
        ps = pool_scale[i].reshape(1, B_WIDTH)
        g_tiled = jnp.tile(diff_subln_g[i], A_HEADS).reshape(1, A_WIDTH)
        wo = w_out[i].astype(BF16)
        wr = jnp.zeros((d, LANES), F32).at[:, :N_EXPERTS].set(w_router[i])
        br = jnp.zeros((1, LANES), F32).at[0, :N_EXPERTS].set(b_router[i])

        qa, ka, va, u, qc, kr, vr, ka32, va32, kc32, vc32 = _inproj(
            xp, mv(ctx_rows, 0), mv(ctx_rows, 1), n1, w_ctx, None)
        new_cache[0].append(ka32.reshape(bc, lc, 2 * A_HEADS, A_HD))
        new_cache[1].append(va32.reshape(bc, lc, A_HEADS, 2 * A_HD))
        new_cache[2].append(kc32.reshape(bc, lc, C_KV, C_HD))
        new_cache[3].append(vc32.reshape(bc, lc, C_KV, C_HD))
        ob = _pool(u, wbd, ps)
        oa = _diff_attn(qa, ka, va, diff_lambda[i], g_tiled, lam_init)
        oc = _gqa(sink[i], qc, kr, vr)
        x1_c, h2_c, idx_c, gate_c, rank_c, cnt_c = _finish(
            xp, oa, ob, oc, wo, mv(ctx_rows, 2), mv(ctx_rows, 3), mv(ctx_rows, 4), n2, wr, br)

        qa, ka, va, u, qc, kr, vr = _inproj(xs_lat, mv(lat_rows, 0), mv(lat_rows, 1), n1, w_lat, tables)
        ob = _pool(u, wbd, ps)
        dk = cache_diff_k[:, i].reshape(bl, -1, A_WIDTH).astype(BF16)
        dv = cache_diff_v[:, i].reshape(bl, -1, A_WIDTH).astype(BF16)
        oa = _diff_attn(qa, jnp.concatenate([ka, dk], axis=1), jnp.concatenate([va, dv], axis=1),
                        diff_lambda[i], g_tiled, lam_init)
        wk = _rep_heads(cache_win_k[:, i]).astype(BF16)
        wv = _rep_heads(cache_win_v[:, i]).astype(BF16)
        oc = _gqa(sink[i], qc, kr, vr, (wk, wv))
        x1_l, h2_l, idx_l, gate_l, rank_l, cnt_l = _finish(
            xs_lat, oa, ob, oc, wo, mv(lat_rows, 2), mv(lat_rows, 3), mv(lat_rows, 4), n2, wr, br)

        cnt = jnp.concatenate([cnt_c, cnt_l], axis=0)[:, 0, :N_EXPERTS].astype(I32)
        c8 = (cnt + CHUNK - 1) // CHUNK * CHUNK
        seg_end = jnp.cumsum(c8, axis=1)
        seg = seg_end - c8
        padded = (jnp.sum(c8, axis=0) + MOE_ROWS - 1) // MOE_ROWS * MOE_ROWS
        pad_end = jnp.cumsum(padded)
        gbase = (pad_end - padded)[None, :] + jnp.cumsum(c8, axis=0) - c8
        nch = seg_end[:, -1] // CHUNK
        j8 = jnp.arange(MAX_CHUNKS, dtype=I32) * CHUNK
        chunk_e = jnp.minimum(jnp.sum((seg_end[:, None, :] <= j8[None, :, None]).astype(I32), axis=-1),
                              N_EXPERTS - 1)
        onehot = chunk_e[..., None] == jnp.arange(N_EXPERTS, dtype=I32)
        dst = (jnp.sum(jnp.where(onehot, (gbase - seg)[:, None, :], 0), axis=-1) + j8[None, :]).reshape(-1)
        seg_f = jnp.zeros((cnt.shape[0], 1, LANES), F32).at[:, 0, :N_EXPERTS].set(seg.astype(F32))
        block_row = jnp.arange(n_blocks, dtype=I32)[:, None] * MOE_ROWS
        block_expert = jnp.minimum(jnp.sum((pad_end[None, :] <= block_row).astype(I32), axis=-1), N_EXPERTS - 1)
        n_used = (pad_end[-1:] // MOE_ROWS).astype(I32)
        tiles_c = cnt_c.shape[0]

        rows_in = jnp.zeros((n_rows, d), F32)
        rows_in = _dispatch(nch, dst, h2_c.reshape(n_ctx, d), idx_c.reshape(n_ctx, LANES),
                            rank_c.reshape(n_ctx, LANES), seg_f, rows_in, 0)
        rows_in = _dispatch(nch, dst, h2_l.reshape(n_lat, d), idx_l.reshape(n_lat, LANES),
                            rank_l.reshape(n_lat, LANES), seg_f, rows_in, tiles_c)
        rows_out = _experts(block_expert + i * N_EXPERTS, n_used, rows_in, w1, b1, w2, b2)
        final = i == depth - 1
        xp = _combine(nch, dst, rows_out, idx_c, rank_c, gate_c, seg_f, x1_c, mv(ctx_rows, 5), fg, 0, final)
        xs_lat = _combine(nch, dst, rows_out, idx_l, rank_l, gate_l, seg_f, x1_l, mv(lat_rows, 5), fg,
                          tiles_c, final)

    return (xp, xs_lat) + tuple(jnp.stack(parts, axis=1) for parts in new_cache)
```

```python
import functools
import math

import numpy as np
import jax
import jax.numpy as jnp
from jax import lax
from jax.experimental import pallas as pl
from jax.experimental.pallas import tpu as pltpu

F32 = jnp.float32
BF16 = jnp.bfloat16
I32 = jnp.int32

D_MODEL = 1024
GRID_W = 64
ROPE_BASE = 10000.0
NORM_EPS = 1e-6
A_HD = 32
A_HEADS = 4
A_WIDTH = 256
B_WIDTH = 256
B_GROUPS = 4
B_GC = 64
C_HD = 64
C_HEADS = 8
C_KV = 2
C_GROUP = 4
C_WIDTH = 512
WINDOW = 128
QBLOCK = 128
N_EXPERTS = 32
TOP_K = 4
D_FF = 1024
SWIGLU_LIMIT = 7.0
SWIGLU_ALPHA = 1.702

LANES = 128
ROW_TILE = 256
MOE_ROWS = 256
KEY_CHUNK = 512
NEG = -1e30
LOG2E = math.log2(math.e)
VMEM_LIMIT = 56 * 1024 * 1024


def _params(*sem):
    return pltpu.CompilerParams(dimension_semantics=sem, vmem_limit_bytes=VMEM_LIMIT)


def _split(x):
    hi = x.astype(BF16)
    lo = (x - hi.astype(F32)).astype(BF16)
    return hi, lo


def _dot3(a, b):
    ah, al = _split(a)
    bh, bl = _split(b)
    return (jnp.dot(ah, bh, preferred_element_type=F32)
            + (jnp.dot(ah, bl, preferred_element_type=F32)
               + jnp.dot(al, bh, preferred_element_type=F32)))


def _nt_dot(a, b):
    return lax.dot_general(a, b, (((1,), (1,)), ((), ())), preferred_element_type=F32)


def _rms(x):
    return x * lax.rsqrt(jnp.mean(x * x, axis=-1, keepdims=True) + NORM_EPS)


def _mod_kernel(c_ref, w_ref, b_ref, o_ref):
    c = c_ref[...]
    a = c * (1.0 / (1.0 + jnp.exp(-c)))
    o_ref[0] = _dot3(a, w_ref[0]) + b_ref[0]


def _mod_vectors(cmat, w_mod, b_mod):
    depth, d, e = w_mod.shape
    rows = cmat.shape[0]
    tn = 512
    return pl.pallas_call(
        _mod_kernel,
        grid=(depth, e // tn),
        in_specs=[pl.BlockSpec((rows, d), lambda l, j: (0, 0)),
                  pl.BlockSpec((1, d, tn), lambda l, j: (l, 0, j)),
                  pl.BlockSpec((1, 1, tn), lambda l, j: (l, 0, j))],
        out_specs=pl.BlockSpec((1, rows, tn), lambda l, j: (l, 0, j)),
        out_shape=jax.ShapeDtypeStruct((depth, rows, e), F32),
        compiler_params=_params("arbitrary", "arbitrary"),
    )(cmat, w_mod, b_mod.reshape(depth, 1, e))


def _rope(z, col_ref, row_ref, nf, r0):
    tm, w = z.shape
    outs = []
    for g in range(tm // GRID_W):
        zs = z[GRID_W * g:GRID_W * (g + 1), :]
        c = col_ref[0] + row_ref[0, pl.ds(r0 + g, 1), :]
        sm = col_ref[1] + row_ref[1, pl.ds(r0 + g, 1), :]
        sp = col_ref[2] + row_ref[2, pl.ds(r0 + g, 1), :]
        outs.append(zs * c + pltpu.roll(zs, w - nf, 1) * sm + pltpu.roll(zs, nf, 1) * sp)
    return jnp.concatenate(outs, axis=0)


def _inproj_kernel(*refs, rope, tm):
    x_ref, sh_ref, sc_ref, n1_ref, w_ref = refs[:5]
    if rope:
        ta_col, ta_row, tc_col, tc_row = refs[5:9]
        qa_o, ka_o, va_o, u_o, qc_o, kr_o, vr_o = refs[9:]
    else:
        qa_o, ka_o, va_o, u_o, qc_o, kr_o, vr_o, ka32_o, va32_o, kc32_o, vc32_o = refs[5:]
    x = x_ref[0]
    h = (_rms(x) * n1_ref[...]) * (1.0 + sc_ref[0]) + sh_ref[0]
    hb = h.astype(BF16)

    def seg(a, b):
        return jnp.dot(hb, w_ref[:, a:b], preferred_element_type=F32)

    qa, ka, va, u = seg(0, 256), seg(256, 512), seg(512, 768), seg(768, 1024)
    qc, kr, vr = seg(1024, 1536), seg(1536, 2048), seg(2048, 2560)
    if rope:
        r0 = pl.program_id(1) * (tm // GRID_W)
        qa = _rope(qa, ta_col, ta_row, A_HD // 4, r0)
        ka = _rope(ka, ta_col, ta_row, A_HD // 4, r0)
        qc = _rope(qc, tc_col, tc_row, C_HD // 4, r0)
        kr = _rope(kr, tc_col, tc_row, C_HD // 4, r0)
    else:
        ka32_o[0] = ka
        va32_o[0] = va
        kc32_o[0] = seg(2560, 2688)
        vc32_o[0] = seg(2688, 2816)
    qa_o[0] = (qa * (A_HD ** -0.5 * LOG2E)).astype(BF16)
    ka_o[0] = ka.astype(BF16)
    va_o[0] = va.astype(BF16)
    u_o[0] = u
    qc_o[0] = (qc * (C_HD ** -0.5)).astype(BF16)
    kr_o[0] = kr.astype(BF16)
    vr_o[0] = vr.astype(BF16)


def _inproj(x, sh, sc, n1, w, tables):
    b, l, d = x.shape
    tm = min(ROW_TILE, l)
    rope = tables is not None
    per_batch = sh.shape[0] > 1
    bidx = (lambda i, t: (i, 0, 0)) if per_batch else (lambda i, t: (0, 0, 0))
    tok = lambda width: pl.BlockSpec((1, tm, width), lambda i, t: (i, t, 0))
    in_specs = [tok(d),
                pl.BlockSpec((1, 1, d), bidx), pl.BlockSpec((1, 1, d), bidx),
                pl.BlockSpec((1, d), lambda i, t: (0, 0)),
                pl.BlockSpec(w.shape, lambda i, t: (0, 0))]
    args = [x, sh, sc, n1, w]
    widths = [(256, BF16), (256, BF16), (256, BF16), (256, F32), (512, BF16), (512, BF16), (512, BF16)]
    if rope:
        for tab in tables:
            in_specs.append(pl.BlockSpec(tab.shape, lambda i, t: (0, 0, 0)))
            args.append(tab)
    else:
        widths += [(256, F32), (256, F32), (128, F32), (128, F32)]
    return pl.pallas_call(
        functools.partial(_inproj_kernel, rope=rope, tm=tm),
        grid=(b, l // tm),
        in_specs=in_specs,
        out_specs=[tok(wd) for wd, _ in widths],
        out_shape=[jax.ShapeDtypeStruct((b, l, wd), dt) for wd, dt in widths],
        compiler_params=_params("arbitrary", "arbitrary"),
    )(*args)


def _rope_tables(n_lat, head_dim, width):
    rows = n_lat // GRID_W
    nf = head_dim // 4
    inv = ROPE_BASE ** (-jnp.arange(nf, dtype=F32) / nf)
    lane = np.arange(width) % head_dim
    half = lane // (2 * nf)
    pair = (lane // nf) % 2
    f = lane % nf

    def part(pos, which):
        ang = pos[:, None] * inv[f][None, :]
        on = jnp.asarray(half == which, F32)[None, :]
        c = jnp.cos(ang) * on
        s = jnp.sin(ang) * on
        sm = -s * jnp.asarray(pair == 0, F32)[None, :]
        sp = s * jnp.asarray(pair == 1, F32)[None, :]
        return jnp.stack([c, sm, sp])

    return part(jnp.arange(GRID_W, dtype=F32), 1), part(jnp.arange(rows, dtype=F32), 0)


POOL_PAD = 8
POOL_CHUNK = 256


def _pool_kernel(u_ref, w_ref, ps_ref, o_ref, pad_ref, *, l):
    zeros = jnp.zeros((POOL_PAD, B_WIDTH), F32)
    pad_ref[0:POOL_PAD, :] = zeros
    pad_ref[POOL_PAD + l:2 * POOL_PAD + l, :] = zeros
    pad_ref[POOL_PAD:POOL_PAD + l, :] = u_ref[0]
    ch = min(POOL_CHUNK, l)
    lane = lax.broadcasted_iota(I32, (ch, B_WIDTH), 1)
    grp = lane >> 6
    half = jnp.where(grp == 0, 1, jnp.where(grp == 1, 2, jnp.where(grp == 2, 4, 8)))
    row = lax.broadcasted_iota(I32, (ch, B_WIDTH), 0)
    for c in range(0, l, ch):
        ld = lambda k: pad_ref[c + POOL_PAD + k:c + POOL_PAD + k + ch, :]
        cur = ld(0)
        s2 = ld(-1) + cur
        s4 = s2 + (ld(-2) + ld(1))
        s8 = s4 + ((ld(-4) + ld(-3)) + (ld(2) + ld(3)))
        s16 = s8 + (((ld(-8) + ld(-7)) + (ld(-6) + ld(-5))) + ((ld(4) + ld(5)) + (ld(6) + ld(7))))
        win = jnp.where(grp == 0, s2, jnp.where(grp == 1, s4, jnp.where(grp == 2, s8, s16)))
        t = row + c
        cnt = (jnp.minimum(t + half, l) - jnp.maximum(t - half, 0)).astype(F32)
        r = (win / cnt - cur).astype(BF16)
        y = jnp.dot(r, w_ref[...], preferred_element_type=F32) * ps_ref[...]
        o_ref[0, c:c + ch, :] = y.astype(BF16)


def _pool(u, wbd, ps):
    b, l, w = u.shape
    return pl.pallas_call(
        functools.partial(_pool_kernel, l=l),
        grid=(b,),
        in_specs=[pl.BlockSpec((1, l, w), lambda i: (i, 0, 0)),
                  pl.BlockSpec((w, w), lambda i: (0, 0)),
                  pl.BlockSpec((1, w), lambda i: (0, 0))],
        out_specs=pl.BlockSpec((1, l, w), lambda i: (i, 0, 0)),
        out_shape=jax.ShapeDtypeStruct((b, l, w), BF16),
        scratch_shapes=[pltpu.VMEM((l + 2 * POOL_PAD, w), F32)],
        compiler_params=_params("arbitrary"),
    )(u, wbd, ps)


def _diff_attn_kernel(q_ref, k_ref, v_ref, lamp_ref, g_ref, o_ref, s_scr, p_scr, *, tq, s_len, lam_init):
    tk = min(s_len, KEY_CHUNK)
    q32 = q_ref[0].astype(F32)
    lp = lamp_ref[...]
    lam = (jnp.exp(jnp.sum(lp[0:1] * lp[1:2], axis=-1, keepdims=True))
           - jnp.exp(jnp.sum(lp[2:3] * lp[3:4], axis=-1, keepdims=True)) + lam_init)
    lane_row = lax.broadcasted_iota(I32, (1, A_WIDTH), 1)
    lane = lax.broadcasted_iota(I32, (tq, A_WIDTH), 1)

    def scores(h):
        row_max = []
        for mp in range(2):
            qm = (q32 * ((lane_row >> 5) == 2 * h + mp).astype(F32)).astype(BF16)
            part = jnp.full((tq, LANES), NEG, F32)
            for c in range(0, s_len, tk):
                s = _nt_dot(qm, k_ref[0, c:c + tk, :])
                s_scr[2 * (h % 2) + mp, :, c:c + tk] = s
                for j in range(0, tk, LANES):
                    part = jnp.maximum(part, s[:, j:j + LANES])
            row_max.append(jnp.max(part, axis=-1, keepdims=True))
        return row_max

    def weights(h, row_max):
        dens = []
        for mp in range(2):
            slot = 2 * (h % 2) + mp
            part = jnp.zeros((tq, LANES), F32)
            for c in range(0, s_len, tk):
                p = jnp.exp2(s_scr[slot, :, c:c + tk] - row_max[mp])
                for j in range(0, tk, LANES):
                    part = part + p[:, j:j + LANES]
                p_scr[slot, :, c:c + tk] = p.astype(BF16)
            dens.append(jnp.sum(part, axis=-1, keepdims=True))
        return dens

    def values(h, dens):
        maps = [jnp.dot(p_scr[2 * (h % 2) + mp], v_ref[0], preferred_element_type=F32) / dens[mp]
                for mp in range(2)]
        return jnp.where((lane >> 6) == h, maps[0] - lam * maps[1], 0.0)

    out = jnp.zeros((tq, A_WIDTH), F32)
    row_max = scores(0)
    for h in range(A_HEADS):
        dens = weights(h, row_max)
        if h + 1 < A_HEADS:
            row_max = scores(h + 1)
        out = out + values(h, dens)
    sq = out * out
    rs = jnp.zeros((tq, A_WIDTH), F32)
    for h in range(A_HEADS):
        msk = (lane >> 6) == h
        ms = jnp.sum(jnp.where(msk, sq, 0.0), axis=-1, keepdims=True) * (1.0 / (2 * A_HD))
        rs = rs + jnp.where(msk, lax.rsqrt(ms + NORM_EPS), 0.0)
    o_ref[0] = (((out * rs) * g_ref[...]) * (1.0 - lam_init)).astype(BF16)


def _diff_attn(q, k, v, lam_p, g_tiled, lam_init):
    b, l, w = q.shape
    s_len = k.shape[1]
    tq = min(ROW_TILE, l)
    whole = pl.BlockSpec((1, s_len, w), lambda i, t: (i, 0, 0))
    return pl.pallas_call(
        functools.partial(_diff_attn_kernel, tq=tq, s_len=s_len, lam_init=lam_init),
        grid=(b, l // tq),
        in_specs=[pl.BlockSpec((1, tq, w), lambda i, t: (i, t, 0)), whole, whole,
                  pl.BlockSpec(lam_p.shape, lambda i, t: (0, 0)),
                  pl.BlockSpec(g_tiled.shape, lambda i, t: (0, 0))],
        out_specs=pl.BlockSpec((1, tq, w), lambda i, t: (i, t, 0)),
        out_shape=jax.ShapeDtypeStruct((b, l, w), BF16),
        scratch_shapes=[pltpu.VMEM((4, tq, s_len), F32), pltpu.VMEM((4, tq, s_len), BF16)],
        compiler_params=_params("arbitrary", "arbitrary"),
    )(q, k, v, lam_p, g_tiled)


GQA_SLAB = C_GROUP * C_HD


def _gqa_kernel(*refs, windowed, tq, l):
    if windowed:
        sink_ref, q_ref, k_ref, v_ref, kc_ref, vc_ref, o_ref = refs
    else:
        sink_ref, q_ref, k_ref, v_ref, o_ref = refs
    i = pl.program_id(1)
    rows = C_GROUP * tq
    shift = int(math.log2(tq))
    q32 = q_ref[0].astype(F32)
    lane_row = lax.broadcasted_iota(I32, (1, GQA_SLAB), 1)
    lane = lax.broadcasted_iota(I32, (tq, GQA_SLAB), 1)
    rid = lax.broadcasted_iota(I32, (rows, 1), 0)
    for g in range(C_KV):
        sl = slice(GQA_SLAB * g, GQA_SLAB * (g + 1))
        qg = q32[:, sl]
        qs = jnp.concatenate(
            [(qg * ((lane_row >> 6) == hh).astype(F32)).astype(BF16) for hh in range(C_GROUP)], axis=0)
        sk = jnp.zeros((rows, 1), F32)
        for hh in range(C_GROUP):
            sk = jnp.where((rid >> shift) == hh, sink_ref[C_GROUP * g + hh], sk)
        if windowed:
            ws = pl.multiple_of(jnp.clip((i - 1) * tq, 0, l - 3 * tq), tq)
            s_loc = _nt_dot(qs, k_ref[0, pl.ds(ws, 3 * tq), sl])
            qpos = i * tq + (rid & (tq - 1))
            kpos = ws + lax.broadcasted_iota(I32, (1, 3 * tq), 1)
            s_loc = jnp.where(jnp.abs(kpos - qpos) <= WINDOW, s_loc, NEG)
            s_ctx = _nt_dot(qs, kc_ref[0, :, sl])
            m = jnp.maximum(sk, jnp.maximum(jnp.max(s_loc, axis=-1, keepdims=True),
                                            jnp.max(s_ctx, axis=-1, keepdims=True)))
            p_loc = jnp.exp(s_loc - m)
            p_ctx = jnp.exp(s_ctx - m)
            den = (jnp.exp(sk - m) + jnp.sum(p_loc, axis=-1, keepdims=True)
                   + jnp.sum(p_ctx, axis=-1, keepdims=True))
            pv = (jnp.dot(p_loc.astype(BF16), v_ref[0, pl.ds(ws, 3 * tq), sl], preferred_element_type=F32)
                  + jnp.dot(p_ctx.astype(BF16), vc_ref[0, :, sl], preferred_element_type=F32))
        else:
            s = _nt_dot(qs, k_ref[0, :, sl])
            m = jnp.maximum(sk, jnp.max(s, axis=-1, keepdims=True))
            p = jnp.exp(s - m)
            den = jnp.exp(sk - m) + jnp.sum(p, axis=-1, keepdims=True)
            pv = jnp.dot(p.astype(BF16), v_ref[0, :, sl], preferred_element_type=F32)
        o = pv / den
        og = jnp.zeros((tq, GQA_SLAB), F32)
        for hh in range(C_GROUP):
            og = og + jnp.where((lane >> 6) == hh, o[hh * tq:(hh + 1) * tq, :], 0.0)
        o_ref[0, :, sl] = og.astype(BF16)


def _gqa(sink, q, k, v, cache=None):
    b, l, w = q.shape
    windowed = cache is not None
    tq = QBLOCK
    whole = lambda a: pl.BlockSpec((1,) + a.shape[1:], lambda i, t: (i, 0, 0))
    in_specs = [pl.BlockSpec(memory_space=pltpu.SMEM),
                pl.BlockSpec((1, tq, w), lambda i, t: (i, t, 0)), whole(k), whole(v)]
    args = [sink, q, k, v]
    if windowed:
        in_specs += [whole(cache[0]), whole(cache[1])]
        args += list(cache)
    return pl.pallas_call(
        functools.partial(_gqa_kernel, windowed=windowed, tq=tq, l=l),
        grid=(b, l // tq),
        in_specs=in_specs,
        out_specs=pl.BlockSpec((1, tq, w), lambda i, t: (i, t, 0)),
        out_shape=jax.ShapeDtypeStruct((b, l, w), BF16),
        compiler_params=_params("arbitrary", "arbitrary"),
    )(*args)


def _finish_kernel(x_ref, oa_ref, ob_ref, oc_ref, wo_ref, g1_ref, sh_ref, sc_ref, n2_ref, wr_ref, br_ref,
                   x1_o, h2_o, idx_o, gate_o, rank_o, cnt_o, *, tm):
    y = (jnp.dot(oa_ref[0], wo_ref[0:256, :], preferred_element_type=F32)
         + jnp.dot(ob_ref[0], wo_ref[256:512, :], preferred_element_type=F32)
         + jnp.dot(oc_ref[0], wo_ref[512:1024, :], preferred_element_type=F32))
    x1 = x_ref[0] + g1_ref[0] * y
    x1_o[0] = x1
    h2 = (_rms(x1) * n2_ref[...]) * (1.0 + sc_ref[0]) + sh_ref[0]
    h2_o[0] = h2
    lane = lax.broadcasted_iota(I32, (tm, LANES), 1)
    lanef = lane.astype(F32)
    logits = jnp.where(lane < N_EXPERTS, _dot3(h2, wr_ref[...]) + br_ref[...], NEG)
    work = logits
    tops, sels = [], []
    for k in range(TOP_K):
        mk = jnp.max(work, axis=-1, keepdims=True)
        ik = jnp.min(jnp.where(work == mk, lanef, float(LANES)), axis=-1, keepdims=True)
        sel = lanef == ik
        work = jnp.where(sel, 2.0 * NEG, work)
        tops.append((mk, ik))
        sels.append(sel)
    es = [jnp.exp(mk - tops[0][0]) for mk, _ in tops]
    den = (es[0] + es[1]) + (es[2] + es[3])
    multi = jnp.zeros((tm, LANES), F32)
    for sel in sels:
        multi = multi + sel.astype(F32)
    r_i = lax.broadcasted_iota(I32, (tm, tm), 0)
    c_i = lax.broadcasted_iota(I32, (tm, tm), 1)
    before = (c_i < r_i).astype(BF16)
    prior = jnp.dot(before, multi.astype(BF16), preferred_element_type=F32)
    idx = jnp.zeros((tm, LANES), F32)
    gate = jnp.zeros((tm, LANES), F32)
    rank = jnp.zeros((tm, LANES), F32)
    for k in range(TOP_K):
        slot = lane == k
        idx = jnp.where(slot, tops[k][1], idx)
        gate = jnp.where(slot, es[k] / den, gate)
        rk = jnp.sum(jnp.where(sels[k], prior, 0.0), axis=-1, keepdims=True)
        rank = jnp.where(slot, rk, rank)
    idx_o[0] = idx.astype(I32)
    gate_o[0] = gate
    rank_o[0] = rank.astype(I32)
    cnt_o[0] = jnp.sum(multi, axis=0, keepdims=True)


def _finish(x, oa, ob, oc, wo, g1, sh2, sc2, n2, wr, br):
    b, l, d = x.shape
    tm = min(ROW_TILE, l)
    nt = l // tm
    per_batch = g1.shape[0] > 1
    bidx = (lambda i, t: (i, 0, 0)) if per_batch else (lambda i, t: (0, 0, 0))
    tok = lambda width: pl.BlockSpec((1, tm, width), lambda i, t: (i, t, 0))
    const2 = lambda a: pl.BlockSpec(a.shape, lambda i, t: (0, 0))
    vec = pl.BlockSpec((1, 1, d), bidx)
    return pl.pallas_call(
        functools.partial(_finish_kernel, tm=tm),
        grid=(b, l // tm),
        in_specs=[tok(d), tok(256), tok(256), tok(512), const2(wo), vec, vec, vec,
                  const2(n2), const2(wr), const2(br)],
        out_specs=[tok(d), tok(d), tok(LANES), tok(LANES), tok(LANES),
                   pl.BlockSpec((1, 1, LANES), lambda i, t: (i * nt + t, 0, 0))],
        out_shape=[jax.ShapeDtypeStruct((b, l, d), F32), jax.ShapeDtypeStruct((b, l, d), F32),
                   jax.ShapeDtypeStruct((b, l, LANES), I32), jax.ShapeDtypeStruct((b, l, LANES), F32),
                   jax.ShapeDtypeStruct((b, l, LANES), I32), jax.ShapeDtypeStruct((b * nt, 1, LANES), F32)],
        compiler_params=_params("arbitrary", "arbitrary"),
    )(x, oa, ob, oc, wo, g1, sh2, sc2, n2, wr, br)


CHUNK = 8
SORT_ROWS = ROW_TILE * TOP_K + N_EXPERTS * CHUNK
MAX_CHUNKS = SORT_ROWS // CHUNK


def _slot_values(idx, rank, seg, k):
    lane = lax.broadcasted_iota(I32, idx.shape, 1)
    return jnp.where(lane == idx[:, k:k + 1], seg + rank[:, k:k + 1].astype(F32), 0.0)


def _dispatch_kernel(nch_ref, dst_ref, h_ref, idx_ref, rank_ref, seg_ref, xs_in_ref, xs_ref, srt, sems, *, tm, t0):
    del xs_in_ref
    tile = t0 + pl.program_id(0)
    idx, rank, seg = idx_ref[...], rank_ref[...], seg_ref[0]
    row = lax.broadcasted_iota(I32, (SORT_ROWS, tm), 0).astype(F32)
    ones = jnp.ones((CHUNK, LANES), BF16)
    perm = jnp.zeros((SORT_ROWS, tm), F32)
    for k in range(TOP_K):
        hi, lo = _split(_slot_values(idx, rank, seg, k))
        pos = (_nt_dot(ones, hi) + _nt_dot(ones, lo))[0:1, :]
        perm = perm + (row == pos).astype(F32)
    step = pl.program_id(0)
    slot = step % 2
    srt[slot] = jnp.dot(perm.astype(BF16), h_ref[...].astype(BF16), preferred_element_type=F32)

    def chunk_copy(tl, sl, j):
        src = pl.multiple_of(j * CHUNK, CHUNK)
        dst = pl.multiple_of(dst_ref[tl * MAX_CHUNKS + j], CHUNK)
        return pltpu.make_async_copy(srt.at[sl, pl.ds(src, CHUNK)], xs_ref.at[pl.ds(dst, CHUNK)], sems.at[sl])

    def issue(j, carry):
        chunk_copy(tile, slot, j).start()
        return carry

    lax.fori_loop(0, nch_ref[tile], issue, 0)

    @pl.when(step > 0)
    def _():
        def drain(j, carry):
            chunk_copy(tile - 1, 1 - slot, j).wait()
            return carry
        lax.fori_loop(0, nch_ref[tile - 1], drain, 0)

    @pl.when(step == pl.num_programs(0) - 1)
    def _():
        def drain(j, carry):
            chunk_copy(tile, slot, j).wait()
            return carry
        lax.fori_loop(0, nch_ref[tile], drain, 0)


def _dispatch(nch, dst, h2, idx, rank, seg, xs, t0):
    n, d = h2.shape
    tm = ROW_TILE
    tok = lambda width: pl.BlockSpec((tm, width), lambda i, a, b: (i, 0))
    return pl.pallas_call(
        functools.partial(_dispatch_kernel, tm=tm, t0=t0),
        grid_spec=pltpu.PrefetchScalarGridSpec(
            num_scalar_prefetch=2,
            grid=(n // tm,),
            in_specs=[tok(d), tok(LANES), tok(LANES),
                      pl.BlockSpec((1, 1, LANES), lambda i, a, b: (t0 + i, 0, 0)),
                      pl.BlockSpec(memory_space=pl.ANY)],
            out_specs=pl.BlockSpec(memory_space=pl.ANY),
            scratch_shapes=[pltpu.VMEM((2, SORT_ROWS, d), F32), pltpu.SemaphoreType.DMA((2,))]),
        out_shape=jax.ShapeDtypeStruct(xs.shape, xs.dtype),
        input_output_aliases={6: 0},
        compiler_params=_params("arbitrary"),
    )(nch, dst, h2, idx, rank, seg, xs)


def _expert_kernel(be_ref, nx_ref, nu_ref, xs_ref, w1_ref, b1_ref, w2_ref, b2_ref, o_ref,
                   w1s, w2s, w1b, w2b, sems):
    i = pl.program_id(0)
    e = be_ref[i]
    prev = be_ref[jnp.maximum(i - 1, 0)]

    def fetch(expert):
        return (pltpu.make_async_copy(w1_ref.at[expert], w1s, sems.at[0]),
                pltpu.make_async_copy(w2_ref.at[expert], w2s, sems.at[1]))

    @pl.when(i == 0)
    def _():
        for cp in fetch(e):
            cp.start()

    @pl.when((i < nu_ref[0]) & ((i == 0) | (e != prev)))
    def _():
        for cp in fetch(e):
            cp.wait()
        w1b[...] = w1s[...].astype(BF16)
        w2b[...] = w2s[...].astype(BF16)

        @pl.when(nx_ref[i] != e)
        def _():
            for cp in fetch(nx_ref[i]):
                cp.start()

    @pl.when(i < nu_ref[0])
    def _():
        gu = jnp.dot(xs_ref[...].astype(BF16), w1b[...], preferred_element_type=F32) + b1_ref[0]
        gt = jnp.minimum(gu[:, :D_FF], SWIGLU_LIMIT)
        up = jnp.clip(gu[:, D_FF:], -SWIGLU_LIMIT, SWIGLU_LIMIT)
        hid = (up + 1.0) * gt * (1.0 / (1.0 + jnp.exp(-SWIGLU_ALPHA * gt)))
        o_ref[...] = jnp.dot(hid.astype(BF16), w2b[...], preferred_element_type=F32) + b2_ref[0]

    @pl.when(i >= nu_ref[0])
    def _():
        o_ref[...] = jnp.zeros_like(o_ref)


def _experts(block_expert, next_expert, n_used, xs, w1, b1, w2, b2):
    n_rows, d = xs.shape
    n_blocks = n_rows // MOE_ROWS
    f2 = w1.shape[-1]
    ne = w1.shape[0] * w1.shape[1]
    w1 = w1.reshape(ne, d, f2)
    w2 = w2.reshape(ne, D_FF, d)
    row = lambda i, be, nx, nu: (jnp.minimum(i, nu[0] - 1), 0)
    return pl.pallas_call(
        _expert_kernel,
        grid_spec=pltpu.PrefetchScalarGridSpec(
            num_scalar_prefetch=3,
            grid=(n_blocks,),
            in_specs=[pl.BlockSpec((MOE_ROWS, d), row),
                      pl.BlockSpec(memory_space=pl.ANY),
                      pl.BlockSpec((1, 1, f2), lambda i, be, nx, nu: (be[i], 0, 0)),
                      pl.BlockSpec(memory_space=pl.ANY),
                      pl.BlockSpec((1, 1, d), lambda i, be, nx, nu: (be[i], 0, 0))],
            out_specs=pl.BlockSpec((MOE_ROWS, d), lambda i, be, nx, nu: (i, 0)),
            scratch_shapes=[pltpu.VMEM((d, f2), F32), pltpu.VMEM((D_FF, d), F32),
                            pltpu.VMEM((d, f2), BF16), pltpu.VMEM((D_FF, d), BF16),
                            pltpu.SemaphoreType.DMA((2,))]),
        out_shape=jax.ShapeDtypeStruct((n_rows, d), F32),
        compiler_params=_params("arbitrary"),
    )(block_expert, next_expert, n_used, xs, w1, b1.reshape(ne, 1, f2), w2, b2.reshape(ne, 1, d))


def _combine_kernel(nch_ref, dst_ref, outs_ref, idx_ref, rank_ref, gate_ref, seg_ref, x1_ref, g2_ref, fg_ref,
                    o_ref, buf, sems, *, tm, nt, t0, final):
    step = pl.program_id(0) * nt + pl.program_id(1)
    tile = t0 + step
    slot = step % 2

    def chunk_copy(tl, sl, j):
        src = pl.multiple_of(dst_ref[tl * MAX_CHUNKS + j], CHUNK)
        dst = pl.multiple_of(j * CHUNK, CHUNK)
        return pltpu.make_async_copy(outs_ref.at[pl.ds(src, CHUNK)], buf.at[sl, pl.ds(dst, CHUNK)], sems.at[sl])

    def fetch(tl, sl):
        def issue(j, carry):
            chunk_copy(tl, sl, j).start()
            return carry
        lax.fori_loop(0, nch_ref[tl], issue, 0)

    @pl.when(step == 0)
    def _():
        buf[...] = jnp.zeros_like(buf)
        fetch(tile, slot)

    @pl.when(step + 1 < pl.num_programs(0) * nt)
    def _():
        fetch(tile + 1, 1 - slot)

    def drain(j, carry):
        chunk_copy(tile, slot, j).wait()
        return carry

    n = nch_ref[tile]
    idx, rank, gate, seg = idx_ref[0], rank_ref[0], gate_ref[0], seg_ref[0]
    col = lax.broadcasted_iota(I32, (tm, SORT_ROWS), 1).astype(F32)
    weights = jnp.zeros((tm, SORT_ROWS), F32)
    for k in range(TOP_K):
        pos = jnp.sum(_slot_values(idx, rank, seg, k), axis=-1, keepdims=True)
        weights = weights + jnp.where(col == pos, gate[:, k:k + 1], 0.0)
    wh, wl = _split(weights)
    lax.fori_loop(0, n, drain, 0)
    rows = buf[slot].astype(BF16)
    y = jnp.dot(wh, rows, preferred_element_type=F32) + jnp.dot(wl, rows, preferred_element_type=F32)
    x2 = x1_ref[0] + g2_ref[0] * y
    if final:
        x2 = _rms(x2) * fg_ref[...]
    o_ref[0] = x2


def _combine(nch, dst, outs, idx, rank, gate, seg, x1, g2, fg, t0, final):
    b, l, d = x1.shape
    tm = ROW_TILE
    nt = l // tm
    per_batch = g2.shape[0] > 1
    bidx = (lambda i, t, a, c: (i, 0, 0)) if per_batch else (lambda i, t, a, c: (0, 0, 0))
    tok = lambda width: pl.BlockSpec((1, tm, width), lambda i, t, a, c: (i, t, 0))
    return pl.pallas_call(
        functools.partial(_combine_kernel, tm=tm, nt=nt, t0=t0, final=final),
        grid_spec=pltpu.PrefetchScalarGridSpec(
            num_scalar_prefetch=2,
            grid=(b, nt),
            in_specs=[pl.BlockSpec(memory_space=pl.ANY), tok(LANES), tok(LANES), tok(LANES),
                      pl.BlockSpec((1, 1, LANES), lambda i, t, a, c: (t0 + i * nt + t, 0, 0)),
                      tok(d), pl.BlockSpec((1, 1, d), bidx),
                      pl.BlockSpec((1, d), lambda i, t, a, c: (0, 0))],
            out_specs=tok(d),
            scratch_shapes=[pltpu.VMEM((2, SORT_ROWS, d), F32), pltpu.SemaphoreType.DMA((2,))]),
        out_shape=jax.ShapeDtypeStruct((b, l, d), F32),
        compiler_params=_params("arbitrary", "arbitrary"),
    )(nch, dst, outs, idx, rank, gate, seg, x1, g2, fg)


def _layer_weights(w_in_l):
    cuts = np.cumsum([A_WIDTH, A_WIDTH, A_WIDTH, B_WIDTH, C_WIDTH, C_KV * C_HD]).tolist()
    front = w_in_l[:, :cuts[4]]
    kc = w_in_l[:, cuts[4]:cuts[5]]
    vc = w_in_l[:, cuts[5]:]

    def rep(w):
        return jnp.concatenate([w[:, C_HD * (j // C_GROUP):C_HD * (j // C_GROUP + 1)] for j in range(C_HEADS)], axis=1)

    lat = jnp.concatenate([front, rep(kc), rep(vc)], axis=1).astype(BF16)
    ctx = jnp.concatenate([front, rep(kc), rep(vc), kc, vc], axis=1).astype(BF16)
    return ctx, lat


def _rep_heads(a):
    return jnp.repeat(a, C_GROUP, axis=2).reshape(a.shape[0], a.shape[1], C_HEADS * C_HD)


def kernel(x_prompt, x_sample, c, cache_diff_k, cache_diff_v, cache_win_k, cache_win_v, c_ctx, norm1_g, norm2_g, w_mod, b_mod, w_in, diff_lambda, diff_subln_g, w_pool, pool_scale, sink, w_out, w_router, b_router, w1, b1, w2, b2, final_g):
    depth = w_in.shape[0]
    bc, lc, d = x_prompt.shape
    bl, ll, _ = x_sample.shape
    n_ctx, n_lat = bc * lc, bl * ll
    n_tok = n_ctx + n_lat
    n_tiles = n_tok // ROW_TILE
    n_blocks = -(-(n_tok * TOP_K + n_tiles * N_EXPERTS * (CHUNK - 1)) // MOE_ROWS) + N_EXPERTS
    n_rows = n_blocks * MOE_ROWS

    mod_rows = -(-(1 + bl) // 8) * 8
    cmat = jnp.zeros((mod_rows, d), F32).at[0].set(c_ctx).at[1:1 + bl].set(c)
    mod = _mod_vectors(cmat, w_mod, b_mod)
    tabs_a = _rope_tables(ll, A_HD, A_WIDTH)
    tabs_c = _rope_tables(ll, C_HD, C_WIDTH)
    tables = (tabs_a[0], tabs_a[1], tabs_c[0], tabs_c[1])
    fg = final_g.reshape(1, d)

    xp, xs_lat = x_prompt, x_sample
    new_cache = [[], [], [], []]
    for i in range(depth):
        lam_init = 0.8 - 0.6 * math.exp(-0.3 * i)
        mv = lambda rows, j: mod[i, rows, j * d:(j + 1) * d].reshape(-1, 1, d)
        ctx_rows, lat_rows = slice(0, 1), slice(1, 1 + bl)
        w_ctx, w_lat = _layer_weights(w_in[i])
        n1 = norm1_g[i].reshape(1, d)
        n2 = norm2_g[i].reshape(1, d)
        wbd = jax.scipy.linalg.block_diag(*[w_pool[i, g] for g in range(B_GROUPS)]).astype(BF16)
        ps = pool_scale[i].reshape(1, B_WIDTH)
        g_tiled = jnp.tile(diff_subln_g[i], A_HEADS).reshape(1, A_WIDTH)
        wo = w_out[i].astype(BF16)
        wr = jnp.zeros((d, LANES), F32).at[:, :N_EXPERTS].set(w_router[i])
        br = jnp.zeros((1, LANES), F32).at[0, :N_EXPERTS].set(b_router[i])

        qa, ka, va, u, qc, kr, vr, ka32, va32, kc32, vc32 = _inproj(
            xp, mv(ctx_rows, 0), mv(ctx_rows, 1), n1, w_ctx, None)
        new_cache[0].append(ka32.reshape(bc, lc, 2 * A_HEADS, A_HD))
        new_cache[1].append(va32.reshape(bc, lc, A_HEADS, 2 * A_HD))
        new_cache[2].append(kc32.reshape(bc, lc, C_KV, C_HD))
        new_cache[3].append(vc32.reshape(bc, lc, C_KV, C_HD))
        ob = _pool(u, wbd, ps)
        oa = _diff_attn(qa, ka, va, diff_lambda[i], g_tiled, lam_init)
        oc = _gqa(sink[i], qc, kr, vr)
        x1_c, h2_c, idx_c, gate_c, rank_c, cnt_c = _finish(
            xp, oa, ob, oc, wo, mv(ctx_rows, 2), mv(ctx_rows, 3), mv(ctx_rows, 4), n2, wr, br)

        qa, ka, va, u, qc, kr, vr = _inproj(xs_lat, mv(lat_rows, 0), mv(lat_rows, 1), n1, w_lat, tables)
        ob = _pool(u, wbd, ps)
        dk = cache_diff_k[:, i].reshape(bl, -1, A_WIDTH).astype(BF16)
        dv = cache_diff_v[:, i].reshape(bl, -1, A_WIDTH).astype(BF16)
        oa = _diff_attn(qa, jnp.concatenate([ka, dk], axis=1), jnp.concatenate([va, dv], axis=1),
                        diff_lambda[i], g_tiled, lam_init)
        wk = _rep_heads(cache_win_k[:, i]).astype(BF16)
        wv = _rep_heads(cache_win_v[:, i]).astype(BF16)
        oc = _gqa(sink[i], qc, kr, vr, (wk, wv))
        x1_l, h2_l, idx_l, gate_l, rank_l, cnt_l = _finish(
            xs_lat, oa, ob, oc, wo, mv(lat_rows, 2), mv(lat_rows, 3), mv(lat_rows, 4), n2, wr, br)

        cnt = jnp.concatenate([cnt_c, cnt_l], axis=0)[:, 0, :N_EXPERTS].astype(I32)
        c8 = (cnt + CHUNK - 1) // CHUNK * CHUNK
        seg_end = jnp.cumsum(c8, axis=1)
        seg = seg_end - c8
        padded = (jnp.sum(c8, axis=0) + MOE_ROWS - 1) // MOE_ROWS * MOE_ROWS
        pad_end = jnp.cumsum(padded)
        gbase = (pad_end - padded)[None, :] + jnp.cumsum(c8, axis=0) - c8
        nch = seg_end[:, -1] // CHUNK
        j8 = jnp.arange(MAX_CHUNKS, dtype=I32) * CHUNK
        chunk_e = jnp.minimum(jnp.sum((seg_end[:, None, :] <= j8[None, :, None]).astype(I32), axis=-1),
                              N_EXPERTS - 1)
        onehot = chunk_e[..., None] == jnp.arange(N_EXPERTS, dtype=I32)
        dst = (jnp.sum(jnp.where(onehot, (gbase - seg)[:, None, :], 0), axis=-1) + j8[None, :]).reshape(-1)
        seg_f = jnp.zeros((cnt.shape[0], 1, LANES), F32).at[:, 0, :N_EXPERTS].set(seg.astype(F32))
        block_row = jnp.arange(n_blocks, dtype=I32)[:, None] * MOE_ROWS
        block_expert = jnp.minimum(jnp.sum((pad_end[None, :] <= block_row).astype(I32), axis=-1), N_EXPERTS - 1)
        n_used = (pad_end[-1:] // MOE_ROWS).astype(I32)
        experts = jnp.arange(N_EXPERTS, dtype=I32)
        later = (experts[None, :] > experts[:, None]) & (padded[None, :] > 0)
        next_of = jnp.min(jnp.where(later, experts[None, :], N_EXPERTS), axis=1)
        next_of = jnp.where(next_of == N_EXPERTS, experts, next_of)
        next_expert = jnp.sum(jnp.where(block_expert[:, None] == experts, next_of, 0), axis=-1)
        tiles_c = cnt_c.shape[0]

        rows_in = jnp.zeros((n_rows, d), F32)
        rows_in = _dispatch(nch, dst, h2_c.reshape(n_ctx, d), idx_c.reshape(n_ctx, LANES),
                            rank_c.reshape(n_ctx, LANES), seg_f, rows_in, 0)
        rows_in = _dispatch(nch, dst, h2_l.reshape(n_lat, d), idx_l.reshape(n_lat, LANES),
                            rank_l.reshape(n_lat, LANES), seg_f, rows_in, tiles_c)
        rows_out = _experts(block_expert + i * N_EXPERTS, next_expert + i * N_EXPERTS, n_used, rows_in,
                            w1, b1, w2, b2)
        final = i == depth - 1
        xp = _combine(nch, dst, rows_out, idx_c, rank_c, gate_c, seg_f, x1_c, mv(ctx_rows, 5), fg, 0, final)
        xs_lat = _combine(nch, dst, rows_out, idx_l, rank_l, gate_l, seg_f, x1_l, mv(lat_rows, 5), fg,
                          tiles_c, final)

    return (xp, xs_lat) + tuple(jnp.stack(parts, axis=1) for parts in new_cache)
```

```python
import functools
import math

import numpy as np
import jax
import jax.numpy as jnp
from jax import lax
from jax.experimental import pallas as pl
from jax.experimental.pallas import tpu as pltpu

F32 = jnp.float32
BF16 = jnp.bfloat16
I32 = jnp.int32
U32 = jnp.uint32

D_MODEL = 1024
GRID_W = 64
ROPE_BASE = 10000.0
NORM_EPS = 1e-6
A_HD = 32
A_HEADS = 4
A_WIDTH = 256
B_WIDTH = 256
B_GROUPS = 4
B_GC = 64
C_HD = 64
C_HEADS = 8
C_KV = 2
C_GROUP = 4
C_WIDTH = 512
WINDOW = 128
QBLOCK = 128
N_EXPERTS = 32
TOP_K = 4
D_FF = 1024
SWIGLU_LIMIT = 7.0
SWIGLU_ALPHA = 1.702

LANES = 128
ROW_TILE = 256
MOE_ROWS = 256
KEY_CHUNK = 512
NEG = -1e30
LOG2E = math.log2(math.e)
VMEM_LIMIT = 56 * 1024 * 1024


def _params(*sem):
    return pltpu.CompilerParams(dimension_semantics=sem, vmem_limit_bytes=VMEM_LIMIT)


def _split(x):
    hi = x.astype(BF16)
    lo = (x - hi.astype(F32)).astype(BF16)
    return hi, lo


def _dot3(a, b):
    ah, al = _split(a)
    bh, bl = _split(b)
    return (jnp.dot(ah, bh, preferred_element_type=F32)
            + (jnp.dot(ah, bl, preferred_element_type=F32)
               + jnp.dot(al, bh, preferred_element_type=F32)))


def _nt_dot(a, b):
    return lax.dot_general(a, b, (((1,), (1,)), ((), ())), preferred_element_type=F32)


def _rms(x):
    return x * lax.rsqrt(jnp.mean(x * x, axis=-1, keepdims=True) + NORM_EPS)


HALF = D_MODEL // 2
HIGH_BITS = 0xFFFF0000


def _pack_rows(x):
    lo = lax.bitcast_convert_type(x[:, :HALF], U32) >> 16
    hi = lax.bitcast_convert_type(x[:, HALF:], U32) & jnp.uint32(HIGH_BITS)
    return lo | hi


def _unpack_rows(w):
    lo = lax.bitcast_convert_type(w << 16, F32).astype(BF16)
    hi = lax.bitcast_convert_type(w & jnp.uint32(HIGH_BITS), F32).astype(BF16)
    return lo, hi


def _mod_kernel(c_ref, w_ref, b_ref, o_ref):
    c = c_ref[...]
    a = c * (1.0 / (1.0 + jnp.exp(-c)))
    o_ref[0] = _dot3(a, w_ref[0]) + b_ref[0]


def _mod_vectors(cmat, w_mod, b_mod):
    depth, d, e = w_mod.shape
    rows = cmat.shape[0]
    tn = 512
    return pl.pallas_call(
        _mod_kernel,
        grid=(depth, e // tn),
        in_specs=[pl.BlockSpec((rows, d), lambda l, j: (0, 0)),
                  pl.BlockSpec((1, d, tn), lambda l, j: (l, 0, j)),
                  pl.BlockSpec((1, 1, tn), lambda l, j: (l, 0, j))],
        out_specs=pl.BlockSpec((1, rows, tn), lambda l, j: (l, 0, j)),
        out_shape=jax.ShapeDtypeStruct((depth, rows, e), F32),
        compiler_params=_params("arbitrary", "arbitrary"),
    )(cmat, w_mod, b_mod.reshape(depth, 1, e))


def _rope(z, col_ref, row_ref, nf, r0):
    tm, w = z.shape
    outs = []
    for g in range(tm // GRID_W):
        zs = z[GRID_W * g:GRID_W * (g + 1), :]
        c = col_ref[0] + row_ref[0, pl.ds(r0 + g, 1), :]
        sm = col_ref[1] + row_ref[1, pl.ds(r0 + g, 1), :]
        sp = col_ref[2] + row_ref[2, pl.ds(r0 + g, 1), :]
        outs.append(zs * c + pltpu.roll(zs, w - nf, 1) * sm + pltpu.roll(zs, nf, 1) * sp)
    return jnp.concatenate(outs, axis=0)


def _inproj_kernel(*refs, rope, tm):
    x_ref, sh_ref, sc_ref, n1_ref, w_ref = refs[:5]
    if rope:
        ta_col, ta_row, tc_col, tc_row = refs[5:9]
        qa_o, ka_o, va_o, u_o, qc_o, kr_o, vr_o = refs[9:]
    else:
        qa_o, ka_o, va_o, u_o, qc_o, kr_o, vr_o, ka32_o, va32_o, kc32_o, vc32_o = refs[5:]
    x = x_ref[0]
    h = (_rms(x) * n1_ref[...]) * (1.0 + sc_ref[0]) + sh_ref[0]
    hb = h.astype(BF16)

    def seg(a, b):
        return jnp.dot(hb, w_ref[:, a:b], preferred_element_type=F32)

    qa, ka, va, u = seg(0, 256), seg(256, 512), seg(512, 768), seg(768, 1024)
    qc, kr, vr = seg(1024, 1536), seg(1536, 2048), seg(2048, 2560)
    if rope:
        r0 = pl.program_id(1) * (tm // GRID_W)
        qa = _rope(qa, ta_col, ta_row, A_HD // 4, r0)
        ka = _rope(ka, ta_col, ta_row, A_HD // 4, r0)
        qc = _rope(qc, tc_col, tc_row, C_HD // 4, r0)
        kr = _rope(kr, tc_col, tc_row, C_HD // 4, r0)
    else:
        ka32_o[0] = ka
        va32_o[0] = va
        kc32_o[0] = seg(2560, 2688)
        vc32_o[0] = seg(2688, 2816)
    qa_o[0] = (qa * (A_HD ** -0.5 * LOG2E)).astype(BF16)
    ka_o[0] = ka.astype(BF16)
    va_o[0] = va.astype(BF16)
    u_o[0] = u
    qc_o[0] = (qc * (C_HD ** -0.5 * LOG2E)).astype(BF16)
    kr_o[0] = kr.astype(BF16)
    vr_o[0] = vr.astype(BF16)


def _inproj(x, sh, sc, n1, w, tables):
    b, l, d = x.shape
    tm = min(ROW_TILE, l)
    rope = tables is not None
    per_batch = sh.shape[0] > 1
    bidx = (lambda i, t: (i, 0, 0)) if per_batch else (lambda i, t: (0, 0, 0))
    tok = lambda width: pl.BlockSpec((1, tm, width), lambda i, t: (i, t, 0))
    in_specs = [tok(d),
                pl.BlockSpec((1, 1, d), bidx), pl.BlockSpec((1, 1, d), bidx),
                pl.BlockSpec((1, d), lambda i, t: (0, 0)),
                pl.BlockSpec(w.shape, lambda i, t: (0, 0))]
    args = [x, sh, sc, n1, w]
    widths = [(256, BF16), (256, BF16), (256, BF16), (256, F32), (512, BF16), (512, BF16), (512, BF16)]
    if rope:
        for tab in tables:
            in_specs.append(pl.BlockSpec(tab.shape, lambda i, t: (0, 0, 0)))
            args.append(tab)
    else:
        widths += [(256, F32), (256, F32), (128, F32), (128, F32)]
    return pl.pallas_call(
        functools.partial(_inproj_kernel, rope=rope, tm=tm),
        grid=(b, l // tm),
        in_specs=in_specs,
        out_specs=[tok(wd) for wd, _ in widths],
        out_shape=[jax.ShapeDtypeStruct((b, l, wd), dt) for wd, dt in widths],
        compiler_params=_params("arbitrary", "arbitrary"),
    )(*args)


def _rope_tables(n_lat, head_dim, width):
    rows = n_lat // GRID_W
    nf = head_dim // 4
    inv = ROPE_BASE ** (-jnp.arange(nf, dtype=F32) / nf)
    lane = np.arange(width) % head_dim
    half = lane // (2 * nf)
    pair = (lane // nf) % 2
    f = lane % nf

    def part(pos, which):
        ang = pos[:, None] * inv[f][None, :]
        on = jnp.asarray(half == which, F32)[None, :]
        c = jnp.cos(ang) * on
        s = jnp.sin(ang) * on
        sm = -s * jnp.asarray(pair == 0, F32)[None, :]
        sp = s * jnp.asarray(pair == 1, F32)[None, :]
        return jnp.stack([c, sm, sp])

    return part(jnp.arange(GRID_W, dtype=F32), 1), part(jnp.arange(rows, dtype=F32), 0)


POOL_PAD = 8
POOL_CHUNK = 256


def _pool_kernel(u_ref, w_ref, ps_ref, o_ref, pad_ref, *, l):
    zeros = jnp.zeros((POOL_PAD, B_WIDTH), F32)
    pad_ref[0:POOL_PAD, :] = zeros
    pad_ref[POOL_PAD + l:2 * POOL_PAD + l, :] = zeros
    pad_ref[POOL_PAD:POOL_PAD + l, :] = u_ref[0]
    ch = min(POOL_CHUNK, l)
    lane = lax.broadcasted_iota(I32, (ch, B_WIDTH), 1)
    grp = lane >> 6
    half = jnp.where(grp == 0, 1, jnp.where(grp == 1, 2, jnp.where(grp == 2, 4, 8)))
    row = lax.broadcasted_iota(I32, (ch, B_WIDTH), 0)
    for c in range(0, l, ch):
        ld = lambda k: pad_ref[c + POOL_PAD + k:c + POOL_PAD + k + ch, :]
        cur = ld(0)
        s2 = ld(-1) + cur
        s4 = s2 + (ld(-2) + ld(1))
        s8 = s4 + ((ld(-4) + ld(-3)) + (ld(2) + ld(3)))
        s16 = s8 + (((ld(-8) + ld(-7)) + (ld(-6) + ld(-5))) + ((ld(4) + ld(5)) + (ld(6) + ld(7))))
        win = jnp.where(grp == 0, s2, jnp.where(grp == 1, s4, jnp.where(grp == 2, s8, s16)))
        t = row + c
        cnt = (jnp.minimum(t + half, l) - jnp.maximum(t - half, 0)).astype(F32)
        r = (win / cnt - cur).astype(BF16)
        y = jnp.dot(r, w_ref[...], preferred_element_type=F32) * ps_ref[...]
        o_ref[0, c:c + ch, :] = y.astype(BF16)


def _pool(u, wbd, ps):
    b, l, w = u.shape
    return pl.pallas_call(
        functools.partial(_pool_kernel, l=l),
        grid=(b,),
        in_specs=[pl.BlockSpec((1, l, w), lambda i: (i, 0, 0)),
                  pl.BlockSpec((w, w), lambda i: (0, 0)),
                  pl.BlockSpec((1, w), lambda i: (0, 0))],
        out_specs=pl.BlockSpec((1, l, w), lambda i: (i, 0, 0)),
        out_shape=jax.ShapeDtypeStruct((b, l, w), BF16),
        scratch_shapes=[pltpu.VMEM((l + 2 * POOL_PAD, w), F32)],
        compiler_params=_params("arbitrary"),
    )(u, wbd, ps)


def _diff_attn_kernel(q_ref, k_ref, v_ref, lamp_ref, g_ref, o_ref, s_scr, p_scr, *, tq, s_len, lam_init):
    tk = min(s_len, KEY_CHUNK)
    q32 = q_ref[0].astype(F32)
    lp = lamp_ref[...]
    lam = (jnp.exp(jnp.sum(lp[0:1] * lp[1:2], axis=-1, keepdims=True))
           - jnp.exp(jnp.sum(lp[2:3] * lp[3:4], axis=-1, keepdims=True)) + lam_init)
    lane_row = lax.broadcasted_iota(I32, (1, A_WIDTH), 1)
    lane = lax.broadcasted_iota(I32, (tq, A_WIDTH), 1)

    def scores(h):
        row_max = []
        for mp in range(2):
            qm = (q32 * ((lane_row >> 5) == 2 * h + mp).astype(F32)).astype(BF16)
            part = jnp.full((tq, LANES), NEG, F32)
            for c in range(0, s_len, tk):
                s = _nt_dot(qm, k_ref[0, c:c + tk, :])
                s_scr[2 * (h % 2) + mp, :, c:c + tk] = s
                for j in range(0, tk, LANES):
                    part = jnp.maximum(part, s[:, j:j + LANES])
            row_max.append(jnp.max(part, axis=-1, keepdims=True))
        return row_max

    def weights(h, row_max):
        dens = []
        for mp in range(2):
            slot = 2 * (h % 2) + mp
            part = jnp.zeros((tq, LANES), F32)
            for c in range(0, s_len, tk):
                p = jnp.exp2(s_scr[slot, :, c:c + tk] - row_max[mp])
                for j in range(0, tk, LANES):
                    part = part + p[:, j:j + LANES]
                p_scr[slot, :, c:c + tk] = p.astype(BF16)
            dens.append(jnp.sum(part, axis=-1, keepdims=True))
        return dens

    def values(h, dens):
        maps = [jnp.dot(p_scr[2 * (h % 2) + mp], v_ref[0], preferred_element_type=F32) / dens[mp]
                for mp in range(2)]
        return jnp.where((lane >> 6) == h, maps[0] - lam * maps[1], 0.0)

    out = jnp.zeros((tq, A_WIDTH), F32)
    row_max = scores(0)
    for h in range(A_HEADS):
        dens = weights(h, row_max)
        if h + 1 < A_HEADS:
            row_max = scores(h + 1)
        out = out + values(h, dens)
    sq = out * out
    rs = jnp.zeros((tq, A_WIDTH), F32)
    for h in range(A_HEADS):
        msk = (lane >> 6) == h
        ms = jnp.sum(jnp.where(msk, sq, 0.0), axis=-1, keepdims=True) * (1.0 / (2 * A_HD))
        rs = rs + jnp.where(msk, lax.rsqrt(ms + NORM_EPS), 0.0)
    o_ref[0] = (((out * rs) * g_ref[...]) * (1.0 - lam_init)).astype(BF16)


def _diff_attn(q, k, v, lam_p, g_tiled, lam_init):
    b, l, w = q.shape
    s_len = k.shape[1]
    tq = min(ROW_TILE, l)
    whole = pl.BlockSpec((1, s_len, w), lambda i, t: (i, 0, 0))
    return pl.pallas_call(
        functools.partial(_diff_attn_kernel, tq=tq, s_len=s_len, lam_init=lam_init),
        grid=(b, l // tq),
        in_specs=[pl.BlockSpec((1, tq, w), lambda i, t: (i, t, 0)), whole, whole,
                  pl.BlockSpec(lam_p.shape, lambda i, t: (0, 0)),
                  pl.BlockSpec(g_tiled.shape, lambda i, t: (0, 0))],
        out_specs=pl.BlockSpec((1, tq, w), lambda i, t: (i, t, 0)),
        out_shape=jax.ShapeDtypeStruct((b, l, w), BF16),
        scratch_shapes=[pltpu.VMEM((4, tq, s_len), F32), pltpu.VMEM((4, tq, s_len), BF16)],
        compiler_params=_params("arbitrary", "arbitrary"),
    )(q, k, v, lam_p, g_tiled)


GQA_SLAB = C_GROUP * C_HD


def _gqa_kernel(*refs, windowed, tq, l):
    if windowed:
        sink_ref, q_ref, k_ref, v_ref, kc_ref, vc_ref, o_ref = refs
    else:
        sink_ref, q_ref, k_ref, v_ref, o_ref = refs
    i = pl.program_id(1)
    rows = C_GROUP * tq
    shift = int(math.log2(tq))
    q32 = q_ref[0].astype(F32)
    lane_row = lax.broadcasted_iota(I32, (1, GQA_SLAB), 1)
    lane = lax.broadcasted_iota(I32, (tq, GQA_SLAB), 1)
    rid = lax.broadcasted_iota(I32, (rows, 1), 0)
    if windowed:
        ws = pl.multiple_of(jnp.clip((i - 1) * tq, 0, l - 3 * tq), tq)
        qpos = i * tq + (rid & (tq - 1))
        kpos = ws + lax.broadcasted_iota(I32, (1, 3 * tq), 1)
        band = jnp.where(jnp.abs(kpos - qpos) <= WINDOW, 0.0, NEG)
    for g in range(C_KV):
        sl = slice(GQA_SLAB * g, GQA_SLAB * (g + 1))
        qg = q32[:, sl]
        qs = jnp.concatenate(
            [(qg * ((lane_row >> 6) == hh).astype(F32)).astype(BF16) for hh in range(C_GROUP)], axis=0)
        sk = jnp.zeros((rows, 1), F32)
        for hh in range(C_GROUP):
            sk = jnp.where((rid >> shift) == hh, sink_ref[C_GROUP * g + hh] * LOG2E, sk)
        if windowed:
            s_loc = _nt_dot(qs, k_ref[0, pl.ds(ws, 3 * tq), sl]) + band
            s_ctx = _nt_dot(qs, kc_ref[0, :, sl])
            m = jnp.maximum(sk, jnp.maximum(jnp.max(s_loc, axis=-1, keepdims=True),
                                            jnp.max(s_ctx, axis=-1, keepdims=True)))
            p_loc = jnp.exp2(s_loc - m)
            p_ctx = jnp.exp2(s_ctx - m)
            den = (jnp.exp2(sk - m) + jnp.sum(p_loc, axis=-1, keepdims=True)
                   + jnp.sum(p_ctx, axis=-1, keepdims=True))
            pv = (jnp.dot(p_loc.astype(BF16), v_ref[0, pl.ds(ws, 3 * tq), sl], preferred_element_type=F32)
                  + jnp.dot(p_ctx.astype(BF16), vc_ref[0, :, sl], preferred_element_type=F32))
        else:
            s = _nt_dot(qs, k_ref[0, :, sl])
            m = jnp.maximum(sk, jnp.max(s, axis=-1, keepdims=True))
            p = jnp.exp2(s - m)
            den = jnp.exp2(sk - m) + jnp.sum(p, axis=-1, keepdims=True)
            pv = jnp.dot(p.astype(BF16), v_ref[0, :, sl], preferred_element_type=F32)
        o = pv / den
        og = jnp.zeros((tq, GQA_SLAB), F32)
        for hh in range(C_GROUP):
            og = og + jnp.where((lane >> 6) == hh, o[hh * tq:(hh + 1) * tq, :], 0.0)
        o_ref[0, :, sl] = og.astype(BF16)


def _gqa(sink, q, k, v, cache=None):
    b, l, w = q.shape
    windowed = cache is not None
    tq = QBLOCK
    whole = lambda a: pl.BlockSpec((1,) + a.shape[1:], lambda i, t: (i, 0, 0))
    in_specs = [pl.BlockSpec(memory_space=pltpu.SMEM),
                pl.BlockSpec((1, tq, w), lambda i, t: (i, t, 0)), whole(k), whole(v)]
    args = [sink, q, k, v]
    if windowed:
        in_specs += [whole(cache[0]), whole(cache[1])]
        args += list(cache)
    return pl.pallas_call(
        functools.partial(_gqa_kernel, windowed=windowed, tq=tq, l=l),
        grid=(b, l // tq),
        in_specs=in_specs,
        out_specs=pl.BlockSpec((1, tq, w), lambda i, t: (i, t, 0)),
        out_shape=jax.ShapeDtypeStruct((b, l, w), BF16),
        compiler_params=_params("arbitrary", "arbitrary"),
    )(*args)


def _finish_kernel(x_ref, oa_ref, ob_ref, oc_ref, wo_ref, g1_ref, sh_ref, sc_ref, n2_ref, wr_ref, br_ref,
                   x1_o, h2_o, idx_o, gate_o, rank_o, cnt_o, *, tm):
    y = (jnp.dot(oa_ref[0], wo_ref[0:256, :], preferred_element_type=F32)
         + jnp.dot(ob_ref[0], wo_ref[256:512, :], preferred_element_type=F32)
         + jnp.dot(oc_ref[0], wo_ref[512:1024, :], preferred_element_type=F32))
    x1 = x_ref[0] + g1_ref[0] * y
    x1_o[0] = x1
    h2 = (_rms(x1) * n2_ref[...]) * (1.0 + sc_ref[0]) + sh_ref[0]
    h2_o[0] = h2
    lane = lax.broadcasted_iota(I32, (tm, LANES), 1)
    lanef = lane.astype(F32)
    logits = jnp.where(lane < N_EXPERTS, _dot3(h2, wr_ref[...]) + br_ref[...], NEG)
    work = logits
    tops, sels = [], []
    for k in range(TOP_K):
        mk = jnp.max(work, axis=-1, keepdims=True)
        ik = jnp.min(jnp.where(work == mk, lanef, float(LANES)), axis=-1, keepdims=True)
        sel = lanef == ik
        work = jnp.where(sel, 2.0 * NEG, work)
        tops.append((mk, ik))
        sels.append(sel)
    es = [jnp.exp(mk - tops[0][0]) for mk, _ in tops]
    den = (es[0] + es[1]) + (es[2] + es[3])
    multi = jnp.zeros((tm, LANES), F32)
    for sel in sels:
        multi = multi + sel.astype(F32)
    r_i = lax.broadcasted_iota(I32, (tm, tm), 0)
    c_i = lax.broadcasted_iota(I32, (tm, tm), 1)
    before = (c_i < r_i).astype(BF16)
    prior = jnp.dot(before, multi.astype(BF16), preferred_element_type=F32)
    idx = jnp.zeros((tm, LANES), F32)
    gate = jnp.zeros((tm, LANES), F32)
    rank = jnp.zeros((tm, LANES), F32)
    for k in range(TOP_K):
        slot = lane == k
        idx = jnp.where(slot, tops[k][1], idx)
        gate = jnp.where(slot, es[k] / den, gate)
        rk = jnp.sum(jnp.where(sels[k], prior, 0.0), axis=-1, keepdims=True)
        rank = jnp.where(slot, rk, rank)
    idx_o[0] = idx.astype(I32)
    gate_o[0] = gate
    rank_o[0] = rank.astype(I32)
    cnt_o[0] = jnp.sum(multi, axis=0, keepdims=True)


def _finish(x, oa, ob, oc, wo, g1, sh2, sc2, n2, wr, br):
    b, l, d = x.shape
    tm = min(ROW_TILE, l)
    nt = l // tm
    per_batch = g1.shape[0] > 1
    bidx = (lambda i, t: (i, 0, 0)) if per_batch else (lambda i, t: (0, 0, 0))
    tok = lambda width: pl.BlockSpec((1, tm, width), lambda i, t: (i, t, 0))
    const2 = lambda a: pl.BlockSpec(a.shape, lambda i, t: (0, 0))
    vec = pl.BlockSpec((1, 1, d), bidx)
    return pl.pallas_call(
        functools.partial(_finish_kernel, tm=tm),
        grid=(b, l // tm),
        in_specs=[tok(d), tok(256), tok(256), tok(512), const2(wo), vec, vec, vec,
                  const2(n2), const2(wr), const2(br)],
        out_specs=[tok(d), tok(d), tok(LANES), tok(LANES), tok(LANES),
                   pl.BlockSpec((1, 1, LANES), lambda i, t: (i * nt + t, 0, 0))],
        out_shape=[jax.ShapeDtypeStruct((b, l, d), F32), jax.ShapeDtypeStruct((b, l, d), F32),
                   jax.ShapeDtypeStruct((b, l, LANES), I32), jax.ShapeDtypeStruct((b, l, LANES), F32),
                   jax.ShapeDtypeStruct((b, l, LANES), I32), jax.ShapeDtypeStruct((b * nt, 1, LANES), F32)],
        compiler_params=_params("arbitrary", "arbitrary"),
    )(x, oa, ob, oc, wo, g1, sh2, sc2, n2, wr, br)


CHUNK = 8
SORT_ROWS = ROW_TILE * TOP_K + N_EXPERTS * CHUNK
MAX_CHUNKS = SORT_ROWS // CHUNK


def _slot_values(idx, rank, seg, k):
    lane = lax.broadcasted_iota(I32, idx.shape, 1)
    return jnp.where(lane == idx[:, k:k + 1], seg + rank[:, k:k + 1].astype(F32), 0.0)


def _dispatch_kernel(nch_ref, dst_ref, h_ref, idx_ref, rank_ref, seg_ref, xs_in_ref, xs_ref, srt, sems, *, tm, t0):
    del xs_in_ref
    tile = t0 + pl.program_id(0)
    idx, rank, seg = idx_ref[...], rank_ref[...], seg_ref[0]
    row = lax.broadcasted_iota(I32, (SORT_ROWS, tm), 0).astype(F32)
    ones = jnp.ones((CHUNK, LANES), BF16)
    perm = jnp.zeros((SORT_ROWS, tm), F32)
    for k in range(TOP_K):
        hi, lo = _split(_slot_values(idx, rank, seg, k))
        pos = (_nt_dot(ones, hi) + _nt_dot(ones, lo))[0:1, :]
        perm = perm + (row == pos).astype(F32)
    step = pl.program_id(0)
    slot = step % 2
    srt[slot] = _pack_rows(jnp.dot(perm.astype(BF16), h_ref[...].astype(BF16), preferred_element_type=F32))

    def chunk_copy(tl, sl, j):
        src = pl.multiple_of(j * CHUNK, CHUNK)
        dst = pl.multiple_of(dst_ref[tl * MAX_CHUNKS + j], CHUNK)
        return pltpu.make_async_copy(srt.at[sl, pl.ds(src, CHUNK)], xs_ref.at[pl.ds(dst, CHUNK)], sems.at[sl])

    def issue(j, carry):
        chunk_copy(tile, slot, j).start()
        return carry

    lax.fori_loop(0, nch_ref[tile], issue, 0)

    @pl.when(step > 0)
    def _():
        def drain(j, carry):
            chunk_copy(tile - 1, 1 - slot, j).wait()
            return carry
        lax.fori_loop(0, nch_ref[tile - 1], drain, 0)

    @pl.when(step == pl.num_programs(0) - 1)
    def _():
        def drain(j, carry):
            chunk_copy(tile, slot, j).wait()
            return carry
        lax.fori_loop(0, nch_ref[tile], drain, 0)


def _dispatch(nch, dst, h2, idx, rank, seg, xs, t0):
    n, d = h2.shape
    tm = ROW_TILE
    tok = lambda width: pl.BlockSpec((tm, width), lambda i, a, b: (i, 0))
    return pl.pallas_call(
        functools.partial(_dispatch_kernel, tm=tm, t0=t0),
        grid_spec=pltpu.PrefetchScalarGridSpec(
            num_scalar_prefetch=2,
            grid=(n // tm,),
            in_specs=[tok(d), tok(LANES), tok(LANES),
                      pl.BlockSpec((1, 1, LANES), lambda i, a, b: (t0 + i, 0, 0)),
                      pl.BlockSpec(memory_space=pl.ANY)],
            out_specs=pl.BlockSpec(memory_space=pl.ANY),
            scratch_shapes=[pltpu.VMEM((2, SORT_ROWS, HALF), U32), pltpu.SemaphoreType.DMA((2,))]),
        out_shape=jax.ShapeDtypeStruct(xs.shape, xs.dtype),
        input_output_aliases={6: 0},
        compiler_params=_params("arbitrary"),
    )(nch, dst, h2, idx, rank, seg, xs)


def _expert_kernel(be_ref, nx_ref, nu_ref, xs_ref, w1_ref, b1_ref, w2_ref, b2_ref, o_ref,
                   w1s, w2s, w1b, w2b, sems):
    i = pl.program_id(0)
    e = be_ref[i]
    prev = be_ref[jnp.maximum(i - 1, 0)]

    def fetch(expert):
        return (pltpu.make_async_copy(w1_ref.at[expert], w1s, sems.at[0]),
                pltpu.make_async_copy(w2_ref.at[expert], w2s, sems.at[1]))

    @pl.when(i == 0)
    def _():
        for cp in fetch(e):
            cp.start()

    @pl.when((i < nu_ref[0]) & ((i == 0) | (e != prev)))
    def _():
        for cp in fetch(e):
            cp.wait()
        w1b[...] = w1s[...].astype(BF16)
        w2b[...] = w2s[...].astype(BF16)

        @pl.when(nx_ref[i] != e)
        def _():
            for cp in fetch(nx_ref[i]):
                cp.start()

    @pl.when(i < nu_ref[0])
    def _():
        x_lo, x_hi = _unpack_rows(xs_ref[...])
        gu = (jnp.dot(x_lo, w1b[:HALF, :], preferred_element_type=F32)
              + jnp.dot(x_hi, w1b[HALF:, :], preferred_element_type=F32)) + b1_ref[0]
        gt = jnp.minimum(gu[:, :D_FF], SWIGLU_LIMIT)
        up = jnp.clip(gu[:, D_FF:], -SWIGLU_LIMIT, SWIGLU_LIMIT)
        hid = (up + 1.0) * gt * (1.0 / (1.0 + jnp.exp(-SWIGLU_ALPHA * gt)))
        out = jnp.dot(hid.astype(BF16), w2b[...], preferred_element_type=F32) + b2_ref[0]
        o_ref[...] = _pack_rows(out.astype(BF16).astype(F32))

    @pl.when(i >= nu_ref[0])
    def _():
        o_ref[...] = jnp.zeros_like(o_ref)


def _experts(block_expert, next_expert, n_used, xs, w1, b1, w2, b2):
    n_rows, half = xs.shape
    d = 2 * half
    n_blocks = n_rows // MOE_ROWS
    f2 = w1.shape[-1]
    ne = w1.shape[0] * w1.shape[1]
    w1 = w1.reshape(ne, d, f2)
    w2 = w2.reshape(ne, D_FF, d)
    row = lambda i, be, nx, nu: (jnp.minimum(i, nu[0] - 1), 0)
    return pl.pallas_call(
        _expert_kernel,
        grid_spec=pltpu.PrefetchScalarGridSpec(
            num_scalar_prefetch=3,
            grid=(n_blocks,),
            in_specs=[pl.BlockSpec((MOE_ROWS, half), row),
                      pl.BlockSpec(memory_space=pl.ANY),
                      pl.BlockSpec((1, 1, f2), lambda i, be, nx, nu: (be[i], 0, 0)),
                      pl.BlockSpec(memory_space=pl.ANY),
                      pl.BlockSpec((1, 1, d), lambda i, be, nx, nu: (be[i], 0, 0))],
            out_specs=pl.BlockSpec((MOE_ROWS, half), lambda i, be, nx, nu: (i, 0)),
            scratch_shapes=[pltpu.VMEM((d, f2), F32), pltpu.VMEM((D_FF, d), F32),
                            pltpu.VMEM((d, f2), BF16), pltpu.VMEM((D_FF, d), BF16),
                            pltpu.SemaphoreType.DMA((2,))]),
        out_shape=jax.ShapeDtypeStruct((n_rows, half), U32),
        compiler_params=_params("arbitrary"),
    )(block_expert, next_expert, n_used, xs, w1, b1.reshape(ne, 1, f2), w2, b2.reshape(ne, 1, d))


def _combine_kernel(nch_ref, dst_ref, outs_ref, idx_ref, rank_ref, gate_ref, seg_ref, x1_ref, g2_ref, fg_ref,
                    o_ref, buf, sems, *, tm, nt, t0, final):
    step = pl.program_id(0) * nt + pl.program_id(1)
    tile = t0 + step
    slot = step % 2

    def chunk_copy(tl, sl, j):
        src = pl.multiple_of(dst_ref[tl * MAX_CHUNKS + j], CHUNK)
        dst = pl.multiple_of(j * CHUNK, CHUNK)
        return pltpu.make_async_copy(outs_ref.at[pl.ds(src, CHUNK)], buf.at[sl, pl.ds(dst, CHUNK)], sems.at[sl])

    def fetch(tl, sl):
        def issue(j, carry):
            chunk_copy(tl, sl, j).start()
            return carry
        lax.fori_loop(0, nch_ref[tl], issue, 0)

    @pl.when(step == 0)
    def _():
        buf[...] = jnp.zeros_like(buf)
        fetch(tile, slot)

    @pl.when(step + 1 < pl.num_programs(0) * nt)
    def _():
        fetch(tile + 1, 1 - slot)

    def drain(j, carry):
        chunk_copy(tile, slot, j).wait()
        return carry

    n = nch_ref[tile]
    idx, rank, gate, seg = idx_ref[0], rank_ref[0], gate_ref[0], seg_ref[0]
    col = lax.broadcasted_iota(I32, (tm, SORT_ROWS), 1).astype(F32)
    weights = jnp.zeros((tm, SORT_ROWS), F32)
    for k in range(TOP_K):
        pos = jnp.sum(_slot_values(idx, rank, seg, k), axis=-1, keepdims=True)
        weights = weights + jnp.where(col == pos, gate[:, k:k + 1], 0.0)
    wh, wl = _split(weights)
    lax.fori_loop(0, n, drain, 0)
    y = jnp.concatenate(
        [jnp.dot(wh, rows, preferred_element_type=F32) + jnp.dot(wl, rows, preferred_element_type=F32)
         for rows in _unpack_rows(buf[slot])], axis=1)
    x2 = x1_ref[0] + g2_ref[0] * y
    if final:
        x2 = _rms(x2) * fg_ref[...]
    o_ref[0] = x2


def _combine(nch, dst, outs, idx, rank, gate, seg, x1, g2, fg, t0, final):
    b, l, d = x1.shape
    tm = ROW_TILE
    nt = l // tm
    per_batch = g2.shape[0] > 1
    bidx = (lambda i, t, a, c: (i, 0, 0)) if per_batch else (lambda i, t, a, c: (0, 0, 0))
    tok = lambda width: pl.BlockSpec((1, tm, width), lambda i, t, a, c: (i, t, 0))
    return pl.pallas_call(
        functools.partial(_combine_kernel, tm=tm, nt=nt, t0=t0, final=final),
        grid_spec=pltpu.PrefetchScalarGridSpec(
            num_scalar_prefetch=2,
            grid=(b, nt),
            in_specs=[pl.BlockSpec(memory_space=pl.ANY), tok(LANES), tok(LANES), tok(LANES),
                      pl.BlockSpec((1, 1, LANES), lambda i, t, a, c: (t0 + i * nt + t, 0, 0)),
                      tok(d), pl.BlockSpec((1, 1, d), bidx),
                      pl.BlockSpec((1, d), lambda i, t, a, c: (0, 0))],
            out_specs=tok(d),
            scratch_shapes=[pltpu.VMEM((2, SORT_ROWS, HALF), U32), pltpu.SemaphoreType.DMA((2,))]),
        out_shape=jax.ShapeDtypeStruct((b, l, d), F32),
        compiler_params=_params("arbitrary", "arbitrary"),
    )(nch, dst, outs, idx, rank, gate, seg, x1, g2, fg)


def _layer_weights(w_in_l):
    cuts = np.cumsum([A_WIDTH, A_WIDTH, A_WIDTH, B_WIDTH, C_WIDTH, C_KV * C_HD]).tolist()
    front = w_in_l[:, :cuts[4]]
    kc = w_in_l[:, cuts[4]:cuts[5]]
    vc = w_in_l[:, cuts[5]:]

    def rep(w):
        return jnp.concatenate([w[:, C_HD * (j // C_GROUP):C_HD * (j // C_GROUP + 1)] for j in range(C_HEADS)], axis=1)

    lat = jnp.concatenate([front, rep(kc), rep(vc)], axis=1).astype(BF16)
    ctx = jnp.concatenate([front, rep(kc), rep(vc), kc, vc], axis=1).astype(BF16)
    return ctx, lat


def _rep_heads(a):
    return jnp.repeat(a, C_GROUP, axis=2).reshape(a.shape[0], a.shape[1], C_HEADS * C_HD)


def kernel(x_prompt, x_sample, c, cache_diff_k, cache_diff_v, cache_win_k, cache_win_v, c_ctx, norm1_g, norm2_g, w_mod, b_mod, w_in, diff_lambda, diff_subln_g, w_pool, pool_scale, sink, w_out, w_router, b_router, w1, b1, w2, b2, final_g):
    depth = w_in.shape[0]
    bc, lc, d = x_prompt.shape
    bl, ll, _ = x_sample.shape
    n_ctx, n_lat = bc * lc, bl * ll
    n_tok = n_ctx + n_lat
    n_tiles = n_tok // ROW_TILE
    n_blocks = -(-(n_tok * TOP_K + n_tiles * N_EXPERTS * (CHUNK - 1)) // MOE_ROWS) + N_EXPERTS
    n_rows = n_blocks * MOE_ROWS

    mod_rows = -(-(1 + bl) // 8) * 8
    cmat = jnp.zeros((mod_rows, d), F32).at[0].set(c_ctx).at[1:1 + bl].set(c)
    mod = _mod_vectors(cmat, w_mod, b_mod)
    tabs_a = _rope_tables(ll, A_HD, A_WIDTH)
    tabs_c = _rope_tables(ll, C_HD, C_WIDTH)
    tables = (tabs_a[0], tabs_a[1], tabs_c[0], tabs_c[1])
    fg = final_g.reshape(1, d)

    xp, xs_lat = x_prompt, x_sample
    new_cache = [[], [], [], []]
    for i in range(depth):
        lam_init = 0.8 - 0.6 * math.exp(-0.3 * i)
        mv = lambda rows, j: mod[i, rows, j * d:(j + 1) * d].reshape(-1, 1, d)
        ctx_rows, lat_rows = slice(0, 1), slice(1, 1 + bl)
        w_ctx, w_lat = _layer_weights(w_in[i])
        n1 = norm1_g[i].reshape(1, d)
        n2 = norm2_g[i].reshape(1, d)
        wbd = jax.scipy.linalg.block_diag(*[w_pool[i, g] for g in range(B_GROUPS)]).astype(BF16)
        ps = pool_scale[i].reshape(1, B_WIDTH)
        g_tiled = jnp.tile(diff_subln_g[i], A_HEADS).reshape(1, A_WIDTH)
        wo = w_out[i].astype(BF16)
        wr = jnp.zeros((d, LANES), F32).at[:, :N_EXPERTS].set(w_router[i])
        br = jnp.zeros((1, LANES), F32).at[0, :N_EXPERTS].set(b_router[i])

        qa, ka, va, u, qc, kr, vr, ka32, va32, kc32, vc32 = _inproj(
            xp, mv(ctx_rows, 0), mv(ctx_rows, 1), n1, w_ctx, None)
        new_cache[0].append(ka32.reshape(bc, lc, 2 * A_HEADS, A_HD))
        new_cache[1].append(va32.reshape(bc, lc, A_HEADS, 2 * A_HD))
        new_cache[2].append(kc32.reshape(bc, lc, C_KV, C_HD))
        new_cache[3].append(vc32.reshape(bc, lc, C_KV, C_HD))
        ob = _pool(u, wbd, ps)
        oa = _diff_attn(qa, ka, va, diff_lambda[i], g_tiled, lam_init)
        oc = _gqa(sink[i], qc, kr, vr)
        x1_c, h2_c, idx_c, gate_c, rank_c, cnt_c = _finish(
            xp, oa, ob, oc, wo, mv(ctx_rows, 2), mv(ctx_rows, 3), mv(ctx_rows, 4), n2, wr, br)

        qa, ka, va, u, qc, kr, vr = _inproj(xs_lat, mv(lat_rows, 0), mv(lat_rows, 1), n1, w_lat, tables)
        ob = _pool(u, wbd, ps)
        dk = cache_diff_k[:, i].reshape(bl, -1, A_WIDTH).astype(BF16)
        dv = cache_diff_v[:, i].reshape(bl, -1, A_WIDTH).astype(BF16)
        oa = _diff_attn(qa, jnp.concatenate([ka, dk], axis=1), jnp.concatenate([va, dv], axis=1),
                        diff_lambda[i], g_tiled, lam_init)
        wk = _rep_heads(cache_win_k[:, i]).astype(BF16)
        wv = _rep_heads(cache_win_v[:, i]).astype(BF16)
        oc = _gqa(sink[i], qc, kr, vr, (wk, wv))
        x1_l, h2_l, idx_l, gate_l, rank_l, cnt_l = _finish(
            xs_lat, oa, ob, oc, wo, mv(lat_rows, 2), mv(lat_rows, 3), mv(lat_rows, 4), n2, wr, br)

        cnt = jnp.concatenate([cnt_c, cnt_l], axis=0)[:, 0, :N_EXPERTS].astype(I32)
        c8 = (cnt + CHUNK - 1) // CHUNK * CHUNK
        seg_end = jnp.cumsum(c8, axis=1)
        seg = seg_end - c8
        padded = (jnp.sum(c8, axis=0) + MOE_ROWS - 1) // MOE_ROWS * MOE_ROWS
        pad_end = jnp.cumsum(padded)
        gbase = (pad_end - padded)[None, :] + jnp.cumsum(c8, axis=0) - c8
        nch = seg_end[:, -1] // CHUNK
        j8 = jnp.arange(MAX_CHUNKS, dtype=I32) * CHUNK
        chunk_e = jnp.minimum(jnp.sum((seg_end[:, None, :] <= j8[None, :, None]).astype(I32), axis=-1),
                              N_EXPERTS - 1)
        onehot = chunk_e[..., None] == jnp.arange(N_EXPERTS, dtype=I32)
        dst = (jnp.sum(jnp.where(onehot, (gbase - seg)[:, None, :], 0), axis=-1) + j8[None, :]).reshape(-1)
        seg_f = jnp.zeros((cnt.shape[0], 1, LANES), F32).at[:, 0, :N_EXPERTS].set(seg.astype(F32))
        block_row = jnp.arange(n_blocks, dtype=I32)[:, None] * MOE_ROWS
        block_expert = jnp.minimum(jnp.sum((pad_end[None, :] <= block_row).astype(I32), axis=-1), N_EXPERTS - 1)
        n_used = (pad_end[-1:] // MOE_ROWS).astype(I32)
        experts = jnp.arange(N_EXPERTS, dtype=I32)
        later = (experts[None, :] > experts[:, None]) & (padded[None, :] > 0)
        next_of = jnp.min(jnp.where(later, experts[None, :], N_EXPERTS), axis=1)
        next_of = jnp.where(next_of == N_EXPERTS, experts, next_of)
        next_expert = jnp.sum(jnp.where(block_expert[:, None] == experts, next_of, 0), axis=-1)
        tiles_c = cnt_c.shape[0]

        rows_in = jnp.zeros((n_rows, HALF), U32)
        rows_in = _dispatch(nch, dst, h2_c.reshape(n_ctx, d), idx_c.reshape(n_ctx, LANES),
                            rank_c.reshape(n_ctx, LANES), seg_f, rows_in, 0)
        rows_in = _dispatch(nch, dst, h2_l.reshape(n_lat, d), idx_l.reshape(n_lat, LANES),
                            rank_l.reshape(n_lat, LANES), seg_f, rows_in, tiles_c)
        rows_out = _experts(block_expert + i * N_EXPERTS, next_expert + i * N_EXPERTS, n_used, rows_in,
                            w1, b1, w2, b2)
        final = i == depth - 1
        xp = _combine(nch, dst, rows_out, idx_c, rank_c, gate_c, seg_f, x1_c, mv(ctx_rows, 5), fg, 0, final)
        xs_lat = _combine(nch, dst, rows_out, idx_l, rank_l, gate_l, seg_f, x1_l, mv(lat_rows, 5), fg,
                          tiles_c, final)

    return (xp, xs_lat) + tuple(jnp.stack(parts, axis=1) for parts in new_cache)
```

```python
import functools
import math

import numpy as np
import jax
import jax.numpy as jnp
from jax import lax
from jax.experimental import pallas as pl
from jax.experimental.pallas import tpu as pltpu

F32 = jnp.float32
BF16 = jnp.bfloat16
I32 = jnp.int32
U32 = jnp.uint32

D_MODEL = 1024
GRID_W = 64
ROPE_BASE = 10000.0
NORM_EPS = 1e-6
A_HD = 32
A_HEADS = 4
A_WIDTH = 256
B_WIDTH = 256
B_GROUPS = 4
B_GC = 64
C_HD = 64
C_HEADS = 8
C_KV = 2
C_GROUP = 4
C_WIDTH = 512
WINDOW = 128
QBLOCK = 128
N_EXPERTS = 32
TOP_K = 4
D_FF = 1024
SWIGLU_LIMIT = 7.0
SWIGLU_ALPHA = 1.702

LANES = 128
ROW_TILE = 256
MOE_ROWS = 256
KEY_CHUNK = 512
NEG = -1e30
LOG2E = math.log2(math.e)
VMEM_LIMIT = 56 * 1024 * 1024


def _params(*sem):
    return pltpu.CompilerParams(dimension_semantics=sem, vmem_limit_bytes=VMEM_LIMIT)


def _split(x):
    hi = x.astype(BF16)
    lo = (x - hi.astype(F32)).astype(BF16)
    return hi, lo


def _dot3(a, b):
    ah, al = _split(a)
    bh, bl = _split(b)
    return (jnp.dot(ah, bh, preferred_element_type=F32)
            + (jnp.dot(ah, bl, preferred_element_type=F32)
               + jnp.dot(al, bh, preferred_element_type=F32)))


def _nt_dot(a, b):
    return lax.dot_general(a, b, (((1,), (1,)), ((), ())), preferred_element_type=F32)


def _rms(x):
    return x * lax.rsqrt(jnp.mean(x * x, axis=-1, keepdims=True) + NORM_EPS)


HALF = D_MODEL // 2
HIGH_BITS = 0xFFFF0000


def _pack_rows(x):
    lo = lax.bitcast_convert_type(x[:, :HALF], U32) >> 16
    hi = lax.bitcast_convert_type(x[:, HALF:], U32) & jnp.uint32(HIGH_BITS)
    return lo | hi


def _unpack_rows(w):
    lo = lax.bitcast_convert_type(w << 16, F32).astype(BF16)
    hi = lax.bitcast_convert_type(w & jnp.uint32(HIGH_BITS), F32).astype(BF16)
    return lo, hi


def _mod_kernel(c_ref, w_ref, b_ref, o_ref):
    c = c_ref[...]
    a = c * (1.0 / (1.0 + jnp.exp(-c)))
    o_ref[0] = _dot3(a, w_ref[0]) + b_ref[0]


def _mod_vectors(cmat, w_mod, b_mod):
    depth, d, e = w_mod.shape
    rows = cmat.shape[0]
    tn = 512
    return pl.pallas_call(
        _mod_kernel,
        grid=(depth, e // tn),
        in_specs=[pl.BlockSpec((rows, d), lambda l, j: (0, 0)),
                  pl.BlockSpec((1, d, tn), lambda l, j: (l, 0, j)),
                  pl.BlockSpec((1, 1, tn), lambda l, j: (l, 0, j))],
        out_specs=pl.BlockSpec((1, rows, tn), lambda l, j: (l, 0, j)),
        out_shape=jax.ShapeDtypeStruct((depth, rows, e), F32),
        compiler_params=_params("arbitrary", "arbitrary"),
    )(cmat, w_mod, b_mod.reshape(depth, 1, e))


def _rope(z, col_ref, row_ref, nf, r0):
    tm, w = z.shape
    outs = []
    for g in range(tm // GRID_W):
        zs = z[GRID_W * g:GRID_W * (g + 1), :]
        c = col_ref[0] + row_ref[0, pl.ds(r0 + g, 1), :]
        sm = col_ref[1] + row_ref[1, pl.ds(r0 + g, 1), :]
        sp = col_ref[2] + row_ref[2, pl.ds(r0 + g, 1), :]
        outs.append(zs * c + pltpu.roll(zs, w - nf, 1) * sm + pltpu.roll(zs, nf, 1) * sp)
    return jnp.concatenate(outs, axis=0)


def _inproj_kernel(*refs, rope, tm):
    x_ref, sh_ref, sc_ref, n1_ref, w_ref = refs[:5]
    if rope:
        ta_col, ta_row, tc_col, tc_row = refs[5:9]
        qa_o, ka_o, va_o, u_o, qc_o, kr_o, vr_o = refs[9:]
    else:
        qa_o, ka_o, va_o, u_o, qc_o, kr_o, vr_o, ka32_o, va32_o, kc32_o, vc32_o = refs[5:]
    x = x_ref[0]
    h = (_rms(x) * n1_ref[...]) * (1.0 + sc_ref[0]) + sh_ref[0]
    hb = h.astype(BF16)

    def seg(a, b):
        return jnp.dot(hb, w_ref[:, a:b], preferred_element_type=F32)

    qa, ka, va, u = seg(0, 256), seg(256, 512), seg(512, 768), seg(768, 1024)
    qc, kr, vr = seg(1024, 1536), seg(1536, 2048), seg(2048, 2560)
    if rope:
        r0 = pl.program_id(1) * (tm // GRID_W)
        qa = _rope(qa, ta_col, ta_row, A_HD // 4, r0)
        ka = _rope(ka, ta_col, ta_row, A_HD // 4, r0)
        qc = _rope(qc, tc_col, tc_row, C_HD // 4, r0)
        kr = _rope(kr, tc_col, tc_row, C_HD // 4, r0)
    else:
        ka32_o[0] = ka
        va32_o[0] = va
        kc32_o[0] = seg(2560, 2688)
        vc32_o[0] = seg(2688, 2816)
    qa_o[0] = (qa * (A_HD ** -0.5 * LOG2E)).astype(BF16)
    ka_o[0] = ka.astype(BF16)
    va_o[0] = va.astype(BF16)
    u_o[0] = u
    qc_o[0] = (qc * (C_HD ** -0.5 * LOG2E)).astype(BF16)
    kr_o[0] = kr.astype(BF16)
    vr_o[0] = vr.astype(BF16)


def _inproj(x, sh, sc, n1, w, tables):
    b, l, d = x.shape
    tm = min(ROW_TILE, l)
    rope = tables is not None
    per_batch = sh.shape[0] > 1
    bidx = (lambda i, t: (i, 0, 0)) if per_batch else (lambda i, t: (0, 0, 0))
    tok = lambda width: pl.BlockSpec((1, tm, width), lambda i, t: (i, t, 0))
    in_specs = [tok(d),
                pl.BlockSpec((1, 1, d), bidx), pl.BlockSpec((1, 1, d), bidx),
                pl.BlockSpec((1, d), lambda i, t: (0, 0)),
                pl.BlockSpec(w.shape, lambda i, t: (0, 0))]
    args = [x, sh, sc, n1, w]
    widths = [(256, BF16), (256, BF16), (256, BF16), (256, F32), (512, BF16), (512, BF16), (512, BF16)]
    if rope:
        for tab in tables:
            in_specs.append(pl.BlockSpec(tab.shape, lambda i, t: (0, 0, 0)))
            args.append(tab)
    else:
        widths += [(256, F32), (256, F32), (128, F32), (128, F32)]
    return pl.pallas_call(
        functools.partial(_inproj_kernel, rope=rope, tm=tm),
        grid=(b, l // tm),
        in_specs=in_specs,
        out_specs=[tok(wd) for wd, _ in widths],
        out_shape=[jax.ShapeDtypeStruct((b, l, wd), dt) for wd, dt in widths],
        compiler_params=_params("arbitrary", "arbitrary"),
    )(*args)


def _rope_tables(n_lat, head_dim, width):
    rows = n_lat // GRID_W
    nf = head_dim // 4
    inv = ROPE_BASE ** (-jnp.arange(nf, dtype=F32) / nf)
    lane = np.arange(width) % head_dim
    half = lane // (2 * nf)
    pair = (lane // nf) % 2
    f = lane % nf

    def part(pos, which):
        ang = pos[:, None] * inv[f][None, :]
        on = jnp.asarray(half == which, F32)[None, :]
        c = jnp.cos(ang) * on
        s = jnp.sin(ang) * on
        sm = -s * jnp.asarray(pair == 0, F32)[None, :]
        sp = s * jnp.asarray(pair == 1, F32)[None, :]
        return jnp.stack([c, sm, sp])

    return part(jnp.arange(GRID_W, dtype=F32), 1), part(jnp.arange(rows, dtype=F32), 0)


POOL_PAD = 8
POOL_CHUNK = 256


def _pool_kernel(u_ref, w_ref, ps_ref, o_ref, pad_ref, *, l):
    zeros = jnp.zeros((POOL_PAD, B_WIDTH), F32)
    pad_ref[0:POOL_PAD, :] = zeros
    pad_ref[POOL_PAD + l:2 * POOL_PAD + l, :] = zeros
    pad_ref[POOL_PAD:POOL_PAD + l, :] = u_ref[0]
    ch = min(POOL_CHUNK, l)
    lane = lax.broadcasted_iota(I32, (ch, B_WIDTH), 1)
    grp = lane >> 6
    half = jnp.where(grp == 0, 1, jnp.where(grp == 1, 2, jnp.where(grp == 2, 4, 8)))
    row = lax.broadcasted_iota(I32, (ch, B_WIDTH), 0)
    for c in range(0, l, ch):
        ld = lambda k: pad_ref[c + POOL_PAD + k:c + POOL_PAD + k + ch, :]
        cur = ld(0)
        s2 = ld(-1) + cur
        s4 = s2 + (ld(-2) + ld(1))
        s8 = s4 + ((ld(-4) + ld(-3)) + (ld(2) + ld(3)))
        s16 = s8 + (((ld(-8) + ld(-7)) + (ld(-6) + ld(-5))) + ((ld(4) + ld(5)) + (ld(6) + ld(7))))
        win = jnp.where(grp == 0, s2, jnp.where(grp == 1, s4, jnp.where(grp == 2, s8, s16)))
        t = row + c
        cnt = (jnp.minimum(t + half, l) - jnp.maximum(t - half, 0)).astype(F32)
        r = (win / cnt - cur).astype(BF16)
        y = jnp.dot(r, w_ref[...], preferred_element_type=F32) * ps_ref[...]
        o_ref[0, c:c + ch, :] = y.astype(BF16)


def _pool(u, wbd, ps):
    b, l, w = u.shape
    return pl.pallas_call(
        functools.partial(_pool_kernel, l=l),
        grid=(b,),
        in_specs=[pl.BlockSpec((1, l, w), lambda i: (i, 0, 0)),
                  pl.BlockSpec((w, w), lambda i: (0, 0)),
                  pl.BlockSpec((1, w), lambda i: (0, 0))],
        out_specs=pl.BlockSpec((1, l, w), lambda i: (i, 0, 0)),
        out_shape=jax.ShapeDtypeStruct((b, l, w), BF16),
        scratch_shapes=[pltpu.VMEM((l + 2 * POOL_PAD, w), F32)],
        compiler_params=_params("arbitrary"),
    )(u, wbd, ps)


def _diff_attn_kernel(*refs, n_src, tq, lam_init):
    q_ref = refs[0]
    srcs = [(refs[1 + 2 * i], refs[2 + 2 * i]) for i in range(n_src)]
    lamp_ref, g_ref, o_ref, s_scr, p_scr = refs[1 + 2 * n_src:]
    chunks = []
    col = 0
    for k_ref, _ in srcs:
        keys = k_ref.shape[1]
        tk = min(keys, KEY_CHUNK)
        for c in range(0, keys, tk):
            chunks.append((k_ref, c, tk, col))
            col += tk
    q32 = q_ref[0].astype(F32)
    lp = lamp_ref[...]
    lam = (jnp.exp(jnp.sum(lp[0:1] * lp[1:2], axis=-1, keepdims=True))
           - jnp.exp(jnp.sum(lp[2:3] * lp[3:4], axis=-1, keepdims=True)) + lam_init)
    lane_row = lax.broadcasted_iota(I32, (1, A_WIDTH), 1)
    lane = lax.broadcasted_iota(I32, (tq, A_WIDTH), 1)

    def scores(j):
        qm = (q32 * ((lane_row >> 5) == j).astype(F32)).astype(BF16)
        part = jnp.full((tq, LANES), NEG, F32)
        for k_ref, c, tk, col in chunks:
            s = _nt_dot(qm, k_ref[0, c:c + tk, :])
            s_scr[j % 2, :, col:col + tk] = s
            for i in range(0, tk, LANES):
                part = jnp.maximum(part, s[:, i:i + LANES])
        return jnp.max(part, axis=-1, keepdims=True)

    def weights(j, row_max):
        part = jnp.zeros((tq, LANES), F32)
        for _, _, tk, col in chunks:
            p = jnp.exp2(s_scr[j % 2, :, col:col + tk] - row_max)
            for i in range(0, tk, LANES):
                part = part + p[:, i:i + LANES]
            p_scr[j % 2, :, col:col + tk] = p.astype(BF16)
        return jnp.sum(part, axis=-1, keepdims=True)

    def values(j, den):
        acc, col = None, 0
        for _, v_ref in srcs:
            keys = v_ref.shape[1]
            part = jnp.dot(p_scr[j % 2, :, col:col + keys], v_ref[0], preferred_element_type=F32)
            acc = part if acc is None else acc + part
            col += keys
        return acc / den

    out = jnp.zeros((tq, A_WIDTH), F32)
    row_max = scores(0)
    for j in range(2 * A_HEADS):
        den = weights(j, row_max)
        if j + 1 < 2 * A_HEADS:
            row_max = scores(j + 1)
        if j % 2 == 0:
            first = values(j, den)
        else:
            out = out + jnp.where((lane >> 6) == j // 2, first - lam * values(j, den), 0.0)
    sq = out * out
    rs = jnp.zeros((tq, A_WIDTH), F32)
    for h in range(A_HEADS):
        msk = (lane >> 6) == h
        ms = jnp.sum(jnp.where(msk, sq, 0.0), axis=-1, keepdims=True) * (1.0 / (2 * A_HD))
        rs = rs + jnp.where(msk, lax.rsqrt(ms + NORM_EPS), 0.0)
    o_ref[0] = (((out * rs) * g_ref[...]) * (1.0 - lam_init)).astype(BF16)


def _diff_attn(q, srcs, lam_p, g_tiled, lam_init):
    b, l, w = q.shape
    s_len = sum(k.shape[1] for k, _ in srcs)
    tq = min(ROW_TILE, l)
    in_specs = [pl.BlockSpec((1, tq, w), lambda i, t: (i, t, 0))]
    args = [q]
    for k, v in srcs:
        for a in (k, v):
            in_specs.append(pl.BlockSpec((1,) + a.shape[1:], lambda i, t: (i, 0, 0)))
            args.append(a)
    in_specs += [pl.BlockSpec(lam_p.shape, lambda i, t: (0, 0)),
                 pl.BlockSpec(g_tiled.shape, lambda i, t: (0, 0))]
    return pl.pallas_call(
        functools.partial(_diff_attn_kernel, n_src=len(srcs), tq=tq, lam_init=lam_init),
        grid=(b, l // tq),
        in_specs=in_specs,
        out_specs=pl.BlockSpec((1, tq, w), lambda i, t: (i, t, 0)),
        out_shape=jax.ShapeDtypeStruct((b, l, w), BF16),
        scratch_shapes=[pltpu.VMEM((2, tq, s_len), F32), pltpu.VMEM((2, tq, s_len), BF16)],
        compiler_params=_params("arbitrary", "arbitrary"),
    )(*args, lam_p, g_tiled)


GQA_SLAB = C_GROUP * C_HD


def _gqa_kernel(*refs, windowed, tq, l):
    if windowed:
        sink_ref, q_ref, k_ref, v_ref, kc_ref, vc_ref, o_ref = refs
    else:
        sink_ref, q_ref, k_ref, v_ref, o_ref = refs
    i = pl.program_id(1)
    rows = C_GROUP * tq
    shift = int(math.log2(tq))
    q32 = q_ref[0].astype(F32)
    lane_row = lax.broadcasted_iota(I32, (1, GQA_SLAB), 1)
    lane = lax.broadcasted_iota(I32, (tq, GQA_SLAB), 1)
    rid = lax.broadcasted_iota(I32, (rows, 1), 0)
    if windowed:
        ws = pl.multiple_of(jnp.clip((i - 1) * tq, 0, l - 3 * tq), tq)
        qpos = i * tq + (rid & (tq - 1))
        kpos = ws + lax.broadcasted_iota(I32, (1, 3 * tq), 1)
        band = jnp.where(jnp.abs(kpos - qpos) <= WINDOW, 0.0, NEG)
    for g in range(C_KV):
        sl = slice(GQA_SLAB * g, GQA_SLAB * (g + 1))
        qg = q32[:, sl]
        qs = jnp.concatenate(
            [(qg * ((lane_row >> 6) == hh).astype(F32)).astype(BF16) for hh in range(C_GROUP)], axis=0)
        sk = jnp.zeros((rows, 1), F32)
        for hh in range(C_GROUP):
            sk = jnp.where((rid >> shift) == hh, sink_ref[C_GROUP * g + hh] * LOG2E, sk)
        if windowed:
            s_loc = _nt_dot(qs, k_ref[0, pl.ds(ws, 3 * tq), sl]) + band
            s_ctx = _nt_dot(qs, kc_ref[0, :, sl])
            m = jnp.maximum(sk, jnp.maximum(jnp.max(s_loc, axis=-1, keepdims=True),
                                            jnp.max(s_ctx, axis=-1, keepdims=True)))
            p_loc = jnp.exp2(s_loc - m)
            p_ctx = jnp.exp2(s_ctx - m)
            den = (jnp.exp2(sk - m) + jnp.sum(p_loc, axis=-1, keepdims=True)
                   + jnp.sum(p_ctx, axis=-1, keepdims=True))
            pv = (jnp.dot(p_loc.astype(BF16), v_ref[0, pl.ds(ws, 3 * tq), sl], preferred_element_type=F32)
                  + jnp.dot(p_ctx.astype(BF16), vc_ref[0, :, sl], preferred_element_type=F32))
        else:
            s = _nt_dot(qs, k_ref[0, :, sl])
            m = jnp.maximum(sk, jnp.max(s, axis=-1, keepdims=True))
            p = jnp.exp2(s - m)
            den = jnp.exp2(sk - m) + jnp.sum(p, axis=-1, keepdims=True)
            pv = jnp.dot(p.astype(BF16), v_ref[0, :, sl], preferred_element_type=F32)
        o = pv / den
        og = jnp.zeros((tq, GQA_SLAB), F32)
        for hh in range(C_GROUP):
            og = og + jnp.where((lane >> 6) == hh, o[hh * tq:(hh + 1) * tq, :], 0.0)
        o_ref[0, :, sl] = og.astype(BF16)


def _gqa(sink, q, k, v, cache=None):
    b, l, w = q.shape
    windowed = cache is not None
    tq = QBLOCK
    whole = lambda a: pl.BlockSpec((1,) + a.shape[1:], lambda i, t: (i, 0, 0))
    in_specs = [pl.BlockSpec(memory_space=pltpu.SMEM),
                pl.BlockSpec((1, tq, w), lambda i, t: (i, t, 0)), whole(k), whole(v)]
    args = [sink, q, k, v]
    if windowed:
        in_specs += [whole(cache[0]), whole(cache[1])]
        args += list(cache)
    return pl.pallas_call(
        functools.partial(_gqa_kernel, windowed=windowed, tq=tq, l=l),
        grid=(b, l // tq),
        in_specs=in_specs,
        out_specs=pl.BlockSpec((1, tq, w), lambda i, t: (i, t, 0)),
        out_shape=jax.ShapeDtypeStruct((b, l, w), BF16),
        compiler_params=_params("arbitrary", "arbitrary"),
    )(*args)


def _finish_kernel(x_ref, oa_ref, ob_ref, oc_ref, wo_ref, g1_ref, sh_ref, sc_ref, n2_ref, wr_ref, br_ref,
                   x1_o, h2_o, idx_o, gate_o, rank_o, cnt_o, *, tm):
    y = (jnp.dot(oa_ref[0], wo_ref[0:256, :], preferred_element_type=F32)
         + jnp.dot(ob_ref[0], wo_ref[256:512, :], preferred_element_type=F32)
         + jnp.dot(oc_ref[0], wo_ref[512:1024, :], preferred_element_type=F32))
    x1 = x_ref[0] + g1_ref[0] * y
    x1_o[0] = x1
    h2 = (_rms(x1) * n2_ref[...]) * (1.0 + sc_ref[0]) + sh_ref[0]
    h2_o[0] = h2
    lane = lax.broadcasted_iota(I32, (tm, LANES), 1)
    lanef = lane.astype(F32)
    logits = jnp.where(lane < N_EXPERTS, _dot3(h2, wr_ref[...]) + br_ref[...], NEG)
    work = logits
    tops, sels = [], []
    for k in range(TOP_K):
        mk = jnp.max(work, axis=-1, keepdims=True)
        ik = jnp.min(jnp.where(work == mk, lanef, float(LANES)), axis=-1, keepdims=True)
        sel = lanef == ik
        work = jnp.where(sel, 2.0 * NEG, work)
        tops.append((mk, ik))
        sels.append(sel)
    es = [jnp.exp(mk - tops[0][0]) for mk, _ in tops]
    den = (es[0] + es[1]) + (es[2] + es[3])
    multi = jnp.zeros((tm, LANES), F32)
    for sel in sels:
        multi = multi + sel.astype(F32)
    r_i = lax.broadcasted_iota(I32, (tm, tm), 0)
    c_i = lax.broadcasted_iota(I32, (tm, tm), 1)
    before = (c_i < r_i).astype(BF16)
    prior = jnp.dot(before, multi.astype(BF16), preferred_element_type=F32)
    idx = jnp.zeros((tm, LANES), F32)
    gate = jnp.zeros((tm, LANES), F32)
    rank = jnp.zeros((tm, LANES), F32)
    for k in range(TOP_K):
        slot = lane == k
        idx = jnp.where(slot, tops[k][1], idx)
        gate = jnp.where(slot, es[k] / den, gate)
        rk = jnp.sum(jnp.where(sels[k], prior, 0.0), axis=-1, keepdims=True)
        rank = jnp.where(slot, rk, rank)
    idx_o[0] = idx.astype(I32)
    gate_o[0] = gate
    rank_o[0] = rank.astype(I32)
    cnt_o[0] = jnp.sum(multi, axis=0, keepdims=True)


def _finish(x, oa, ob, oc, wo, g1, sh2, sc2, n2, wr, br):
    b, l, d = x.shape
    tm = min(ROW_TILE, l)
    nt = l // tm
    per_batch = g1.shape[0] > 1
    bidx = (lambda i, t: (i, 0, 0)) if per_batch else (lambda i, t: (0, 0, 0))
    tok = lambda width: pl.BlockSpec((1, tm, width), lambda i, t: (i, t, 0))
    const2 = lambda a: pl.BlockSpec(a.shape, lambda i, t: (0, 0))
    vec = pl.BlockSpec((1, 1, d), bidx)
    return pl.pallas_call(
        functools.partial(_finish_kernel, tm=tm),
        grid=(b, l // tm),
        in_specs=[tok(d), tok(256), tok(256), tok(512), const2(wo), vec, vec, vec,
                  const2(n2), const2(wr), const2(br)],
        out_specs=[tok(d), tok(d), tok(LANES), tok(LANES), tok(LANES),
                   pl.BlockSpec((1, 1, LANES), lambda i, t: (i * nt + t, 0, 0))],
        out_shape=[jax.ShapeDtypeStruct((b, l, d), F32), jax.ShapeDtypeStruct((b, l, d), F32),
                   jax.ShapeDtypeStruct((b, l, LANES), I32), jax.ShapeDtypeStruct((b, l, LANES), F32),
                   jax.ShapeDtypeStruct((b, l, LANES), I32), jax.ShapeDtypeStruct((b * nt, 1, LANES), F32)],
        compiler_params=_params("arbitrary", "arbitrary"),
    )(x, oa, ob, oc, wo, g1, sh2, sc2, n2, wr, br)


CHUNK = 8
SORT_ROWS = ROW_TILE * TOP_K + N_EXPERTS * CHUNK
MAX_CHUNKS = SORT_ROWS // CHUNK


def _slot_values(idx, rank, seg, k):
    lane = lax.broadcasted_iota(I32, idx.shape, 1)
    return jnp.where(lane == idx[:, k:k + 1], seg + rank[:, k:k + 1].astype(F32), 0.0)


def _dispatch_kernel(nch_ref, dst_ref, h_ref, idx_ref, rank_ref, seg_ref, xs_in_ref, xs_ref, srt, sems, *, tm, t0):
    del xs_in_ref
    tile = t0 + pl.program_id(0)
    idx, rank, seg = idx_ref[...], rank_ref[...], seg_ref[0]
    row = lax.broadcasted_iota(I32, (SORT_ROWS, tm), 0).astype(F32)
    ones = jnp.ones((CHUNK, LANES), BF16)
    perm = jnp.zeros((SORT_ROWS, tm), F32)
    for k in range(TOP_K):
        hi, lo = _split(_slot_values(idx, rank, seg, k))
        pos = (_nt_dot(ones, hi) + _nt_dot(ones, lo))[0:1, :]
        perm = perm + (row == pos).astype(F32)
    step = pl.program_id(0)
    slot = step % 2
    srt[slot] = _pack_rows(jnp.dot(perm.astype(BF16), h_ref[...].astype(BF16), preferred_element_type=F32))

    def chunk_copy(tl, sl, j):
        src = pl.multiple_of(j * CHUNK, CHUNK)
        dst = pl.multiple_of(dst_ref[tl * MAX_CHUNKS + j], CHUNK)
        return pltpu.make_async_copy(srt.at[sl, pl.ds(src, CHUNK)], xs_ref.at[pl.ds(dst, CHUNK)], sems.at[sl])

    def issue(j, carry):
        chunk_copy(tile, slot, j).start()
        return carry

    lax.fori_loop(0, nch_ref[tile], issue, 0)

    @pl.when(step > 0)
    def _():
        def drain(j, carry):
            chunk_copy(tile - 1, 1 - slot, j).wait()
            return carry
        lax.fori_loop(0, nch_ref[tile - 1], drain, 0)

    @pl.when(step == pl.num_programs(0) - 1)
    def _():
        def drain(j, carry):
            chunk_copy(tile, slot, j).wait()
            return carry
        lax.fori_loop(0, nch_ref[tile], drain, 0)


def _dispatch(nch, dst, h2, idx, rank, seg, xs, t0):
    n, d = h2.shape
    tm = ROW_TILE
    tok = lambda width: pl.BlockSpec((tm, width), lambda i, a, b: (i, 0))
    return pl.pallas_call(
        functools.partial(_dispatch_kernel, tm=tm, t0=t0),
        grid_spec=pltpu.PrefetchScalarGridSpec(
            num_scalar_prefetch=2,
            grid=(n // tm,),
            in_specs=[tok(d), tok(LANES), tok(LANES),
                      pl.BlockSpec((1, 1, LANES), lambda i, a, b: (t0 + i, 0, 0)),
                      pl.BlockSpec(memory_space=pl.ANY)],
            out_specs=pl.BlockSpec(memory_space=pl.ANY),
            scratch_shapes=[pltpu.VMEM((2, SORT_ROWS, HALF), U32), pltpu.SemaphoreType.DMA((2,))]),
        out_shape=jax.ShapeDtypeStruct(xs.shape, xs.dtype),
        input_output_aliases={6: 0},
        compiler_params=_params("arbitrary"),
    )(nch, dst, h2, idx, rank, seg, xs)


def _expert_kernel(start_ref, nblk_ref, xs_ref, w1_ref, b1_ref, w2_ref, b2_ref, o_ref,
                   w1b, w2b, xbuf, obuf, sin, sout):
    e = pl.program_id(0)
    n = nblk_ref[e]

    def rows(expert, j):
        return pl.ds(pl.multiple_of(start_ref[expert] + j * MOE_ROWS, MOE_ROWS), MOE_ROWS)

    def in_copy(expert, j, slot):
        return pltpu.make_async_copy(xs_ref.at[rows(expert, j)], xbuf.at[slot], sin.at[slot])

    def out_copy(j, slot):
        return pltpu.make_async_copy(obuf.at[slot], o_ref.at[rows(e, j)], sout.at[slot])

    @pl.when((e == 0) & (n > 0))
    def _():
        in_copy(e, 0, 0).start()

    @pl.when(n > 0)
    def _():
        w1b[...] = w1_ref[0].astype(BF16)
        w2b[...] = w2_ref[0].astype(BF16)

        def block(j, carry):
            slot = j % 2

            @pl.when(j + 1 < n)
            def _():
                in_copy(e, j + 1, 1 - slot).start()

            in_copy(e, j, slot).wait()

            @pl.when(j >= 2)
            def _():
                out_copy(j - 2, slot).wait()

            x_lo, x_hi = _unpack_rows(xbuf[slot])
            gu = (jnp.dot(x_lo, w1b[:HALF, :], preferred_element_type=F32)
                  + jnp.dot(x_hi, w1b[HALF:, :], preferred_element_type=F32)) + b1_ref[0]
            gt = jnp.minimum(gu[:, :D_FF], SWIGLU_LIMIT)
            up = jnp.clip(gu[:, D_FF:], -SWIGLU_LIMIT, SWIGLU_LIMIT)
            hid = (up + 1.0) * gt * (1.0 / (1.0 + jnp.exp(-SWIGLU_ALPHA * gt)))
            out = jnp.dot(hid.astype(BF16), w2b[...], preferred_element_type=F32) + b2_ref[0]
            obuf[slot] = _pack_rows(out.astype(BF16).astype(F32))
            out_copy(j, slot).start()
            return carry

        lax.fori_loop(0, n, block, 0)

        @pl.when(n >= 2)
        def _():
            out_copy(n - 2, n % 2).wait()

        out_copy(n - 1, (n - 1) % 2).wait()

    nxt = jnp.minimum(e + 1, pl.num_programs(0) - 1)

    @pl.when((e + 1 < pl.num_programs(0)) & (nblk_ref[nxt] > 0))
    def _():
        in_copy(nxt, 0, 0).start()


def _experts(row_start, n_blk, xs, w1, b1, w2, b2, layer):
    n_rows, half = xs.shape
    d = 2 * half
    f2 = w1.shape[-1]
    n_exp = w1.shape[1]
    ne = w1.shape[0] * n_exp
    w1 = w1.reshape(ne, d, f2)
    w2 = w2.reshape(ne, D_FF, d)
    pick = lambda e, st, nb: (layer * n_exp + e, 0, 0)
    return pl.pallas_call(
        _expert_kernel,
        grid_spec=pltpu.PrefetchScalarGridSpec(
            num_scalar_prefetch=2,
            grid=(n_exp,),
            in_specs=[pl.BlockSpec(memory_space=pl.ANY),
                      pl.BlockSpec((1, d, f2), pick), pl.BlockSpec((1, 1, f2), pick),
                      pl.BlockSpec((1, D_FF, d), pick), pl.BlockSpec((1, 1, d), pick)],
            out_specs=pl.BlockSpec(memory_space=pl.ANY),
            scratch_shapes=[pltpu.VMEM((d, f2), BF16), pltpu.VMEM((D_FF, d), BF16),
                            pltpu.VMEM((2, MOE_ROWS, half), U32), pltpu.VMEM((2, MOE_ROWS, half), U32),
                            pltpu.SemaphoreType.DMA((2,)), pltpu.SemaphoreType.DMA((2,))]),
        out_shape=jax.ShapeDtypeStruct((n_rows, half), U32),
        input_output_aliases={2: 0},
        compiler_params=_params("arbitrary"),
    )(row_start, n_blk, xs, w1, b1.reshape(ne, 1, f2), w2, b2.reshape(ne, 1, d))


def _combine_kernel(nch_ref, dst_ref, outs_ref, idx_ref, rank_ref, gate_ref, seg_ref, x1_ref, g2_ref, fg_ref,
                    o_ref, buf, sems, *, tm, nt, t0, final):
    step = pl.program_id(0) * nt + pl.program_id(1)
    tile = t0 + step
    slot = step % 2

    def chunk_copy(tl, sl, j):
        src = pl.multiple_of(dst_ref[tl * MAX_CHUNKS + j], CHUNK)
        dst = pl.multiple_of(j * CHUNK, CHUNK)
        return pltpu.make_async_copy(outs_ref.at[pl.ds(src, CHUNK)], buf.at[sl, pl.ds(dst, CHUNK)], sems.at[sl])

    def fetch(tl, sl):
        def issue(j, carry):
            chunk_copy(tl, sl, j).start()
            return carry
        lax.fori_loop(0, nch_ref[tl], issue, 0)

    @pl.when(step == 0)
    def _():
        buf[...] = jnp.zeros_like(buf)
        fetch(tile, slot)

    @pl.when(step + 1 < pl.num_programs(0) * nt)
    def _():
        fetch(tile + 1, 1 - slot)

    def drain(j, carry):
        chunk_copy(tile, slot, j).wait()
        return carry

    n = nch_ref[tile]
    idx, rank, gate, seg = idx_ref[0], rank_ref[0], gate_ref[0], seg_ref[0]
    col = lax.broadcasted_iota(I32, (tm, SORT_ROWS), 1).astype(F32)
    weights = jnp.zeros((tm, SORT_ROWS), F32)
    for k in range(TOP_K):
        pos = jnp.sum(_slot_values(idx, rank, seg, k), axis=-1, keepdims=True)
        weights = jnp.where(col == pos, gate[:, k:k + 1], weights)
    wh, wl = _split(weights)
    lax.fori_loop(0, n, drain, 0)
    y = jnp.concatenate(
        [jnp.dot(wh, rows, preferred_element_type=F32) + jnp.dot(wl, rows, preferred_element_type=F32)
         for rows in _unpack_rows(buf[slot])], axis=1)
    x2 = x1_ref[0] + g2_ref[0] * y
    if final:
        x2 = _rms(x2) * fg_ref[...]
    o_ref[0] = x2


def _combine(nch, dst, outs, idx, rank, gate, seg, x1, g2, fg, t0, final):
    b, l, d = x1.shape
    tm = ROW_TILE
    nt = l // tm
    per_batch = g2.shape[0] > 1
    bidx = (lambda i, t, a, c: (i, 0, 0)) if per_batch else (lambda i, t, a, c: (0, 0, 0))
    tok = lambda width: pl.BlockSpec((1, tm, width), lambda i, t, a, c: (i, t, 0))
    return pl.pallas_call(
        functools.partial(_combine_kernel, tm=tm, nt=nt, t0=t0, final=final),
        grid_spec=pltpu.PrefetchScalarGridSpec(
            num_scalar_prefetch=2,
            grid=(b, nt),
            in_specs=[pl.BlockSpec(memory_space=pl.ANY), tok(LANES), tok(LANES), tok(LANES),
                      pl.BlockSpec((1, 1, LANES), lambda i, t, a, c: (t0 + i * nt + t, 0, 0)),
                      tok(d), pl.BlockSpec((1, 1, d), bidx),
                      pl.BlockSpec((1, d), lambda i, t, a, c: (0, 0))],
            out_specs=tok(d),
            scratch_shapes=[pltpu.VMEM((2, SORT_ROWS, HALF), U32), pltpu.SemaphoreType.DMA((2,))]),
        out_shape=jax.ShapeDtypeStruct((b, l, d), F32),
        compiler_params=_params("arbitrary", "arbitrary"),
    )(nch, dst, outs, idx, rank, gate, seg, x1, g2, fg)


def _layer_weights(w_in_l):
    cuts = np.cumsum([A_WIDTH, A_WIDTH, A_WIDTH, B_WIDTH, C_WIDTH, C_KV * C_HD]).tolist()
    front = w_in_l[:, :cuts[4]]
    kc = w_in_l[:, cuts[4]:cuts[5]]
    vc = w_in_l[:, cuts[5]:]

    def rep(w):
        return jnp.concatenate([w[:, C_HD * (j // C_GROUP):C_HD * (j // C_GROUP + 1)] for j in range(C_HEADS)], axis=1)

    lat = jnp.concatenate([front, rep(kc), rep(vc)], axis=1).astype(BF16)
    ctx = jnp.concatenate([front, rep(kc), rep(vc), kc, vc], axis=1).astype(BF16)
    return ctx, lat


def _rep_heads(a):
    return jnp.repeat(a, C_GROUP, axis=2).reshape(a.shape[0], a.shape[1], C_HEADS * C_HD)


def kernel(x_prompt, x_sample, c, cache_diff_k, cache_diff_v, cache_win_k, cache_win_v, c_ctx, norm1_g, norm2_g, w_mod, b_mod, w_in, diff_lambda, diff_subln_g, w_pool, pool_scale, sink, w_out, w_router, b_router, w1, b1, w2, b2, final_g):
    depth = w_in.shape[0]
    bc, lc, d = x_prompt.shape
    bl, ll, _ = x_sample.shape
    n_ctx, n_lat = bc * lc, bl * ll
    n_tok = n_ctx + n_lat
    n_tiles = n_tok // ROW_TILE
    n_blocks = -(-(n_tok * TOP_K + n_tiles * N_EXPERTS * (CHUNK - 1)) // MOE_ROWS) + N_EXPERTS
    n_rows = n_blocks * MOE_ROWS

    mod_rows = -(-(1 + bl) // 8) * 8
    cmat = jnp.zeros((mod_rows, d), F32).at[0].set(c_ctx).at[1:1 + bl].set(c)
    mod = _mod_vectors(cmat, w_mod, b_mod)
    tabs_a = _rope_tables(ll, A_HD, A_WIDTH)
    tabs_c = _rope_tables(ll, C_HD, C_WIDTH)
    tables = (tabs_a[0], tabs_a[1], tabs_c[0], tabs_c[1])
    fg = final_g.reshape(1, d)

    xp, xs_lat = x_prompt, x_sample
    new_cache = [[], [], [], []]
    for i in range(depth):
        lam_init = 0.8 - 0.6 * math.exp(-0.3 * i)
        mv = lambda rows, j: mod[i, rows, j * d:(j + 1) * d].reshape(-1, 1, d)
        ctx_rows, lat_rows = slice(0, 1), slice(1, 1 + bl)
        w_ctx, w_lat = _layer_weights(w_in[i])
        n1 = norm1_g[i].reshape(1, d)
        n2 = norm2_g[i].reshape(1, d)
        wbd = jax.scipy.linalg.block_diag(*[w_pool[i, g] for g in range(B_GROUPS)]).astype(BF16)
        ps = pool_scale[i].reshape(1, B_WIDTH)
        g_tiled = jnp.tile(diff_subln_g[i], A_HEADS).reshape(1, A_WIDTH)
        wo = w_out[i].astype(BF16)
        wr = jnp.zeros((d, LANES), F32).at[:, :N_EXPERTS].set(w_router[i])
        br = jnp.zeros((1, LANES), F32).at[0, :N_EXPERTS].set(b_router[i])

        qa, ka, va, u, qc, kr, vr, ka32, va32, kc32, vc32 = _inproj(
            xp, mv(ctx_rows, 0), mv(ctx_rows, 1), n1, w_ctx, None)
        new_cache[0].append(ka32.reshape(bc, lc, 2 * A_HEADS, A_HD))
        new_cache[1].append(va32.reshape(bc, lc, A_HEADS, 2 * A_HD))
        new_cache[2].append(kc32.reshape(bc, lc, C_KV, C_HD))
        new_cache[3].append(vc32.reshape(bc, lc, C_KV, C_HD))
        ob = _pool(u, wbd, ps)
        oa = _diff_attn(qa, [(ka, va)], diff_lambda[i], g_tiled, lam_init)
        oc = _gqa(sink[i], qc, kr, vr)
        x1_c, h2_c, idx_c, gate_c, rank_c, cnt_c = _finish(
            xp, oa, ob, oc, wo, mv(ctx_rows, 2), mv(ctx_rows, 3), mv(ctx_rows, 4), n2, wr, br)

        qa, ka, va, u, qc, kr, vr = _inproj(xs_lat, mv(lat_rows, 0), mv(lat_rows, 1), n1, w_lat, tables)
        ob = _pool(u, wbd, ps)
        dk = cache_diff_k[:, i].reshape(bl, -1, A_WIDTH).astype(BF16)
        dv = cache_diff_v[:, i].reshape(bl, -1, A_WIDTH).astype(BF16)
        oa = _diff_attn(qa, [(ka, va), (dk, dv)], diff_lambda[i], g_tiled, lam_init)
        wk = _rep_heads(cache_win_k[:, i]).astype(BF16)
        wv = _rep_heads(cache_win_v[:, i]).astype(BF16)
        oc = _gqa(sink[i], qc, kr, vr, (wk, wv))
        x1_l, h2_l, idx_l, gate_l, rank_l, cnt_l = _finish(
            xs_lat, oa, ob, oc, wo, mv(lat_rows, 2), mv(lat_rows, 3), mv(lat_rows, 4), n2, wr, br)

        cnt = jnp.concatenate([cnt_c, cnt_l], axis=0)[:, 0, :N_EXPERTS].astype(I32)
        c8 = (cnt + CHUNK - 1) // CHUNK * CHUNK
        seg_end = jnp.cumsum(c8, axis=1)
        seg = seg_end - c8
        padded = (jnp.sum(c8, axis=0) + MOE_ROWS - 1) // MOE_ROWS * MOE_ROWS
        pad_end = jnp.cumsum(padded)
        gbase = (pad_end - padded)[None, :] + jnp.cumsum(c8, axis=0) - c8
        nch = seg_end[:, -1] // CHUNK
        j8 = jnp.arange(MAX_CHUNKS, dtype=I32) * CHUNK
        chunk_e = jnp.minimum(jnp.sum((seg_end[:, None, :] <= j8[None, :, None]).astype(I32), axis=-1),
                              N_EXPERTS - 1)
        onehot = chunk_e[..., None] == jnp.arange(N_EXPERTS, dtype=I32)
        dst = (jnp.sum(jnp.where(onehot, (gbase - seg)[:, None, :], 0), axis=-1) + j8[None, :]).reshape(-1)
        seg_f = jnp.zeros((cnt.shape[0], 1, LANES), F32).at[:, 0, :N_EXPERTS].set(seg.astype(F32))
        tiles_c = cnt_c.shape[0]

        rows_in = jnp.zeros((n_rows, HALF), U32)
        rows_in = _dispatch(nch, dst, h2_c.reshape(n_ctx, d), idx_c.reshape(n_ctx, LANES),
                            rank_c.reshape(n_ctx, LANES), seg_f, rows_in, 0)
        rows_in = _dispatch(nch, dst, h2_l.reshape(n_lat, d), idx_l.reshape(n_lat, LANES),
                            rank_l.reshape(n_lat, LANES), seg_f, rows_in, tiles_c)
        rows_out = _experts(pad_end - padded, padded // MOE_ROWS, rows_in, w1, b1, w2, b2, i)
        final = i == depth - 1
        xp = _combine(nch, dst, rows_out, idx_c, rank_c, gate_c, seg_f, x1_c, mv(ctx_rows, 5), fg, 0, final)
        xs_lat = _combine(nch, dst, rows_out, idx_l, rank_l, gate_l, seg_f, x1_l, mv(lat_rows, 5), fg,
                          tiles_c, final)

    return (xp, xs_lat) + tuple(jnp.stack(parts, axis=1) for parts in new_cache)
```

```python
import functools
import math

import numpy as np
import jax
import jax.numpy as jnp
from jax import lax
from jax.experimental import pallas as pl
from jax.experimental.pallas import tpu as pltpu

F32 = jnp.float32
BF16 = jnp.bfloat16
I32 = jnp.int32
U32 = jnp.uint32

D_MODEL = 1024
GRID_W = 64
ROPE_BASE = 10000.0
NORM_EPS = 1e-6
A_HD = 32
A_HEADS = 4
A_WIDTH = 256
B_WIDTH = 256
B_GROUPS = 4
B_GC = 64
C_HD = 64
C_HEADS = 8
C_KV = 2
C_GROUP = 4
C_WIDTH = 512
WINDOW = 128
QBLOCK = 128
N_EXPERTS = 32
TOP_K = 4
D_FF = 1024
SWIGLU_LIMIT = 7.0
SWIGLU_ALPHA = 1.702

LANES = 128
ROW_TILE = 256
MOE_ROWS = 512
KEY_CHUNK = 512
NEG = -1e30
LOG2E = math.log2(math.e)
VMEM_LIMIT = 56 * 1024 * 1024


def _params(*sem):
    return pltpu.CompilerParams(dimension_semantics=sem, vmem_limit_bytes=VMEM_LIMIT)


def _split(x):
    hi = x.astype(BF16)
    lo = (x - hi.astype(F32)).astype(BF16)
    return hi, lo


def _dot3(a, b):
    ah, al = _split(a)
    bh, bl = _split(b)
    return (jnp.dot(ah, bh, preferred_element_type=F32)
            + (jnp.dot(ah, bl, preferred_element_type=F32)
               + jnp.dot(al, bh, preferred_element_type=F32)))


def _nt_dot(a, b):
    return lax.dot_general(a, b, (((1,), (1,)), ((), ())), preferred_element_type=F32)


def _rms(x):
    return x * lax.rsqrt(jnp.mean(x * x, axis=-1, keepdims=True) + NORM_EPS)


HALF = D_MODEL // 2
HIGH_BITS = 0xFFFF0000


def _pack_rows(x):
    lo = lax.bitcast_convert_type(x[:, :HALF], U32) >> 16
    hi = lax.bitcast_convert_type(x[:, HALF:], U32) & jnp.uint32(HIGH_BITS)
    return lo | hi


def _unpack_rows(w):
    lo = lax.bitcast_convert_type(w << 16, F32).astype(BF16)
    hi = lax.bitcast_convert_type(w & jnp.uint32(HIGH_BITS), F32).astype(BF16)
    return lo, hi


def _mod_kernel(c_ref, w_ref, b_ref, o_ref):
    c = c_ref[...]
    a = c * (1.0 / (1.0 + jnp.exp(-c)))
    o_ref[0] = _dot3(a, w_ref[0]) + b_ref[0]


def _mod_vectors(cmat, w_mod, b_mod):
    depth, d, e = w_mod.shape
    rows = cmat.shape[0]
    tn = 512
    return pl.pallas_call(
        _mod_kernel,
        grid=(depth, e // tn),
        in_specs=[pl.BlockSpec((rows, d), lambda l, j: (0, 0)),
                  pl.BlockSpec((1, d, tn), lambda l, j: (l, 0, j)),
                  pl.BlockSpec((1, 1, tn), lambda l, j: (l, 0, j))],
        out_specs=pl.BlockSpec((1, rows, tn), lambda l, j: (l, 0, j)),
        out_shape=jax.ShapeDtypeStruct((depth, rows, e), F32),
        compiler_params=_params("arbitrary", "arbitrary"),
    )(cmat, w_mod, b_mod.reshape(depth, 1, e))


def _rope(z, col_ref, row_ref, nf, r0):
    tm, w = z.shape
    outs = []
    for g in range(tm // GRID_W):
        zs = z[GRID_W * g:GRID_W * (g + 1), :]
        c = col_ref[0] + row_ref[0, pl.ds(r0 + g, 1), :]
        sm = col_ref[1] + row_ref[1, pl.ds(r0 + g, 1), :]
        sp = col_ref[2] + row_ref[2, pl.ds(r0 + g, 1), :]
        outs.append(zs * c + pltpu.roll(zs, w - nf, 1) * sm + pltpu.roll(zs, nf, 1) * sp)
    return jnp.concatenate(outs, axis=0)


def _inproj_kernel(*refs, rope, tm):
    x_ref, sh_ref, sc_ref, n1_ref, w_ref = refs[:5]
    if rope:
        ta_col, ta_row, tc_col, tc_row = refs[5:9]
        qa_o, ka_o, va_o, u_o, qc_o, kr_o, vr_o = refs[9:]
    else:
        qa_o, ka_o, va_o, u_o, qc_o, kr_o, vr_o, ka32_o, va32_o, kc32_o, vc32_o = refs[5:]
    x = x_ref[0]
    h = (_rms(x) * n1_ref[...]) * (1.0 + sc_ref[0]) + sh_ref[0]
    hb = h.astype(BF16)

    def seg(a, b):
        return jnp.dot(hb, w_ref[:, a:b], preferred_element_type=F32)

    qa, ka, va, u = seg(0, 256), seg(256, 512), seg(512, 768), seg(768, 1024)
    qc, kr, vr = seg(1024, 1536), seg(1536, 2048), seg(2048, 2560)
    if rope:
        r0 = pl.program_id(1) * (tm // GRID_W)
        qa = _rope(qa, ta_col, ta_row, A_HD // 4, r0)
        ka = _rope(ka, ta_col, ta_row, A_HD // 4, r0)
        qc = _rope(qc, tc_col, tc_row, C_HD // 4, r0)
        kr = _rope(kr, tc_col, tc_row, C_HD // 4, r0)
    else:
        ka32_o[0] = ka
        va32_o[0] = va
        kc32_o[0] = seg(2560, 2688)
        vc32_o[0] = seg(2688, 2816)
    qa_o[0] = (qa * (A_HD ** -0.5 * LOG2E)).astype(BF16)
    ka_o[0] = ka.astype(BF16)
    va_o[0] = va.astype(BF16)
    u_o[0] = u
    qc_o[0] = (qc * (C_HD ** -0.5 * LOG2E)).astype(BF16)
    kr_o[0] = kr.astype(BF16)
    vr_o[0] = vr.astype(BF16)


def _inproj(x, sh, sc, n1, w, tables):
    b, l, d = x.shape
    tm = min(ROW_TILE, l)
    rope = tables is not None
    per_batch = sh.shape[0] > 1
    bidx = (lambda i, t: (i, 0, 0)) if per_batch else (lambda i, t: (0, 0, 0))
    tok = lambda width: pl.BlockSpec((1, tm, width), lambda i, t: (i, t, 0))
    in_specs = [tok(d),
                pl.BlockSpec((1, 1, d), bidx), pl.BlockSpec((1, 1, d), bidx),
                pl.BlockSpec((1, d), lambda i, t: (0, 0)),
                pl.BlockSpec(w.shape, lambda i, t: (0, 0))]
    args = [x, sh, sc, n1, w]
    widths = [(256, BF16), (256, BF16), (256, BF16), (256, F32), (512, BF16), (512, BF16), (512, BF16)]
    if rope:
        for tab in tables:
            in_specs.append(pl.BlockSpec(tab.shape, lambda i, t: (0, 0, 0)))
            args.append(tab)
    else:
        widths += [(256, F32), (256, F32), (128, F32), (128, F32)]
    return pl.pallas_call(
        functools.partial(_inproj_kernel, rope=rope, tm=tm),
        grid=(b, l // tm),
        in_specs=in_specs,
        out_specs=[tok(wd) for wd, _ in widths],
        out_shape=[jax.ShapeDtypeStruct((b, l, wd), dt) for wd, dt in widths],
        compiler_params=_params("arbitrary", "arbitrary"),
    )(*args)


def _rope_tables(n_lat, head_dim, width):
    rows = n_lat // GRID_W
    nf = head_dim // 4
    inv = ROPE_BASE ** (-jnp.arange(nf, dtype=F32) / nf)
    lane = np.arange(width) % head_dim
    half = lane // (2 * nf)
    pair = (lane // nf) % 2
    f = lane % nf

    def part(pos, which):
        ang = pos[:, None] * inv[f][None, :]
        on = jnp.asarray(half == which, F32)[None, :]
        c = jnp.cos(ang) * on
        s = jnp.sin(ang) * on
        sm = -s * jnp.asarray(pair == 0, F32)[None, :]
        sp = s * jnp.asarray(pair == 1, F32)[None, :]
        return jnp.stack([c, sm, sp])

    return part(jnp.arange(GRID_W, dtype=F32), 1), part(jnp.arange(rows, dtype=F32), 0)


POOL_PAD = 8
POOL_CHUNK = 256


def _pool_kernel(u_ref, w_ref, ps_ref, o_ref, pad_ref, *, l):
    zeros = jnp.zeros((POOL_PAD, B_WIDTH), F32)
    pad_ref[0:POOL_PAD, :] = zeros
    pad_ref[POOL_PAD + l:2 * POOL_PAD + l, :] = zeros
    pad_ref[POOL_PAD:POOL_PAD + l, :] = u_ref[0]
    ch = min(POOL_CHUNK, l)
    lane = lax.broadcasted_iota(I32, (ch, B_WIDTH), 1)
    grp = lane >> 6
    half = jnp.where(grp == 0, 1, jnp.where(grp == 1, 2, jnp.where(grp == 2, 4, 8)))
    row = lax.broadcasted_iota(I32, (ch, B_WIDTH), 0)
    for c in range(0, l, ch):
        ld = lambda k: pad_ref[c + POOL_PAD + k:c + POOL_PAD + k + ch, :]
        cur = ld(0)
        s2 = ld(-1) + cur
        s4 = s2 + (ld(-2) + ld(1))
        s8 = s4 + ((ld(-4) + ld(-3)) + (ld(2) + ld(3)))
        s16 = s8 + (((ld(-8) + ld(-7)) + (ld(-6) + ld(-5))) + ((ld(4) + ld(5)) + (ld(6) + ld(7))))
        win = jnp.where(grp == 0, s2, jnp.where(grp == 1, s4, jnp.where(grp == 2, s8, s16)))
        t = row + c
        cnt = (jnp.minimum(t + half, l) - jnp.maximum(t - half, 0)).astype(F32)
        r = (win / cnt - cur).astype(BF16)
        y = jnp.dot(r, w_ref[...], preferred_element_type=F32) * ps_ref[...]
        o_ref[0, c:c + ch, :] = y.astype(BF16)


def _pool(u, wbd, ps):
    b, l, w = u.shape
    return pl.pallas_call(
        functools.partial(_pool_kernel, l=l),
        grid=(b,),
        in_specs=[pl.BlockSpec((1, l, w), lambda i: (i, 0, 0)),
                  pl.BlockSpec((w, w), lambda i: (0, 0)),
                  pl.BlockSpec((1, w), lambda i: (0, 0))],
        out_specs=pl.BlockSpec((1, l, w), lambda i: (i, 0, 0)),
        out_shape=jax.ShapeDtypeStruct((b, l, w), BF16),
        scratch_shapes=[pltpu.VMEM((l + 2 * POOL_PAD, w), F32)],
        compiler_params=_params("arbitrary"),
    )(u, wbd, ps)


def _diff_attn_kernel(*refs, n_src, tq, lam_init):
    q_ref = refs[0]
    srcs = [(refs[1 + 2 * i], refs[2 + 2 * i]) for i in range(n_src)]
    lamp_ref, g_ref, o_ref, s_scr, p_scr = refs[1 + 2 * n_src:]
    chunks = []
    col = 0
    for k_ref, _ in srcs:
        keys = k_ref.shape[1]
        tk = min(keys, KEY_CHUNK)
        for c in range(0, keys, tk):
            chunks.append((k_ref, c, tk, col))
            col += tk
    q32 = q_ref[0].astype(F32)
    lp = lamp_ref[...]
    lam = (jnp.exp(jnp.sum(lp[0:1] * lp[1:2], axis=-1, keepdims=True))
           - jnp.exp(jnp.sum(lp[2:3] * lp[3:4], axis=-1, keepdims=True)) + lam_init)
    lane_row = lax.broadcasted_iota(I32, (1, A_WIDTH), 1)
    lane = lax.broadcasted_iota(I32, (tq, A_WIDTH), 1)

    def scores(j):
        qm = (q32 * ((lane_row >> 5) == j).astype(F32)).astype(BF16)
        part = jnp.full((tq, LANES), NEG, F32)
        for k_ref, c, tk, col in chunks:
            s = _nt_dot(qm, k_ref[0, c:c + tk, :])
            s_scr[j % 2, :, col:col + tk] = s
            for i in range(0, tk, LANES):
                part = jnp.maximum(part, s[:, i:i + LANES])
        return jnp.max(part, axis=-1, keepdims=True)

    def weights(j, row_max):
        part = jnp.zeros((tq, LANES), F32)
        for _, _, tk, col in chunks:
            p = jnp.exp2(s_scr[j % 2, :, col:col + tk] - row_max)
            for i in range(0, tk, LANES):
                part = part + p[:, i:i + LANES]
            p_scr[j % 2, :, col:col + tk] = p.astype(BF16)
        return jnp.sum(part, axis=-1, keepdims=True)

    def values(j, den):
        acc, col = None, 0
        for _, v_ref in srcs:
            keys = v_ref.shape[1]
            part = jnp.dot(p_scr[j % 2, :, col:col + keys], v_ref[0], preferred_element_type=F32)
            acc = part if acc is None else acc + part
            col += keys
        return acc / den

    out = jnp.zeros((tq, A_WIDTH), F32)
    row_max = scores(0)
    for j in range(2 * A_HEADS):
        den = weights(j, row_max)
        if j + 1 < 2 * A_HEADS:
            row_max = scores(j + 1)
        if j % 2 == 0:
            first = values(j, den)
        else:
            out = out + jnp.where((lane >> 6) == j // 2, first - lam * values(j, den), 0.0)
    sq = out * out
    rs = jnp.zeros((tq, A_WIDTH), F32)
    for h in range(A_HEADS):
        msk = (lane >> 6) == h
        ms = jnp.sum(jnp.where(msk, sq, 0.0), axis=-1, keepdims=True) * (1.0 / (2 * A_HD))
        rs = rs + jnp.where(msk, lax.rsqrt(ms + NORM_EPS), 0.0)
    o_ref[0] = (((out * rs) * g_ref[...]) * (1.0 - lam_init)).astype(BF16)


def _diff_attn(q, srcs, lam_p, g_tiled, lam_init):
    b, l, w = q.shape
    s_len = sum(k.shape[1] for k, _ in srcs)
    tq = min(ROW_TILE, l)
    in_specs = [pl.BlockSpec((1, tq, w), lambda i, t: (i, t, 0))]
    args = [q]
    for k, v in srcs:
        for a in (k, v):
            in_specs.append(pl.BlockSpec((1,) + a.shape[1:], lambda i, t: (i, 0, 0)))
            args.append(a)
    in_specs += [pl.BlockSpec(lam_p.shape, lambda i, t: (0, 0)),
                 pl.BlockSpec(g_tiled.shape, lambda i, t: (0, 0))]
    return pl.pallas_call(
        functools.partial(_diff_attn_kernel, n_src=len(srcs), tq=tq, lam_init=lam_init),
        grid=(b, l // tq),
        in_specs=in_specs,
        out_specs=pl.BlockSpec((1, tq, w), lambda i, t: (i, t, 0)),
        out_shape=jax.ShapeDtypeStruct((b, l, w), BF16),
        scratch_shapes=[pltpu.VMEM((2, tq, s_len), F32), pltpu.VMEM((2, tq, s_len), BF16)],
        compiler_params=_params("arbitrary", "arbitrary"),
    )(*args, lam_p, g_tiled)


GQA_SLAB = C_GROUP * C_HD


def _gqa_kernel(*refs, windowed, tq, l):
    if windowed:
        sink_ref, q_ref, k_ref, v_ref, kc_ref, vc_ref, o_ref, s_scr, p_scr = refs
    else:
        sink_ref, q_ref, k_ref, v_ref, o_ref, s_scr, p_scr = refs
    i = pl.program_id(1)
    rows = C_GROUP * tq
    shift = int(math.log2(tq))
    q32 = q_ref[0].astype(F32)
    lane_row = lax.broadcasted_iota(I32, (1, GQA_SLAB), 1)
    lane = lax.broadcasted_iota(I32, (tq, GQA_SLAB), 1)
    rid = lax.broadcasted_iota(I32, (rows, 1), 0)
    if windowed:
        ws = pl.multiple_of(jnp.clip((i - 1) * tq, 0, l - 3 * tq), tq)
        qpos = i * tq + (rid & (tq - 1))
        kpos = ws + lax.broadcasted_iota(I32, (1, 3 * tq), 1)
        band = jnp.where(jnp.abs(kpos - qpos) <= WINDOW, 0.0, NEG)
    def slab(g):
        return slice(GQA_SLAB * g, GQA_SLAB * (g + 1))

    def key_sources(g):
        if windowed:
            return [(lambda: k_ref[0, pl.ds(ws, 3 * tq), slab(g)], lambda: v_ref[0, pl.ds(ws, 3 * tq), slab(g)],
                     band, 3 * tq),
                    (lambda: kc_ref[0, :, slab(g)], lambda: vc_ref[0, :, slab(g)], None, kc_ref.shape[1])]
        return [(lambda: k_ref[0, :, slab(g)], lambda: v_ref[0, :, slab(g)], None, l)]

    def sink_col(g):
        sk = jnp.zeros((rows, 1), F32)
        for hh in range(C_GROUP):
            sk = jnp.where((rid >> shift) == hh, sink_ref[C_GROUP * g + hh] * LOG2E, sk)
        return sk

    def scores(g):
        qg = q32[:, slab(g)]
        qs = jnp.concatenate(
            [(qg * ((lane_row >> 6) == hh).astype(F32)).astype(BF16) for hh in range(C_GROUP)], axis=0)
        part = jnp.full((rows, LANES), NEG, F32)
        col = 0
        for keys, _, mask, n in key_sources(g):
            s = _nt_dot(qs, keys())
            if mask is not None:
                s = s + mask
            s_scr[g, :, col:col + n] = s
            for c in range(0, n, LANES):
                part = jnp.maximum(part, s[:, c:c + LANES])
            col += n
        return jnp.maximum(sink_col(g), jnp.max(part, axis=-1, keepdims=True))

    def weights(g, m):
        part = jnp.zeros((rows, LANES), F32)
        col = 0
        for _, _, _, n in key_sources(g):
            p = jnp.exp2(s_scr[g, :, col:col + n] - m)
            for c in range(0, n, LANES):
                part = part + p[:, c:c + LANES]
            p_scr[g, :, col:col + n] = p.astype(BF16)
            col += n
        return jnp.exp2(sink_col(g) - m) + jnp.sum(part, axis=-1, keepdims=True)

    def values(g, den):
        pv, col = None, 0
        for _, vals, _, n in key_sources(g):
            part = jnp.dot(p_scr[g, :, col:col + n], vals(), preferred_element_type=F32)
            pv = part if pv is None else pv + part
            col += n
        o = pv / den
        og = jnp.zeros((tq, GQA_SLAB), F32)
        for hh in range(C_GROUP):
            og = og + jnp.where((lane >> 6) == hh, o[hh * tq:(hh + 1) * tq, :], 0.0)
        o_ref[0, :, slab(g)] = og.astype(BF16)

    maxes = [scores(g) for g in range(C_KV)]
    dens = [weights(g, maxes[g]) for g in range(C_KV)]
    for g in range(C_KV):
        values(g, dens[g])


def _gqa(sink, q, k, v, cache=None):
    b, l, w = q.shape
    windowed = cache is not None
    tq = QBLOCK
    whole = lambda a: pl.BlockSpec((1,) + a.shape[1:], lambda i, t: (i, 0, 0))
    in_specs = [pl.BlockSpec(memory_space=pltpu.SMEM),
                pl.BlockSpec((1, tq, w), lambda i, t: (i, t, 0)), whole(k), whole(v)]
    args = [sink, q, k, v]
    n_keys = l
    if windowed:
        in_specs += [whole(cache[0]), whole(cache[1])]
        args += list(cache)
        n_keys = 3 * tq + cache[0].shape[1]
    return pl.pallas_call(
        functools.partial(_gqa_kernel, windowed=windowed, tq=tq, l=l),
        grid=(b, l // tq),
        in_specs=in_specs,
        out_specs=pl.BlockSpec((1, tq, w), lambda i, t: (i, t, 0)),
        out_shape=jax.ShapeDtypeStruct((b, l, w), BF16),
        scratch_shapes=[pltpu.VMEM((C_KV, C_GROUP * tq, n_keys), F32),
                        pltpu.VMEM((C_KV, C_GROUP * tq, n_keys), BF16)],
        compiler_params=_params("arbitrary", "arbitrary"),
    )(*args)


def _finish_kernel(x_ref, oa_ref, ob_ref, oc_ref, wo_ref, g1_ref, sh_ref, sc_ref, n2_ref, wr_ref, br_ref,
                   x1_o, h2_o, idx_o, gate_o, rank_o, cnt_o, *, tm):
    y = (jnp.dot(oa_ref[0], wo_ref[0:256, :], preferred_element_type=F32)
         + jnp.dot(ob_ref[0], wo_ref[256:512, :], preferred_element_type=F32)
         + jnp.dot(oc_ref[0], wo_ref[512:1024, :], preferred_element_type=F32))
    x1 = x_ref[0] + g1_ref[0] * y
    x1_o[0] = x1
    h2 = (_rms(x1) * n2_ref[...]) * (1.0 + sc_ref[0]) + sh_ref[0]
    h2_o[0] = h2
    lane = lax.broadcasted_iota(I32, (tm, LANES), 1)
    lanef = lane.astype(F32)
    logits = jnp.where(lane < N_EXPERTS, _dot3(h2, wr_ref[...]) + br_ref[...], NEG)
    work = logits
    tops, sels = [], []
    for k in range(TOP_K):
        mk = jnp.max(work, axis=-1, keepdims=True)
        ik = jnp.min(jnp.where(work == mk, lanef, float(LANES)), axis=-1, keepdims=True)
        sel = lanef == ik
        work = jnp.where(sel, 2.0 * NEG, work)
        tops.append((mk, ik))
        sels.append(sel)
    es = [jnp.exp(mk - tops[0][0]) for mk, _ in tops]
    den = (es[0] + es[1]) + (es[2] + es[3])
    multi = jnp.zeros((tm, LANES), F32)
    for sel in sels:
        multi = multi + sel.astype(F32)
    r_i = lax.broadcasted_iota(I32, (tm, tm), 0)
    c_i = lax.broadcasted_iota(I32, (tm, tm), 1)
    before = (c_i < r_i).astype(BF16)
    prior = jnp.dot(before, multi.astype(BF16), preferred_element_type=F32)
    idx = jnp.zeros((tm, LANES), F32)
    gate = jnp.zeros((tm, LANES), F32)
    rank = jnp.zeros((tm, LANES), F32)
    for k in range(TOP_K):
        slot = lane == k
        idx = jnp.where(slot, tops[k][1], idx)
        gate = jnp.where(slot, es[k] / den, gate)
        rk = jnp.sum(jnp.where(sels[k], prior, 0.0), axis=-1, keepdims=True)
        rank = jnp.where(slot, rk, rank)
    idx_o[0] = idx.astype(I32)
    gate_o[0] = gate
    rank_o[0] = rank.astype(I32)
    cnt_o[0] = jnp.sum(multi, axis=0, keepdims=True)


def _finish(x, oa, ob, oc, wo, g1, sh2, sc2, n2, wr, br):
    b, l, d = x.shape
    tm = min(ROW_TILE, l)
    nt = l // tm
    per_batch = g1.shape[0] > 1
    bidx = (lambda i, t: (i, 0, 0)) if per_batch else (lambda i, t: (0, 0, 0))
    tok = lambda width: pl.BlockSpec((1, tm, width), lambda i, t: (i, t, 0))
    const2 = lambda a: pl.BlockSpec(a.shape, lambda i, t: (0, 0))
    vec = pl.BlockSpec((1, 1, d), bidx)
    return pl.pallas_call(
        functools.partial(_finish_kernel, tm=tm),
        grid=(b, l // tm),
        in_specs=[tok(d), tok(256), tok(256), tok(512), const2(wo), vec, vec, vec,
                  const2(n2), const2(wr), const2(br)],
        out_specs=[tok(d), tok(d), tok(LANES), tok(LANES), tok(LANES),
                   pl.BlockSpec((1, 1, LANES), lambda i, t: (i * nt + t, 0, 0))],
        out_shape=[jax.ShapeDtypeStruct((b, l, d), F32), jax.ShapeDtypeStruct((b, l, d), F32),
                   jax.ShapeDtypeStruct((b, l, LANES), I32), jax.ShapeDtypeStruct((b, l, LANES), F32),
                   jax.ShapeDtypeStruct((b, l, LANES), I32), jax.ShapeDtypeStruct((b * nt, 1, LANES), F32)],
        compiler_params=_params("arbitrary", "arbitrary"),
    )(x, oa, ob, oc, wo, g1, sh2, sc2, n2, wr, br)


CHUNK = 8
SORT_ROWS = ROW_TILE * TOP_K + N_EXPERTS * CHUNK
MAX_CHUNKS = SORT_ROWS // CHUNK


def _slot_values(idx, rank, seg, k):
    lane = lax.broadcasted_iota(I32, idx.shape, 1)
    return jnp.where(lane == idx[:, k:k + 1], seg + rank[:, k:k + 1].astype(F32), 0.0)


def _dispatch_kernel(nch_ref, dst_ref, h_ref, idx_ref, rank_ref, seg_ref, xs_in_ref, xs_ref, srt, sems, *, tm, t0):
    del xs_in_ref
    tile = t0 + pl.program_id(0)
    idx, rank, seg = idx_ref[...], rank_ref[...], seg_ref[0]
    row = lax.broadcasted_iota(I32, (SORT_ROWS, tm), 0).astype(F32)
    ones = jnp.ones((CHUNK, LANES), BF16)
    perm = jnp.zeros((SORT_ROWS, tm), F32)
    for k in range(TOP_K):
        hi, lo = _split(_slot_values(idx, rank, seg, k))
        pos = (_nt_dot(ones, hi) + _nt_dot(ones, lo))[0:1, :]
        perm = perm + (row == pos).astype(F32)
    step = pl.program_id(0)
    slot = step % 2
    srt[slot] = _pack_rows(jnp.dot(perm.astype(BF16), h_ref[...].astype(BF16), preferred_element_type=F32))

    def chunk_copy(tl, sl, j):
        src = pl.multiple_of(j * CHUNK, CHUNK)
        dst = pl.multiple_of(dst_ref[tl * MAX_CHUNKS + j], CHUNK)
        return pltpu.make_async_copy(srt.at[sl, pl.ds(src, CHUNK)], xs_ref.at[pl.ds(dst, CHUNK)], sems.at[sl])

    def issue(j, carry):
        chunk_copy(tile, slot, j).start()
        return carry

    lax.fori_loop(0, nch_ref[tile], issue, 0)

    @pl.when(step > 0)
    def _():
        def drain(j, carry):
            chunk_copy(tile - 1, 1 - slot, j).wait()
            return carry
        lax.fori_loop(0, nch_ref[tile - 1], drain, 0)

    @pl.when(step == pl.num_programs(0) - 1)
    def _():
        def drain(j, carry):
            chunk_copy(tile, slot, j).wait()
            return carry
        lax.fori_loop(0, nch_ref[tile], drain, 0)


def _dispatch(nch, dst, h2, idx, rank, seg, xs, t0):
    n, d = h2.shape
    tm = ROW_TILE
    tok = lambda width: pl.BlockSpec((tm, width), lambda i, a, b: (i, 0))
    return pl.pallas_call(
        functools.partial(_dispatch_kernel, tm=tm, t0=t0),
        grid_spec=pltpu.PrefetchScalarGridSpec(
            num_scalar_prefetch=2,
            grid=(n // tm,),
            in_specs=[tok(d), tok(LANES), tok(LANES),
                      pl.BlockSpec((1, 1, LANES), lambda i, a, b: (t0 + i, 0, 0)),
                      pl.BlockSpec(memory_space=pl.ANY)],
            out_specs=pl.BlockSpec(memory_space=pl.ANY),
            scratch_shapes=[pltpu.VMEM((2, SORT_ROWS, HALF), U32), pltpu.SemaphoreType.DMA((2,))]),
        out_shape=jax.ShapeDtypeStruct(xs.shape, xs.dtype),
        input_output_aliases={6: 0},
        compiler_params=_params("arbitrary"),
    )(nch, dst, h2, idx, rank, seg, xs)


def _expert_kernel(start_ref, nblk_ref, xs_ref, w1_ref, b1_ref, w2_ref, b2_ref, o_ref,
                   w1b, w2b, xbuf, obuf, sin, sout):
    e = pl.program_id(0)
    n = nblk_ref[e]

    def rows(expert, j):
        return pl.ds(pl.multiple_of(start_ref[expert] + j * MOE_ROWS, MOE_ROWS), MOE_ROWS)

    def in_copy(expert, j, slot):
        return pltpu.make_async_copy(xs_ref.at[rows(expert, j)], xbuf.at[slot], sin.at[slot])

    def out_copy(j, slot):
        return pltpu.make_async_copy(obuf.at[slot], o_ref.at[rows(e, j)], sout.at[slot])

    @pl.when((e == 0) & (n > 0))
    def _():
        in_copy(e, 0, 0).start()

    @pl.when(n > 0)
    def _():
        w1b[...] = w1_ref[0].astype(BF16)
        w2b[...] = w2_ref[0].astype(BF16)

        def block(j, carry):
            slot = j % 2

            @pl.when(j + 1 < n)
            def _():
                in_copy(e, j + 1, 1 - slot).start()

            in_copy(e, j, slot).wait()

            @pl.when(j >= 2)
            def _():
                out_copy(j - 2, slot).wait()

            x_lo, x_hi = _unpack_rows(xbuf[slot])
            gu = (jnp.dot(x_lo, w1b[:HALF, :], preferred_element_type=F32)
                  + jnp.dot(x_hi, w1b[HALF:, :], preferred_element_type=F32)) + b1_ref[0]
            gt = jnp.minimum(gu[:, :D_FF], SWIGLU_LIMIT)
            up = jnp.clip(gu[:, D_FF:], -SWIGLU_LIMIT, SWIGLU_LIMIT)
            hid = (up + 1.0) * gt * (1.0 / (1.0 + jnp.exp(-SWIGLU_ALPHA * gt)))
            out = jnp.dot(hid.astype(BF16), w2b[...], preferred_element_type=F32) + b2_ref[0]
            obuf[slot] = _pack_rows(out.astype(BF16).astype(F32))
            out_copy(j, slot).start()
            return carry

        lax.fori_loop(0, n, block, 0)

        @pl.when(n >= 2)
        def _():
            out_copy(n - 2, n % 2).wait()

        out_copy(n - 1, (n - 1) % 2).wait()

    nxt = jnp.minimum(e + 1, pl.num_programs(0) - 1)

    @pl.when((e + 1 < pl.num_programs(0)) & (nblk_ref[nxt] > 0))
    def _():
        in_copy(nxt, 0, 0).start()


def _experts(row_start, n_blk, xs, w1, b1, w2, b2, layer):
    n_rows, half = xs.shape
    d = 2 * half
    f2 = w1.shape[-1]
    n_exp = w1.shape[1]
    ne = w1.shape[0] * n_exp
    w1 = w1.reshape(ne, d, f2)
    w2 = w2.reshape(ne, D_FF, d)
    pick = lambda e, st, nb: (layer * n_exp + e, 0, 0)
    return pl.pallas_call(
        _expert_kernel,
        grid_spec=pltpu.PrefetchScalarGridSpec(
            num_scalar_prefetch=2,
            grid=(n_exp,),
            in_specs=[pl.BlockSpec(memory_space=pl.ANY),
                      pl.BlockSpec((1, d, f2), pick), pl.BlockSpec((1, 1, f2), pick),
                      pl.BlockSpec((1, D_FF, d), pick), pl.BlockSpec((1, 1, d), pick)],
            out_specs=pl.BlockSpec(memory_space=pl.ANY),
            scratch_shapes=[pltpu.VMEM((d, f2), BF16), pltpu.VMEM((D_FF, d), BF16),
                            pltpu.VMEM((2, MOE_ROWS, half), U32), pltpu.VMEM((2, MOE_ROWS, half), U32),
                            pltpu.SemaphoreType.DMA((2,)), pltpu.SemaphoreType.DMA((2,))]),
        out_shape=jax.ShapeDtypeStruct((n_rows, half), U32),
        input_output_aliases={2: 0},
        compiler_params=_params("arbitrary"),
    )(row_start, n_blk, xs, w1, b1.reshape(ne, 1, f2), w2, b2.reshape(ne, 1, d))


def _combine_kernel(nch_ref, dst_ref, outs_ref, idx_ref, rank_ref, gate_ref, seg_ref, x1_ref, g2_ref, fg_ref,
                    o_ref, buf, sems, *, tm, nt, t0, final):
    step = pl.program_id(0) * nt + pl.program_id(1)
    tile = t0 + step
    slot = step % 2

    def chunk_copy(tl, sl, j):
        src = pl.multiple_of(dst_ref[tl * MAX_CHUNKS + j], CHUNK)
        dst = pl.multiple_of(j * CHUNK, CHUNK)
        return pltpu.make_async_copy(outs_ref.at[pl.ds(src, CHUNK)], buf.at[sl, pl.ds(dst, CHUNK)], sems.at[sl])

    def fetch(tl, sl):
        def issue(j, carry):
            chunk_copy(tl, sl, j).start()
            return carry
        lax.fori_loop(0, nch_ref[tl], issue, 0)

    @pl.when(step == 0)
    def _():
        buf[...] = jnp.zeros_like(buf)
        fetch(tile, slot)

    @pl.when(step + 1 < pl.num_programs(0) * nt)
    def _():
        fetch(tile + 1, 1 - slot)

    def drain(j, carry):
        chunk_copy(tile, slot, j).wait()
        return carry

    n = nch_ref[tile]
    idx, rank, gate, seg = idx_ref[0], rank_ref[0], gate_ref[0], seg_ref[0]
    col = lax.broadcasted_iota(I32, (tm, SORT_ROWS), 1).astype(F32)
    weights = jnp.zeros((tm, SORT_ROWS), F32)
    for k in range(TOP_K):
        pos = jnp.sum(_slot_values(idx, rank, seg, k), axis=-1, keepdims=True)
        weights = jnp.where(col == pos, gate[:, k:k + 1], weights)
    wh, wl = _split(weights)
    lax.fori_loop(0, n, drain, 0)
    y = jnp.concatenate(
        [jnp.dot(wh, rows, preferred_element_type=F32) + jnp.dot(wl, rows, preferred_element_type=F32)
         for rows in _unpack_rows(buf[slot])], axis=1)
    x2 = x1_ref[0] + g2_ref[0] * y
    if final:
        x2 = _rms(x2) * fg_ref[...]
    o_ref[0] = x2


def _combine(nch, dst, outs, idx, rank, gate, seg, x1, g2, fg, t0, final):
    b, l, d = x1.shape
    tm = ROW_TILE
    nt = l // tm
    per_batch = g2.shape[0] > 1
    bidx = (lambda i, t, a, c: (i, 0, 0)) if per_batch else (lambda i, t, a, c: (0, 0, 0))
    tok = lambda width: pl.BlockSpec((1, tm, width), lambda i, t, a, c: (i, t, 0))
    return pl.pallas_call(
        functools.partial(_combine_kernel, tm=tm, nt=nt, t0=t0, final=final),
        grid_spec=pltpu.PrefetchScalarGridSpec(
            num_scalar_prefetch=2,
            grid=(b, nt),
            in_specs=[pl.BlockSpec(memory_space=pl.ANY), tok(LANES), tok(LANES), tok(LANES),
                      pl.BlockSpec((1, 1, LANES), lambda i, t, a, c: (t0 + i * nt + t, 0, 0)),
                      tok(d), pl.BlockSpec((1, 1, d), bidx),
                      pl.BlockSpec((1, d), lambda i, t, a, c: (0, 0))],
            out_specs=tok(d),
            scratch_shapes=[pltpu.VMEM((2, SORT_ROWS, HALF), U32), pltpu.SemaphoreType.DMA((2,))]),
        out_shape=jax.ShapeDtypeStruct((b, l, d), F32),
        compiler_params=_params("arbitrary", "arbitrary"),
    )(nch, dst, outs, idx, rank, gate, seg, x1, g2, fg)


def _layer_weights(w_in_l):
    cuts = np.cumsum([A_WIDTH, A_WIDTH, A_WIDTH, B_WIDTH, C_WIDTH, C_KV * C_HD]).tolist()
    front = w_in_l[:, :cuts[4]]
    kc = w_in_l[:, cuts[4]:cuts[5]]
    vc = w_in_l[:, cuts[5]:]

    def rep(w):
        return jnp.concatenate([w[:, C_HD * (j // C_GROUP):C_HD * (j // C_GROUP + 1)] for j in range(C_HEADS)], axis=1)

    lat = jnp.concatenate([front, rep(kc), rep(vc)], axis=1).astype(BF16)
    ctx = jnp.concatenate([front, rep(kc), rep(vc), kc, vc], axis=1).astype(BF16)
    return ctx, lat


def _rep_heads(a):
    return jnp.repeat(a, C_GROUP, axis=2).reshape(a.shape[0], a.shape[1], C_HEADS * C_HD)


def kernel(x_prompt, x_sample, c, cache_diff_k, cache_diff_v, cache_win_k, cache_win_v, c_ctx, norm1_g, norm2_g, w_mod, b_mod, w_in, diff_lambda, diff_subln_g, w_pool, pool_scale, sink, w_out, w_router, b_router, w1, b1, w2, b2, final_g):
    depth = w_in.shape[0]
    bc, lc, d = x_prompt.shape
    bl, ll, _ = x_sample.shape
    n_ctx, n_lat = bc * lc, bl * ll
    n_tok = n_ctx + n_lat
    n_tiles = n_tok // ROW_TILE
    n_blocks = -(-(n_tok * TOP_K + n_tiles * N_EXPERTS * (CHUNK - 1)) // MOE_ROWS) + N_EXPERTS
    n_rows = n_blocks * MOE_ROWS

    mod_rows = -(-(1 + bl) // 8) * 8
    cmat = jnp.zeros((mod_rows, d), F32).at[0].set(c_ctx).at[1:1 + bl].set(c)
    mod = _mod_vectors(cmat, w_mod, b_mod)
    tabs_a = _rope_tables(ll, A_HD, A_WIDTH)
    tabs_c = _rope_tables(ll, C_HD, C_WIDTH)
    tables = (tabs_a[0], tabs_a[1], tabs_c[0], tabs_c[1])
    fg = final_g.reshape(1, d)

    xp, xs_lat = x_prompt, x_sample
    new_cache = [[], [], [], []]
    for i in range(depth):
        lam_init = 0.8 - 0.6 * math.exp(-0.3 * i)
        mv = lambda rows, j: mod[i, rows, j * d:(j + 1) * d].reshape(-1, 1, d)
        ctx_rows, lat_rows = slice(0, 1), slice(1, 1 + bl)
        w_ctx, w_lat = _layer_weights(w_in[i])
        n1 = norm1_g[i].reshape(1, d)
        n2 = norm2_g[i].reshape(1, d)
        wbd = jax.scipy.linalg.block_diag(*[w_pool[i, g] for g in range(B_GROUPS)]).astype(BF16)
        ps = pool_scale[i].reshape(1, B_WIDTH)
        g_tiled = jnp.tile(diff_subln_g[i], A_HEADS).reshape(1, A_WIDTH)
        wo = w_out[i].astype(BF16)
        wr = jnp.zeros((d, LANES), F32).at[:, :N_EXPERTS].set(w_router[i])
        br = jnp.zeros((1, LANES), F32).at[0, :N_EXPERTS].set(b_router[i])

        qa, ka, va, u, qc, kr, vr, ka32, va32, kc32, vc32 = _inproj(
            xp, mv(ctx_rows, 0), mv(ctx_rows, 1), n1, w_ctx, None)
        new_cache[0].append(ka32.reshape(bc, lc, 2 * A_HEADS, A_HD))
        new_cache[1].append(va32.reshape(bc, lc, A_HEADS, 2 * A_HD))
        new_cache[2].append(kc32.reshape(bc, lc, C_KV, C_HD))
        new_cache[3].append(vc32.reshape(bc, lc, C_KV, C_HD))
        ob = _pool(u, wbd, ps)
        oa = _diff_attn(qa, [(ka, va)], diff_lambda[i], g_tiled, lam_init)
        oc = _gqa(sink[i], qc, kr, vr)
        x1_c, h2_c, idx_c, gate_c, rank_c, cnt_c = _finish(
            xp, oa, ob, oc, wo, mv(ctx_rows, 2), mv(ctx_rows, 3), mv(ctx_rows, 4), n2, wr, br)

        qa, ka, va, u, qc, kr, vr = _inproj(xs_lat, mv(lat_rows, 0), mv(lat_rows, 1), n1, w_lat, tables)
        ob = _pool(u, wbd, ps)
        dk = cache_diff_k[:, i].reshape(bl, -1, A_WIDTH).astype(BF16)
        dv = cache_diff_v[:, i].reshape(bl, -1, A_WIDTH).astype(BF16)
        oa = _diff_attn(qa, [(ka, va), (dk, dv)], diff_lambda[i], g_tiled, lam_init)
        wk = _rep_heads(cache_win_k[:, i]).astype(BF16)
        wv = _rep_heads(cache_win_v[:, i]).astype(BF16)
        oc = _gqa(sink[i], qc, kr, vr, (wk, wv))
        x1_l, h2_l, idx_l, gate_l, rank_l, cnt_l = _finish(
            xs_lat, oa, ob, oc, wo, mv(lat_rows, 2), mv(lat_rows, 3), mv(lat_rows, 4), n2, wr, br)

        cnt = jnp.concatenate([cnt_c, cnt_l], axis=0)[:, 0, :N_EXPERTS].astype(I32)
        c8 = (cnt + CHUNK - 1) // CHUNK * CHUNK
        seg_end = jnp.cumsum(c8, axis=1)
        seg = seg_end - c8
        padded = (jnp.sum(c8, axis=0) + MOE_ROWS - 1) // MOE_ROWS * MOE_ROWS
        pad_end = jnp.cumsum(padded)
        gbase = (pad_end - padded)[None, :] + jnp.cumsum(c8, axis=0) - c8
        nch = seg_end[:, -1] // CHUNK
        j8 = jnp.arange(MAX_CHUNKS, dtype=I32) * CHUNK
        chunk_e = jnp.minimum(jnp.sum((seg_end[:, None, :] <= j8[None, :, None]).astype(I32), axis=-1),
                              N_EXPERTS - 1)
        onehot = chunk_e[..., None] == jnp.arange(N_EXPERTS, dtype=I32)
        dst = (jnp.sum(jnp.where(onehot, (gbase - seg)[:, None, :], 0), axis=-1) + j8[None, :]).reshape(-1)
        seg_f = jnp.zeros((cnt.shape[0], 1, LANES), F32).at[:, 0, :N_EXPERTS].set(seg.astype(F32))
        tiles_c = cnt_c.shape[0]

        rows_in = jnp.zeros((n_rows, HALF), U32)
        rows_in = _dispatch(nch, dst, h2_c.reshape(n_ctx, d), idx_c.reshape(n_ctx, LANES),
                            rank_c.reshape(n_ctx, LANES), seg_f, rows_in, 0)
        rows_in = _dispatch(nch, dst, h2_l.reshape(n_lat, d), idx_l.reshape(n_lat, LANES),
                            rank_l.reshape(n_lat, LANES), seg_f, rows_in, tiles_c)
        rows_out = _experts(pad_end - padded, padded // MOE_ROWS, rows_in, w1, b1, w2, b2, i)
        final = i == depth - 1
        xp = _combine(nch, dst, rows_out, idx_c, rank_c, gate_c, seg_f, x1_c, mv(ctx_rows, 5), fg, 0, final)
        xs_lat = _combine(nch, dst, rows_out, idx_l, rank_l, gate_l, seg_f, x1_l, mv(lat_rows, 5), fg,
                          tiles_c, final)

    return (xp, xs_lat) + tuple(jnp.stack(parts, axis=1) for parts in new_cache)
```

```python
import functools
import math

import numpy as np
import jax
import jax.numpy as jnp
from jax import lax
from jax.experimental import pallas as pl
from jax.experimental.pallas import tpu as pltpu

F32 = jnp.float32
BF16 = jnp.bfloat16
I32 = jnp.int32
U32 = jnp.uint32

D_MODEL = 1024
GRID_W = 64
ROPE_BASE = 10000.0
NORM_EPS = 1e-6
A_HD = 32
A_HEADS = 4
A_WIDTH = 256
B_WIDTH = 256
B_GROUPS = 4
B_GC = 64
C_HD = 64
C_HEADS = 8
C_KV = 2
C_GROUP = 4
C_WIDTH = 512
WINDOW = 128
QBLOCK = 128
N_EXPERTS = 32
TOP_K = 4
D_FF = 1024
SWIGLU_LIMIT = 7.0
SWIGLU_ALPHA = 1.702

LANES = 128
ROW_TILE = 256
MOE_ROWS = 512
KEY_CHUNK = 512
NEG = -1e30
LOG2E = math.log2(math.e)
VMEM_LIMIT = 56 * 1024 * 1024


def _params(*sem):
    return pltpu.CompilerParams(dimension_semantics=sem, vmem_limit_bytes=VMEM_LIMIT)


def _split(x):
    hi = x.astype(BF16)
    lo = (x - hi.astype(F32)).astype(BF16)
    return hi, lo


def _dot3(a, b):
    ah, al = _split(a)
    bh, bl = _split(b)
    return (jnp.dot(ah, bh, preferred_element_type=F32)
            + (jnp.dot(ah, bl, preferred_element_type=F32)
               + jnp.dot(al, bh, preferred_element_type=F32)))


def _nt_dot(a, b):
    return lax.dot_general(a, b, (((1,), (1,)), ((), ())), preferred_element_type=F32)


def _rms(x):
    return x * lax.rsqrt(jnp.mean(x * x, axis=-1, keepdims=True) + NORM_EPS)


HALF = D_MODEL // 2
HIGH_BITS = 0xFFFF0000


def _pack_rows(x):
    lo = lax.bitcast_convert_type(x[:, :HALF], U32) >> 16
    hi = lax.bitcast_convert_type(x[:, HALF:], U32) & jnp.uint32(HIGH_BITS)
    return lo | hi


def _unpack_rows(w):
    lo = lax.bitcast_convert_type(w << 16, F32).astype(BF16)
    hi = lax.bitcast_convert_type(w & jnp.uint32(HIGH_BITS), F32).astype(BF16)
    return lo, hi


def _mod_kernel(c_ref, w_ref, b_ref, o_ref):
    c = c_ref[...]
    a = c * (1.0 / (1.0 + jnp.exp(-c)))
    o_ref[0] = _dot3(a, w_ref[0]) + b_ref[0]


def _mod_vectors(cmat, w_mod, b_mod):
    depth, d, e = w_mod.shape
    rows = cmat.shape[0]
    tn = 512
    return pl.pallas_call(
        _mod_kernel,
        grid=(depth, e // tn),
        in_specs=[pl.BlockSpec((rows, d), lambda l, j: (0, 0)),
                  pl.BlockSpec((1, d, tn), lambda l, j: (l, 0, j)),
                  pl.BlockSpec((1, 1, tn), lambda l, j: (l, 0, j))],
        out_specs=pl.BlockSpec((1, rows, tn), lambda l, j: (l, 0, j)),
        out_shape=jax.ShapeDtypeStruct((depth, rows, e), F32),
        compiler_params=_params("arbitrary", "arbitrary"),
    )(cmat, w_mod, b_mod.reshape(depth, 1, e))


def _rope(z, col_ref, row_ref, nf, r0):
    tm, w = z.shape
    outs = []
    for g in range(tm // GRID_W):
        zs = z[GRID_W * g:GRID_W * (g + 1), :]
        c = col_ref[0] + row_ref[0, pl.ds(r0 + g, 1), :]
        sm = col_ref[1] + row_ref[1, pl.ds(r0 + g, 1), :]
        sp = col_ref[2] + row_ref[2, pl.ds(r0 + g, 1), :]
        outs.append(zs * c + pltpu.roll(zs, w - nf, 1) * sm + pltpu.roll(zs, nf, 1) * sp)
    return jnp.concatenate(outs, axis=0)


def _inproj_kernel(*refs, rope, tm):
    x_ref, sh_ref, sc_ref, n1_ref, w_ref = refs[:5]
    if rope:
        ta_col, ta_row, tc_col, tc_row = refs[5:9]
        qa_o, ka_o, va_o, u_o, qc_o, kr_o, vr_o = refs[9:]
    else:
        qa_o, ka_o, va_o, u_o, qc_o, kr_o, vr_o, ka32_o, va32_o, kc32_o, vc32_o = refs[5:]
    x = x_ref[0]
    h = (_rms(x) * n1_ref[...]) * (1.0 + sc_ref[0]) + sh_ref[0]
    hb = h.astype(BF16)

    def seg(a, b):
        return jnp.dot(hb, w_ref[:, a:b], preferred_element_type=F32)

    qa, ka, va, u = seg(0, 256), seg(256, 512), seg(512, 768), seg(768, 1024)
    qc, kr, vr = seg(1024, 1536), seg(1536, 2048), seg(2048, 2560)
    if rope:
        r0 = pl.program_id(1) * (tm // GRID_W)
        qa = _rope(qa, ta_col, ta_row, A_HD // 4, r0)
        ka = _rope(ka, ta_col, ta_row, A_HD // 4, r0)
        qc = _rope(qc, tc_col, tc_row, C_HD // 4, r0)
        kr = _rope(kr, tc_col, tc_row, C_HD // 4, r0)
    else:
        ka32_o[0] = ka
        va32_o[0] = va
        kc32_o[0] = seg(2560, 2688)
        vc32_o[0] = seg(2688, 2816)
    qa_o[0] = (qa * (A_HD ** -0.5 * LOG2E)).astype(BF16)
    ka_o[0] = ka.astype(BF16)
    va_o[0] = va.astype(BF16)
    u_o[0] = u
    qc_o[0] = (qc * (C_HD ** -0.5 * LOG2E)).astype(BF16)
    kr_o[0] = kr.astype(BF16)
    vr_o[0] = vr.astype(BF16)


def _inproj(x, sh, sc, n1, w, tables):
    b, l, d = x.shape
    tm = min(ROW_TILE, l)
    rope = tables is not None
    per_batch = sh.shape[0] > 1
    bidx = (lambda i, t: (i, 0, 0)) if per_batch else (lambda i, t: (0, 0, 0))
    tok = lambda width: pl.BlockSpec((1, tm, width), lambda i, t: (i, t, 0))
    in_specs = [tok(d),
                pl.BlockSpec((1, 1, d), bidx), pl.BlockSpec((1, 1, d), bidx),
                pl.BlockSpec((1, d), lambda i, t: (0, 0)),
                pl.BlockSpec(w.shape, lambda i, t: (0, 0))]
    args = [x, sh, sc, n1, w]
    widths = [(256, BF16), (256, BF16), (256, BF16), (256, F32), (512, BF16), (512, BF16), (512, BF16)]
    if rope:
        for tab in tables:
            in_specs.append(pl.BlockSpec(tab.shape, lambda i, t: (0, 0, 0)))
            args.append(tab)
    else:
        widths += [(256, F32), (256, F32), (128, F32), (128, F32)]
    return pl.pallas_call(
        functools.partial(_inproj_kernel, rope=rope, tm=tm),
        grid=(b, l // tm),
        in_specs=in_specs,
        out_specs=[tok(wd) for wd, _ in widths],
        out_shape=[jax.ShapeDtypeStruct((b, l, wd), dt) for wd, dt in widths],
        compiler_params=_params("arbitrary", "arbitrary"),
    )(*args)


def _rope_tables(n_lat, head_dim, width):
    rows = n_lat // GRID_W
    nf = head_dim // 4
    inv = ROPE_BASE ** (-jnp.arange(nf, dtype=F32) / nf)
    lane = np.arange(width) % head_dim
    half = lane // (2 * nf)
    pair = (lane // nf) % 2
    f = lane % nf

    def part(pos, which):
        ang = pos[:, None] * inv[f][None, :]
        on = jnp.asarray(half == which, F32)[None, :]
        c = jnp.cos(ang) * on
        s = jnp.sin(ang) * on
        sm = -s * jnp.asarray(pair == 0, F32)[None, :]
        sp = s * jnp.asarray(pair == 1, F32)[None, :]
        return jnp.stack([c, sm, sp])

    return part(jnp.arange(GRID_W, dtype=F32), 1), part(jnp.arange(rows, dtype=F32), 0)


POOL_PAD = 8
POOL_CHUNK = 256


def _pool_kernel(u_ref, w_ref, ps_ref, o_ref, pad_ref, *, l):
    zeros = jnp.zeros((POOL_PAD, B_WIDTH), F32)
    pad_ref[0:POOL_PAD, :] = zeros
    pad_ref[POOL_PAD + l:2 * POOL_PAD + l, :] = zeros
    pad_ref[POOL_PAD:POOL_PAD + l, :] = u_ref[0]
    ch = min(POOL_CHUNK, l)
    lane = lax.broadcasted_iota(I32, (ch, B_WIDTH), 1)
    grp = lane >> 6
    half = jnp.where(grp == 0, 1, jnp.where(grp == 1, 2, jnp.where(grp == 2, 4, 8)))
    row = lax.broadcasted_iota(I32, (ch, B_WIDTH), 0)
    for c in range(0, l, ch):
        ld = lambda k: pad_ref[c + POOL_PAD + k:c + POOL_PAD + k + ch, :]
        cur = ld(0)
        s2 = ld(-1) + cur
        s4 = s2 + (ld(-2) + ld(1))
        s8 = s4 + ((ld(-4) + ld(-3)) + (ld(2) + ld(3)))
        s16 = s8 + (((ld(-8) + ld(-7)) + (ld(-6) + ld(-5))) + ((ld(4) + ld(5)) + (ld(6) + ld(7))))
        win = jnp.where(grp == 0, s2, jnp.where(grp == 1, s4, jnp.where(grp == 2, s8, s16)))
        t = row + c
        cnt = (jnp.minimum(t + half, l) - jnp.maximum(t - half, 0)).astype(F32)
        r = (win / cnt - cur).astype(BF16)
        y = jnp.dot(r, w_ref[...], preferred_element_type=F32) * ps_ref[...]
        o_ref[0, c:c + ch, :] = y.astype(BF16)


def _pool(u, wbd, ps):
    b, l, w = u.shape
    return pl.pallas_call(
        functools.partial(_pool_kernel, l=l),
        grid=(b,),
        in_specs=[pl.BlockSpec((1, l, w), lambda i: (i, 0, 0)),
                  pl.BlockSpec((w, w), lambda i: (0, 0)),
                  pl.BlockSpec((1, w), lambda i: (0, 0))],
        out_specs=pl.BlockSpec((1, l, w), lambda i: (i, 0, 0)),
        out_shape=jax.ShapeDtypeStruct((b, l, w), BF16),
        scratch_shapes=[pltpu.VMEM((l + 2 * POOL_PAD, w), F32)],
        compiler_params=_params("arbitrary"),
    )(u, wbd, ps)


def _diff_attn_kernel(*refs, n_src, tq, lam_init):
    q_ref = refs[0]
    srcs = [(refs[1 + 2 * i], refs[2 + 2 * i]) for i in range(n_src)]
    lamp_ref, g_ref, o_ref, s_scr, p_scr = refs[1 + 2 * n_src:]
    chunks = []
    col = 0
    for k_ref, _ in srcs:
        keys = k_ref.shape[1]
        tk = min(keys, KEY_CHUNK)
        for c in range(0, keys, tk):
            chunks.append((k_ref, c, tk, col))
            col += tk
    q32 = q_ref[0].astype(F32)
    lp = lamp_ref[...]
    lam = (jnp.exp(jnp.sum(lp[0:1] * lp[1:2], axis=-1, keepdims=True))
           - jnp.exp(jnp.sum(lp[2:3] * lp[3:4], axis=-1, keepdims=True)) + lam_init)
    lane_row = lax.broadcasted_iota(I32, (1, A_WIDTH), 1)
    lane = lax.broadcasted_iota(I32, (tq, A_WIDTH), 1)

    def scores(j):
        qm = (q32 * ((lane_row >> 5) == j).astype(F32)).astype(BF16)
        part = jnp.full((tq, LANES), NEG, F32)
        for k_ref, c, tk, col in chunks:
            s = _nt_dot(qm, k_ref[0, c:c + tk, :])
            s_scr[j % 2, :, col:col + tk] = s
            for i in range(0, tk, LANES):
                part = jnp.maximum(part, s[:, i:i + LANES])
        return jnp.max(part, axis=-1, keepdims=True)

    def weights(j, row_max):
        part = jnp.zeros((tq, LANES), F32)
        for _, _, tk, col in chunks:
            p = jnp.exp2(s_scr[j % 2, :, col:col + tk] - row_max)
            for i in range(0, tk, LANES):
                part = part + p[:, i:i + LANES]
            p_scr[j % 2, :, col:col + tk] = p.astype(BF16)
        return jnp.sum(part, axis=-1, keepdims=True)

    def values(j, den):
        acc, col = None, 0
        for _, v_ref in srcs:
            keys = v_ref.shape[1]
            part = jnp.dot(p_scr[j % 2, :, col:col + keys], v_ref[0], preferred_element_type=F32)
            acc = part if acc is None else acc + part
            col += keys
        return acc / den

    out = jnp.zeros((tq, A_WIDTH), F32)
    row_max = scores(0)
    for j in range(2 * A_HEADS):
        den = weights(j, row_max)
        if j + 1 < 2 * A_HEADS:
            row_max = scores(j + 1)
        if j % 2 == 0:
            first = values(j, den)
        else:
            out = out + jnp.where((lane >> 6) == j // 2, first - lam * values(j, den), 0.0)
    sq = out * out
    rs = jnp.zeros((tq, A_WIDTH), F32)
    for h in range(A_HEADS):
        msk = (lane >> 6) == h
        ms = jnp.sum(jnp.where(msk, sq, 0.0), axis=-1, keepdims=True) * (1.0 / (2 * A_HD))
        rs = rs + jnp.where(msk, lax.rsqrt(ms + NORM_EPS), 0.0)
    o_ref[0] = (((out * rs) * g_ref[...]) * (1.0 - lam_init)).astype(BF16)


def _diff_attn(q, srcs, lam_p, g_tiled, lam_init):
    b, l, w = q.shape
    s_len = sum(k.shape[1] for k, _ in srcs)
    tq = min(ROW_TILE, l)
    in_specs = [pl.BlockSpec((1, tq, w), lambda i, t: (i, t, 0))]
    args = [q]
    for k, v in srcs:
        for a in (k, v):
            in_specs.append(pl.BlockSpec((1,) + a.shape[1:], lambda i, t: (i, 0, 0)))
            args.append(a)
    in_specs += [pl.BlockSpec(lam_p.shape, lambda i, t: (0, 0)),
                 pl.BlockSpec(g_tiled.shape, lambda i, t: (0, 0))]
    return pl.pallas_call(
        functools.partial(_diff_attn_kernel, n_src=len(srcs), tq=tq, lam_init=lam_init),
        grid=(b, l // tq),
        in_specs=in_specs,
        out_specs=pl.BlockSpec((1, tq, w), lambda i, t: (i, t, 0)),
        out_shape=jax.ShapeDtypeStruct((b, l, w), BF16),
        scratch_shapes=[pltpu.VMEM((2, tq, s_len), F32), pltpu.VMEM((2, tq, s_len), BF16)],
        compiler_params=_params("arbitrary", "arbitrary"),
    )(*args, lam_p, g_tiled)


GQA_SLAB = C_GROUP * C_HD


def _gqa_kernel(*refs, windowed, tq, l):
    if windowed:
        sink_ref, q_ref, k_ref, v_ref, kc_ref, vc_ref, o_ref, s_scr, p_scr = refs
    else:
        sink_ref, q_ref, k_ref, v_ref, o_ref, s_scr, p_scr = refs
    i = pl.program_id(1)
    rows = C_GROUP * tq
    shift = int(math.log2(tq))
    q32 = q_ref[0].astype(F32)
    lane_row = lax.broadcasted_iota(I32, (1, GQA_SLAB), 1)
    lane = lax.broadcasted_iota(I32, (tq, GQA_SLAB), 1)
    rid = lax.broadcasted_iota(I32, (rows, 1), 0)
    if windowed:
        ws = pl.multiple_of(jnp.clip((i - 1) * tq, 0, l - 3 * tq), tq)
        qpos = i * tq + (rid & (tq - 1))
        kpos = ws + lax.broadcasted_iota(I32, (1, 3 * tq), 1)
        band = jnp.where(jnp.abs(kpos - qpos) <= WINDOW, 0.0, NEG)
    def slab(g):
        return slice(GQA_SLAB * g, GQA_SLAB * (g + 1))

    def key_sources(g):
        if windowed:
            return [(lambda: k_ref[0, pl.ds(ws, 3 * tq), slab(g)], lambda: v_ref[0, pl.ds(ws, 3 * tq), slab(g)],
                     band, 3 * tq),
                    (lambda: kc_ref[0, :, slab(g)], lambda: vc_ref[0, :, slab(g)], None, kc_ref.shape[1])]
        return [(lambda: k_ref[0, :, slab(g)], lambda: v_ref[0, :, slab(g)], None, l)]

    def sink_col(g):
        sk = jnp.zeros((rows, 1), F32)
        for hh in range(C_GROUP):
            sk = jnp.where((rid >> shift) == hh, sink_ref[C_GROUP * g + hh] * LOG2E, sk)
        return sk

    def scores(g):
        qg = q32[:, slab(g)]
        qs = jnp.concatenate(
            [(qg * ((lane_row >> 6) == hh).astype(F32)).astype(BF16) for hh in range(C_GROUP)], axis=0)
        part = jnp.full((rows, LANES), NEG, F32)
        col = 0
        for keys, _, mask, n in key_sources(g):
            s = _nt_dot(qs, keys())
            if mask is not None:
                s = s + mask
            s_scr[g, :, col:col + n] = s
            for c in range(0, n, LANES):
                part = jnp.maximum(part, s[:, c:c + LANES])
            col += n
        return jnp.maximum(sink_col(g), jnp.max(part, axis=-1, keepdims=True))

    def weights(g, m):
        part = jnp.zeros((rows, LANES), F32)
        col = 0
        for _, _, _, n in key_sources(g):
            p = jnp.exp2(s_scr[g, :, col:col + n] - m)
            for c in range(0, n, LANES):
                part = part + p[:, c:c + LANES]
            p_scr[g, :, col:col + n] = p.astype(BF16)
            col += n
        return jnp.exp2(sink_col(g) - m) + jnp.sum(part, axis=-1, keepdims=True)

    def values(g, den):
        pv, col = None, 0
        for _, vals, _, n in key_sources(g):
            part = jnp.dot(p_scr[g, :, col:col + n], vals(), preferred_element_type=F32)
            pv = part if pv is None else pv + part
            col += n
        o = pv / den
        og = jnp.zeros((tq, GQA_SLAB), F32)
        for hh in range(C_GROUP):
            og = og + jnp.where((lane >> 6) == hh, o[hh * tq:(hh + 1) * tq, :], 0.0)
        o_ref[0, :, slab(g)] = og.astype(BF16)

    maxes = [scores(g) for g in range(C_KV)]
    dens = [weights(g, maxes[g]) for g in range(C_KV)]
    for g in range(C_KV):
        values(g, dens[g])


def _gqa(sink, q, k, v, cache=None):
    b, l, w = q.shape
    windowed = cache is not None
    tq = QBLOCK
    whole = lambda a: pl.BlockSpec((1,) + a.shape[1:], lambda i, t: (i, 0, 0))
    in_specs = [pl.BlockSpec(memory_space=pltpu.SMEM),
                pl.BlockSpec((1, tq, w), lambda i, t: (i, t, 0)), whole(k), whole(v)]
    args = [sink, q, k, v]
    n_keys = l
    if windowed:
        in_specs += [whole(cache[0]), whole(cache[1])]
        args += list(cache)
        n_keys = 3 * tq + cache[0].shape[1]
    return pl.pallas_call(
        functools.partial(_gqa_kernel, windowed=windowed, tq=tq, l=l),
        grid=(b, l // tq),
        in_specs=in_specs,
        out_specs=pl.BlockSpec((1, tq, w), lambda i, t: (i, t, 0)),
        out_shape=jax.ShapeDtypeStruct((b, l, w), BF16),
        scratch_shapes=[pltpu.VMEM((C_KV, C_GROUP * tq, n_keys), F32),
                        pltpu.VMEM((C_KV, C_GROUP * tq, n_keys), BF16)],
        compiler_params=_params("arbitrary", "arbitrary"),
    )(*args)


def _finish_kernel(x_ref, oa_ref, ob_ref, oc_ref, wo_ref, g1_ref, sh_ref, sc_ref, n2_ref, wr_ref, br_ref,
                   x1_o, h2_o, idx_o, gate_o, rank_o, cnt_o, *, tm):
    y = (jnp.dot(oa_ref[0], wo_ref[0:256, :], preferred_element_type=F32)
         + jnp.dot(ob_ref[0], wo_ref[256:512, :], preferred_element_type=F32)
         + jnp.dot(oc_ref[0], wo_ref[512:1024, :], preferred_element_type=F32))
    x1 = x_ref[0] + g1_ref[0] * y
    x1_o[0] = x1
    h2 = (_rms(x1) * n2_ref[...]) * (1.0 + sc_ref[0]) + sh_ref[0]
    h2_o[0] = h2
    lane = lax.broadcasted_iota(I32, (tm, LANES), 1)
    lanef = lane.astype(F32)
    logits = jnp.where(lane < N_EXPERTS, _dot3(h2, wr_ref[...]) + br_ref[...], NEG)
    work = logits
    tops, sels = [], []
    for k in range(TOP_K):
        mk = jnp.max(work, axis=-1, keepdims=True)
        ik = jnp.min(jnp.where(work == mk, lanef, float(LANES)), axis=-1, keepdims=True)
        sel = lanef == ik
        work = jnp.where(sel, 2.0 * NEG, work)
        tops.append((mk, ik))
        sels.append(sel)
    es = [jnp.exp(mk - tops[0][0]) for mk, _ in tops]
    den = (es[0] + es[1]) + (es[2] + es[3])
    multi = jnp.zeros((tm, LANES), F32)
    for sel in sels:
        multi = multi + sel.astype(F32)
    r_i = lax.broadcasted_iota(I32, (tm, tm), 0)
    c_i = lax.broadcasted_iota(I32, (tm, tm), 1)
    before = (c_i < r_i).astype(BF16)
    prior = jnp.dot(before, multi.astype(BF16), preferred_element_type=F32)
    idx = jnp.zeros((tm, LANES), F32)
    gate = jnp.zeros((tm, LANES), F32)
    rank = jnp.zeros((tm, LANES), F32)
    for k in range(TOP_K):
        slot = lane == k
        idx = jnp.where(slot, tops[k][1], idx)
        gate = jnp.where(slot, es[k] / den, gate)
        rk = jnp.sum(jnp.where(sels[k], prior, 0.0), axis=-1, keepdims=True)
        rank = jnp.where(slot, rk, rank)
    idx_o[0] = idx.astype(I32)
    gate_o[0] = gate
    rank_o[0] = rank.astype(I32)
    cnt_o[0] = jnp.sum(multi, axis=0, keepdims=True)


def _finish(x, oa, ob, oc, wo, g1, sh2, sc2, n2, wr, br):
    b, l, d = x.shape
    tm = min(ROW_TILE, l)
    nt = l // tm
    per_batch = g1.shape[0] > 1
    bidx = (lambda i, t: (i, 0, 0)) if per_batch else (lambda i, t: (0, 0, 0))
    tok = lambda width: pl.BlockSpec((1, tm, width), lambda i, t: (i, t, 0))
    const2 = lambda a: pl.BlockSpec(a.shape, lambda i, t: (0, 0))
    vec = pl.BlockSpec((1, 1, d), bidx)
    return pl.pallas_call(
        functools.partial(_finish_kernel, tm=tm),
        grid=(b, l // tm),
        in_specs=[tok(d), tok(256), tok(256), tok(512), const2(wo), vec, vec, vec,
                  const2(n2), const2(wr), const2(br)],
        out_specs=[tok(d), tok(d), tok(LANES), tok(LANES), tok(LANES),
                   pl.BlockSpec((1, 1, LANES), lambda i, t: (i * nt + t, 0, 0))],
        out_shape=[jax.ShapeDtypeStruct((b, l, d), F32), jax.ShapeDtypeStruct((b, l, d), F32),
                   jax.ShapeDtypeStruct((b, l, LANES), I32), jax.ShapeDtypeStruct((b, l, LANES), F32),
                   jax.ShapeDtypeStruct((b, l, LANES), I32), jax.ShapeDtypeStruct((b * nt, 1, LANES), F32)],
        compiler_params=_params("arbitrary", "arbitrary"),
    )(x, oa, ob, oc, wo, g1, sh2, sc2, n2, wr, br)


CHUNK = 8
SORT_ROWS = ROW_TILE * TOP_K + N_EXPERTS * CHUNK
MAX_CHUNKS = SORT_ROWS // CHUNK
MIN_CHUNKS = ROW_TILE * TOP_K // CHUNK
HALF_SORT = SORT_ROWS // 2
HALF_CHUNKS = HALF_SORT // CHUNK
assert HALF_CHUNKS <= MIN_CHUNKS


def _slot_values(idx, rank, seg, k):
    lane = lax.broadcasted_iota(I32, idx.shape, 1)
    return jnp.where(lane == idx[:, k:k + 1], seg + rank[:, k:k + 1].astype(F32), 0.0)


def _dispatch_kernel(nch_ref, dst_ref, h_ref, idx_ref, rank_ref, seg_ref, xs_in_ref, xs_ref, srt_a, srt_b, sems,
                     *, tm, t0):
    del xs_in_ref
    step = pl.program_id(0)
    tile = t0 + step
    slot = step % 2
    idx, rank, seg = idx_ref[...], rank_ref[...], seg_ref[0]
    ones = jnp.ones((CHUNK, LANES), BF16)
    pos = []
    for k in range(TOP_K):
        hi, lo = _split(_slot_values(idx, rank, seg, k))
        pos.append((_nt_dot(ones, hi) + _nt_dot(ones, lo))[0:1, :])
    hb = h_ref[...].astype(BF16)

    def sorted_half(first_row):
        row = (lax.broadcasted_iota(I32, (HALF_SORT, tm), 0) + first_row).astype(F32)
        perm = jnp.zeros((HALF_SORT, tm), F32)
        for p in pos:
            perm = perm + (row == p).astype(F32)
        return _pack_rows(jnp.dot(perm.astype(BF16), hb, preferred_element_type=F32))

    def chunk_copy(tl, sl, j, second):
        src = pl.multiple_of(j * CHUNK - (HALF_SORT if second else 0), CHUNK)
        dst = pl.multiple_of(dst_ref[tl * MAX_CHUNKS + j], CHUNK)
        half = srt_b if second else srt_a
        return pltpu.make_async_copy(half.at[sl, pl.ds(src, CHUNK)], xs_ref.at[pl.ds(dst, CHUNK)], sems.at[sl])

    srt_a[slot] = sorted_half(0)
    for j in range(HALF_CHUNKS):
        chunk_copy(tile, slot, j, False).start()
    srt_b[slot] = sorted_half(HALF_SORT)

    def issue(j, carry):
        chunk_copy(tile, slot, j, True).start()
        return carry

    lax.fori_loop(HALF_CHUNKS, nch_ref[tile], issue, 0)

    def drain_tile(tl, sl):
        def drain_a(j, carry):
            chunk_copy(tl, sl, j, False).wait()
            return carry

        def drain_b(j, carry):
            chunk_copy(tl, sl, j, True).wait()
            return carry

        lax.fori_loop(0, HALF_CHUNKS, drain_a, 0)
        lax.fori_loop(HALF_CHUNKS, nch_ref[tl], drain_b, 0)

    @pl.when(step > 0)
    def _():
        drain_tile(tile - 1, 1 - slot)

    @pl.when(step == pl.num_programs(0) - 1)
    def _():
        drain_tile(tile, slot)


def _dispatch(nch, dst, h2, idx, rank, seg, xs, t0):
    n, d = h2.shape
    tm = ROW_TILE
    tok = lambda width: pl.BlockSpec((tm, width), lambda i, a, b: (i, 0))
    return pl.pallas_call(
        functools.partial(_dispatch_kernel, tm=tm, t0=t0),
        grid_spec=pltpu.PrefetchScalarGridSpec(
            num_scalar_prefetch=2,
            grid=(n // tm,),
            in_specs=[tok(d), tok(LANES), tok(LANES),
                      pl.BlockSpec((1, 1, LANES), lambda i, a, b: (t0 + i, 0, 0)),
                      pl.BlockSpec(memory_space=pl.ANY)],
            out_specs=pl.BlockSpec(memory_space=pl.ANY),
            scratch_shapes=[pltpu.VMEM((2, HALF_SORT, HALF), U32), pltpu.VMEM((2, HALF_SORT, HALF), U32),
                            pltpu.SemaphoreType.DMA((2,))]),
        out_shape=jax.ShapeDtypeStruct(xs.shape, xs.dtype),
        input_output_aliases={6: 0},
        compiler_params=_params("arbitrary"),
    )(nch, dst, h2, idx, rank, seg, xs)


def _expert_kernel(start_ref, nblk_ref, xs_ref, w1_ref, b1_ref, w2_ref, b2_ref, o_ref,
                   w1b, w2b, xbuf, obuf, sin, sout):
    e = pl.program_id(0)
    n = nblk_ref[e]

    def rows(expert, j):
        return pl.ds(pl.multiple_of(start_ref[expert] + j * MOE_ROWS, MOE_ROWS), MOE_ROWS)

    def in_copy(expert, j, slot):
        return pltpu.make_async_copy(xs_ref.at[rows(expert, j)], xbuf.at[slot], sin.at[slot])

    def out_copy(j, slot):
        return pltpu.make_async_copy(obuf.at[slot], o_ref.at[rows(e, j)], sout.at[slot])

    @pl.when((e == 0) & (n > 0))
    def _():
        in_copy(e, 0, 0).start()

    @pl.when(n > 0)
    def _():
        w1b[...] = w1_ref[0].astype(BF16)
        w2b[...] = w2_ref[0].astype(BF16)

        def block(j, carry):
            slot = j % 2

            @pl.when(j + 1 < n)
            def _():
                in_copy(e, j + 1, 1 - slot).start()

            in_copy(e, j, slot).wait()

            @pl.when(j >= 2)
            def _():
                out_copy(j - 2, slot).wait()

            x_lo, x_hi = _unpack_rows(xbuf[slot])
            gu = (jnp.dot(x_lo, w1b[:HALF, :], preferred_element_type=F32)
                  + jnp.dot(x_hi, w1b[HALF:, :], preferred_element_type=F32)) + b1_ref[0]
            gt = jnp.minimum(gu[:, :D_FF], SWIGLU_LIMIT)
            up = jnp.clip(gu[:, D_FF:], -SWIGLU_LIMIT, SWIGLU_LIMIT)
            hid = (up + 1.0) * gt * (1.0 / (1.0 + jnp.exp(-SWIGLU_ALPHA * gt)))
            out = jnp.dot(hid.astype(BF16), w2b[...], preferred_element_type=F32) + b2_ref[0]
            obuf[slot] = _pack_rows(out.astype(BF16).astype(F32))
            out_copy(j, slot).start()
            return carry

        lax.fori_loop(0, n, block, 0)

        @pl.when(n >= 2)
        def _():
            out_copy(n - 2, n % 2).wait()

        out_copy(n - 1, (n - 1) % 2).wait()

    nxt = jnp.minimum(e + 1, pl.num_programs(0) - 1)

    @pl.when((e + 1 < pl.num_programs(0)) & (nblk_ref[nxt] > 0))
    def _():
        in_copy(nxt, 0, 0).start()


def _experts(row_start, n_blk, xs, w1, b1, w2, b2, layer):
    n_rows, half = xs.shape
    d = 2 * half
    f2 = w1.shape[-1]
    n_exp = w1.shape[1]
    ne = w1.shape[0] * n_exp
    w1 = w1.reshape(ne, d, f2)
    w2 = w2.reshape(ne, D_FF, d)
    pick = lambda e, st, nb: (layer * n_exp + e, 0, 0)
    return pl.pallas_call(
        _expert_kernel,
        grid_spec=pltpu.PrefetchScalarGridSpec(
            num_scalar_prefetch=2,
            grid=(n_exp,),
            in_specs=[pl.BlockSpec(memory_space=pl.ANY),
                      pl.BlockSpec((1, d, f2), pick), pl.BlockSpec((1, 1, f2), pick),
                      pl.BlockSpec((1, D_FF, d), pick), pl.BlockSpec((1, 1, d), pick)],
            out_specs=pl.BlockSpec(memory_space=pl.ANY),
            scratch_shapes=[pltpu.VMEM((d, f2), BF16), pltpu.VMEM((D_FF, d), BF16),
                            pltpu.VMEM((2, MOE_ROWS, half), U32), pltpu.VMEM((2, MOE_ROWS, half), U32),
                            pltpu.SemaphoreType.DMA((2,)), pltpu.SemaphoreType.DMA((2,))]),
        out_shape=jax.ShapeDtypeStruct((n_rows, half), U32),
        input_output_aliases={2: 0},
        compiler_params=_params("arbitrary"),
    )(row_start, n_blk, xs, w1, b1.reshape(ne, 1, f2), w2, b2.reshape(ne, 1, d))


def _combine_kernel(nch_ref, dst_ref, outs_ref, idx_ref, rank_ref, gate_ref, seg_ref, x1_ref, g2_ref, fg_ref,
                    o_ref, buf, sems, *, tm, nt, t0, final):
    step = pl.program_id(0) * nt + pl.program_id(1)
    tile = t0 + step
    slot = step % 2

    def chunk_copy(tl, sl, j):
        src = pl.multiple_of(dst_ref[tl * MAX_CHUNKS + j], CHUNK)
        dst = pl.multiple_of(j * CHUNK, CHUNK)
        return pltpu.make_async_copy(outs_ref.at[pl.ds(src, CHUNK)], buf.at[sl, pl.ds(dst, CHUNK)], sems.at[sl])

    def fetch(tl, sl, first):
        def issue(j, carry):
            chunk_copy(tl, sl, j).start()
            return carry
        lax.fori_loop(first, nch_ref[tl], issue, 0)

    @pl.when(step == 0)
    def _():
        buf[...] = jnp.zeros_like(buf)
        fetch(tile, slot, 0)

    last = pl.num_programs(0) * nt - 1
    nxt = t0 + jnp.minimum(step + 1, last)
    for j in range(MIN_CHUNKS):
        chunk_copy(nxt, 1 - slot, j).start()

    def drain(j, carry):
        chunk_copy(tile, slot, j).wait()
        return carry

    n = nch_ref[tile]
    idx, rank, gate, seg = idx_ref[0], rank_ref[0], gate_ref[0], seg_ref[0]
    col = lax.broadcasted_iota(I32, (tm, SORT_ROWS), 1).astype(F32)
    weights = jnp.zeros((tm, SORT_ROWS), F32)
    for k in range(TOP_K):
        pos = jnp.sum(_slot_values(idx, rank, seg, k), axis=-1, keepdims=True)
        weights = jnp.where(col == pos, gate[:, k:k + 1], weights)
    wh, wl = _split(weights)
    fetch(nxt, 1 - slot, MIN_CHUNKS)
    lax.fori_loop(0, n, drain, 0)
    y = jnp.concatenate(
        [jnp.dot(wh, rows, preferred_element_type=F32) + jnp.dot(wl, rows, preferred_element_type=F32)
         for rows in _unpack_rows(buf[slot])], axis=1)
    x2 = x1_ref[0] + g2_ref[0] * y
    if final:
        x2 = _rms(x2) * fg_ref[...]
    o_ref[0] = x2

    @pl.when(step == last)
    def _():
        def drain_extra(j, carry):
            chunk_copy(nxt, 1 - slot, j).wait()
            return carry
        lax.fori_loop(0, nch_ref[nxt], drain_extra, 0)


def _combine(nch, dst, outs, idx, rank, gate, seg, x1, g2, fg, t0, final):
    b, l, d = x1.shape
    tm = ROW_TILE
    nt = l // tm
    per_batch = g2.shape[0] > 1
    bidx = (lambda i, t, a, c: (i, 0, 0)) if per_batch else (lambda i, t, a, c: (0, 0, 0))
    tok = lambda width: pl.BlockSpec((1, tm, width), lambda i, t, a, c: (i, t, 0))
    return pl.pallas_call(
        functools.partial(_combine_kernel, tm=tm, nt=nt, t0=t0, final=final),
        grid_spec=pltpu.PrefetchScalarGridSpec(
            num_scalar_prefetch=2,
            grid=(b, nt),
            in_specs=[pl.BlockSpec(memory_space=pl.ANY), tok(LANES), tok(LANES), tok(LANES),
                      pl.BlockSpec((1, 1, LANES), lambda i, t, a, c: (t0 + i * nt + t, 0, 0)),
                      tok(d), pl.BlockSpec((1, 1, d), bidx),
                      pl.BlockSpec((1, d), lambda i, t, a, c: (0, 0))],
            out_specs=tok(d),
            scratch_shapes=[pltpu.VMEM((2, SORT_ROWS, HALF), U32), pltpu.SemaphoreType.DMA((2,))]),
        out_shape=jax.ShapeDtypeStruct((b, l, d), F32),
        compiler_params=_params("arbitrary", "arbitrary"),
    )(nch, dst, outs, idx, rank, gate, seg, x1, g2, fg)


def _layer_weights(w_in_l):
    cuts = np.cumsum([A_WIDTH, A_WIDTH, A_WIDTH, B_WIDTH, C_WIDTH, C_KV * C_HD]).tolist()
    front = w_in_l[:, :cuts[4]]
    kc = w_in_l[:, cuts[4]:cuts[5]]
    vc = w_in_l[:, cuts[5]:]

    def rep(w):
        return jnp.concatenate([w[:, C_HD * (j // C_GROUP):C_HD * (j // C_GROUP + 1)] for j in range(C_HEADS)], axis=1)

    lat = jnp.concatenate([front, rep(kc), rep(vc)], axis=1).astype(BF16)
    ctx = jnp.concatenate([front, rep(kc), rep(vc), kc, vc], axis=1).astype(BF16)
    return ctx, lat


def _rep_heads(a):
    return jnp.repeat(a, C_GROUP, axis=2).reshape(a.shape[0], a.shape[1], C_HEADS * C_HD)


def kernel(x_prompt, x_sample, c, cache_diff_k, cache_diff_v, cache_win_k, cache_win_v, c_ctx, norm1_g, norm2_g, w_mod, b_mod, w_in, diff_lambda, diff_subln_g, w_pool, pool_scale, sink, w_out, w_router, b_router, w1, b1, w2, b2, final_g):
    depth = w_in.shape[0]
    bc, lc, d = x_prompt.shape
    bl, ll, _ = x_sample.shape
    n_ctx, n_lat = bc * lc, bl * ll
    n_tok = n_ctx + n_lat
    n_tiles = n_tok // ROW_TILE
    n_blocks = -(-(n_tok * TOP_K + n_tiles * N_EXPERTS * (CHUNK - 1)) // MOE_ROWS) + N_EXPERTS
    n_rows = n_blocks * MOE_ROWS

    mod_rows = -(-(1 + bl) // 8) * 8
    cmat = jnp.zeros((mod_rows, d), F32).at[0].set(c_ctx).at[1:1 + bl].set(c)
    mod = _mod_vectors(cmat, w_mod, b_mod)
    tabs_a = _rope_tables(ll, A_HD, A_WIDTH)
    tabs_c = _rope_tables(ll, C_HD, C_WIDTH)
    tables = (tabs_a[0], tabs_a[1], tabs_c[0], tabs_c[1])
    fg = final_g.reshape(1, d)

    xp, xs_lat = x_prompt, x_sample
    new_cache = [[], [], [], []]
    for i in range(depth):
        lam_init = 0.8 - 0.6 * math.exp(-0.3 * i)
        mv = lambda rows, j: mod[i, rows, j * d:(j + 1) * d].reshape(-1, 1, d)
        ctx_rows, lat_rows = slice(0, 1), slice(1, 1 + bl)
        w_ctx, w_lat = _layer_weights(w_in[i])
        n1 = norm1_g[i].reshape(1, d)
        n2 = norm2_g[i].reshape(1, d)
        wbd = jax.scipy.linalg.block_diag(*[w_pool[i, g] for g in range(B_GROUPS)]).astype(BF16)
        ps = pool_scale[i].reshape(1, B_WIDTH)
        g_tiled = jnp.tile(diff_subln_g[i], A_HEADS).reshape(1, A_WIDTH)
        wo = w_out[i].astype(BF16)
        wr = jnp.zeros((d, LANES), F32).at[:, :N_EXPERTS].set(w_router[i])
        br = jnp.zeros((1, LANES), F32).at[0, :N_EXPERTS].set(b_router[i])

        qa, ka, va, u, qc, kr, vr, ka32, va32, kc32, vc32 = _inproj(
            xp, mv(ctx_rows, 0), mv(ctx_rows, 1), n1, w_ctx, None)
        new_cache[0].append(ka32.reshape(bc, lc, 2 * A_HEADS, A_HD))
        new_cache[1].append(va32.reshape(bc, lc, A_HEADS, 2 * A_HD))
        new_cache[2].append(kc32.reshape(bc, lc, C_KV, C_HD))
        new_cache[3].append(vc32.reshape(bc, lc, C_KV, C_HD))
        ob = _pool(u, wbd, ps)
        oa = _diff_attn(qa, [(ka, va)], diff_lambda[i], g_tiled, lam_init)
        oc = _gqa(sink[i], qc, kr, vr)
        x1_c, h2_c, idx_c, gate_c, rank_c, cnt_c = _finish(
            xp, oa, ob, oc, wo, mv(ctx_rows, 2), mv(ctx_rows, 3), mv(ctx_rows, 4), n2, wr, br)

        qa, ka, va, u, qc, kr, vr = _inproj(xs_lat, mv(lat_rows, 0), mv(lat_rows, 1), n1, w_lat, tables)
        ob = _pool(u, wbd, ps)
        dk = cache_diff_k[:, i].reshape(bl, -1, A_WIDTH).astype(BF16)
        dv = cache_diff_v[:, i].reshape(bl, -1, A_WIDTH).astype(BF16)
        oa = _diff_attn(qa, [(ka, va), (dk, dv)], diff_lambda[i], g_tiled, lam_init)
        wk = _rep_heads(cache_win_k[:, i]).astype(BF16)
        wv = _rep_heads(cache_win_v[:, i]).astype(BF16)
        oc = _gqa(sink[i], qc, kr, vr, (wk, wv))
        x1_l, h2_l, idx_l, gate_l, rank_l, cnt_l = _finish(
            xs_lat, oa, ob, oc, wo, mv(lat_rows, 2), mv(lat_rows, 3), mv(lat_rows, 4), n2, wr, br)

        cnt = jnp.concatenate([cnt_c, cnt_l], axis=0)[:, 0, :N_EXPERTS].astype(I32)
        c8 = (cnt + CHUNK - 1) // CHUNK * CHUNK
        seg_end = jnp.cumsum(c8, axis=1)
        seg = seg_end - c8
        padded = (jnp.sum(c8, axis=0) + MOE_ROWS - 1) // MOE_ROWS * MOE_ROWS
        pad_end = jnp.cumsum(padded)
        gbase = (pad_end - padded)[None, :] + jnp.cumsum(c8, axis=0) - c8
        nch = seg_end[:, -1] // CHUNK
        j8 = jnp.arange(MAX_CHUNKS, dtype=I32) * CHUNK
        chunk_e = jnp.minimum(jnp.sum((seg_end[:, None, :] <= j8[None, :, None]).astype(I32), axis=-1),
                              N_EXPERTS - 1)
        onehot = chunk_e[..., None] == jnp.arange(N_EXPERTS, dtype=I32)
        dst = (jnp.sum(jnp.where(onehot, (gbase - seg)[:, None, :], 0), axis=-1) + j8[None, :]).reshape(-1)
        seg_f = jnp.zeros((cnt.shape[0], 1, LANES), F32).at[:, 0, :N_EXPERTS].set(seg.astype(F32))
        tiles_c = cnt_c.shape[0]

        rows_in = jnp.zeros((n_rows, HALF), U32)
        rows_in = _dispatch(nch, dst, h2_c.reshape(n_ctx, d), idx_c.reshape(n_ctx, LANES),
                            rank_c.reshape(n_ctx, LANES), seg_f, rows_in, 0)
        rows_in = _dispatch(nch, dst, h2_l.reshape(n_lat, d), idx_l.reshape(n_lat, LANES),
                            rank_l.reshape(n_lat, LANES), seg_f, rows_in, tiles_c)
        rows_out = _experts(pad_end - padded, padded // MOE_ROWS, rows_in, w1, b1, w2, b2, i)
        final = i == depth - 1
        xp = _combine(nch, dst, rows_out, idx_c, rank_c, gate_c, seg_f, x1_c, mv(ctx_rows, 5), fg, 0, final)
        xs_lat = _combine(nch, dst, rows_out, idx_l, rank_l, gate_l, seg_f, x1_l, mv(lat_rows, 5), fg,
                          tiles_c, final)

    return (xp, xs_lat) + tuple(jnp.stack(parts, axis=1) for parts in new_cache)
```

```python
import functools
import math

import numpy as np
import jax
import jax.numpy as jnp
from jax import lax
from jax.experimental import pallas as pl
from jax.experimental.pallas import tpu as pltpu

F32 = jnp.float32
BF16 = jnp.bfloat16
I32 = jnp.int32
U32 = jnp.uint32

D_MODEL = 1024
GRID_W = 64
ROPE_BASE = 10000.0
NORM_EPS = 1e-6
A_HD = 32
A_HEADS = 4
A_WIDTH = 256
B_WIDTH = 256
B_GROUPS = 4
B_GC = 64
C_HD = 64
C_HEADS = 8
C_KV = 2
C_GROUP = 4
C_WIDTH = 512
WINDOW = 128
QBLOCK = 128
N_EXPERTS = 32
TOP_K = 4
D_FF = 1024
SWIGLU_LIMIT = 7.0
SWIGLU_ALPHA = 1.702

LANES = 128
ROW_TILE = 256
MOE_ROWS = 512
KEY_CHUNK = 512
NEG = -1e30
LOG2E = math.log2(math.e)
VMEM_LIMIT = 56 * 1024 * 1024


def _params(*sem):
    return pltpu.CompilerParams(dimension_semantics=sem, vmem_limit_bytes=VMEM_LIMIT)


def _split(x):
    hi = x.astype(BF16)
    lo = (x - hi.astype(F32)).astype(BF16)
    return hi, lo


def _dot3(a, b):
    ah, al = _split(a)
    bh, bl = _split(b)
    return (jnp.dot(ah, bh, preferred_element_type=F32)
            + (jnp.dot(ah, bl, preferred_element_type=F32)
               + jnp.dot(al, bh, preferred_element_type=F32)))


def _nt_dot(a, b):
    return lax.dot_general(a, b, (((1,), (1,)), ((), ())), preferred_element_type=F32)


def _rms(x):
    return x * lax.rsqrt(jnp.mean(x * x, axis=-1, keepdims=True) + NORM_EPS)


HALF = D_MODEL // 2
HIGH_BITS = 0xFFFF0000


def _pack_rows(x):
    lo = lax.bitcast_convert_type(x[:, :HALF], U32) >> 16
    hi = lax.bitcast_convert_type(x[:, HALF:], U32) & jnp.uint32(HIGH_BITS)
    return lo | hi


def _unpack_rows(w):
    lo = lax.bitcast_convert_type(w << 16, F32).astype(BF16)
    hi = lax.bitcast_convert_type(w & jnp.uint32(HIGH_BITS), F32).astype(BF16)
    return lo, hi


def _mod_kernel(c_ref, w_ref, b_ref, o_ref):
    c = c_ref[...]
    a = c * (1.0 / (1.0 + jnp.exp(-c)))
    o_ref[0] = _dot3(a, w_ref[0]) + b_ref[0]


def _mod_vectors(cmat, w_mod, b_mod):
    depth, d, e = w_mod.shape
    rows = cmat.shape[0]
    tn = 512
    return pl.pallas_call(
        _mod_kernel,
        grid=(depth, e // tn),
        in_specs=[pl.BlockSpec((rows, d), lambda l, j: (0, 0)),
                  pl.BlockSpec((1, d, tn), lambda l, j: (l, 0, j)),
                  pl.BlockSpec((1, 1, tn), lambda l, j: (l, 0, j))],
        out_specs=pl.BlockSpec((1, rows, tn), lambda l, j: (l, 0, j)),
        out_shape=jax.ShapeDtypeStruct((depth, rows, e), F32),
        compiler_params=_params("arbitrary", "arbitrary"),
    )(cmat, w_mod, b_mod.reshape(depth, 1, e))


def _rope(z, col_ref, row_ref, nf, r0):
    tm, w = z.shape
    outs = []
    for g in range(tm // GRID_W):
        zs = z[GRID_W * g:GRID_W * (g + 1), :]
        c = col_ref[0] + row_ref[0, pl.ds(r0 + g, 1), :]
        sm = col_ref[1] + row_ref[1, pl.ds(r0 + g, 1), :]
        sp = col_ref[2] + row_ref[2, pl.ds(r0 + g, 1), :]
        outs.append(zs * c + pltpu.roll(zs, w - nf, 1) * sm + pltpu.roll(zs, nf, 1) * sp)
    return jnp.concatenate(outs, axis=0)


def _inproj_kernel(*refs, rope, tm):
    x_ref, sh_ref, sc_ref, n1_ref, w_ref = refs[:5]
    if rope:
        ta_col, ta_row, tc_col, tc_row = refs[5:9]
        qa_o, ka_o, va_o, u_o, qc_o, kr_o, vr_o = refs[9:]
    else:
        qa_o, ka_o, va_o, u_o, qc_o, kr_o, vr_o, ka32_o, va32_o, kc32_o, vc32_o = refs[5:]
    x = x_ref[0]
    h = (_rms(x) * n1_ref[...]) * (1.0 + sc_ref[0]) + sh_ref[0]
    hb = h.astype(BF16)

    def seg(a, b):
        return jnp.dot(hb, w_ref[:, a:b], preferred_element_type=F32)

    qa, ka, va, u = seg(0, 256), seg(256, 512), seg(512, 768), seg(768, 1024)
    qc, kr, vr = seg(1024, 1536), seg(1536, 2048), seg(2048, 2560)
    if rope:
        r0 = pl.program_id(1) * (tm // GRID_W)
        qa = _rope(qa, ta_col, ta_row, A_HD // 4, r0)
        ka = _rope(ka, ta_col, ta_row, A_HD // 4, r0)
        qc = _rope(qc, tc_col, tc_row, C_HD // 4, r0)
        kr = _rope(kr, tc_col, tc_row, C_HD // 4, r0)
    else:
        ka32_o[0] = ka
        va32_o[0] = va
        kc32_o[0] = seg(2560, 2688)
        vc32_o[0] = seg(2688, 2816)
    qa_o[0] = (qa * (A_HD ** -0.5 * LOG2E)).astype(BF16)
    ka_o[0] = ka.astype(BF16)
    va_o[0] = va.astype(BF16)
    u_o[0] = u
    qc_o[0] = (qc * (C_HD ** -0.5 * LOG2E)).astype(BF16)
    kr_o[0] = kr.astype(BF16)
    vr_o[0] = vr.astype(BF16)


def _inproj(x, sh, sc, n1, w, tables):
    b, l, d = x.shape
    tm = min(ROW_TILE, l)
    rope = tables is not None
    per_batch = sh.shape[0] > 1
    bidx = (lambda i, t: (i, 0, 0)) if per_batch else (lambda i, t: (0, 0, 0))
    tok = lambda width: pl.BlockSpec((1, tm, width), lambda i, t: (i, t, 0))
    in_specs = [tok(d),
                pl.BlockSpec((1, 1, d), bidx), pl.BlockSpec((1, 1, d), bidx),
                pl.BlockSpec((1, d), lambda i, t: (0, 0)),
                pl.BlockSpec(w.shape, lambda i, t: (0, 0))]
    args = [x, sh, sc, n1, w]
    widths = [(256, BF16), (256, BF16), (256, BF16), (256, F32), (512, BF16), (512, BF16), (512, BF16)]
    if rope:
        for tab in tables:
            in_specs.append(pl.BlockSpec(tab.shape, lambda i, t: (0, 0, 0)))
            args.append(tab)
    else:
        widths += [(256, F32), (256, F32), (128, F32), (128, F32)]
    return pl.pallas_call(
        functools.partial(_inproj_kernel, rope=rope, tm=tm),
        grid=(b, l // tm),
        in_specs=in_specs,
        out_specs=[tok(wd) for wd, _ in widths],
        out_shape=[jax.ShapeDtypeStruct((b, l, wd), dt) for wd, dt in widths],
        compiler_params=_params("arbitrary", "arbitrary"),
    )(*args)


def _rope_tables(n_lat, head_dim, width):
    rows = n_lat // GRID_W
    nf = head_dim // 4
    inv = ROPE_BASE ** (-jnp.arange(nf, dtype=F32) / nf)
    lane = np.arange(width) % head_dim
    half = lane // (2 * nf)
    pair = (lane // nf) % 2
    f = lane % nf

    def part(pos, which):
        ang = pos[:, None] * inv[f][None, :]
        on = jnp.asarray(half == which, F32)[None, :]
        c = jnp.cos(ang) * on
        s = jnp.sin(ang) * on
        sm = -s * jnp.asarray(pair == 0, F32)[None, :]
        sp = s * jnp.asarray(pair == 1, F32)[None, :]
        return jnp.stack([c, sm, sp])

    return part(jnp.arange(GRID_W, dtype=F32), 1), part(jnp.arange(rows, dtype=F32), 0)


POOL_PAD = 8
POOL_CHUNK = 256


def _pool_kernel(u_ref, w_ref, ps_ref, o_ref, pad_ref, *, l):
    zeros = jnp.zeros((POOL_PAD, B_WIDTH), F32)
    pad_ref[0:POOL_PAD, :] = zeros
    pad_ref[POOL_PAD + l:2 * POOL_PAD + l, :] = zeros
    pad_ref[POOL_PAD:POOL_PAD + l, :] = u_ref[0]
    ch = min(POOL_CHUNK, l)
    lane = lax.broadcasted_iota(I32, (ch, B_WIDTH), 1)
    grp = lane >> 6
    half = jnp.where(grp == 0, 1, jnp.where(grp == 1, 2, jnp.where(grp == 2, 4, 8)))
    row = lax.broadcasted_iota(I32, (ch, B_WIDTH), 0)
    for c in range(0, l, ch):
        ld = lambda k: pad_ref[c + POOL_PAD + k:c + POOL_PAD + k + ch, :]
        cur = ld(0)
        s2 = ld(-1) + cur
        s4 = s2 + (ld(-2) + ld(1))
        s8 = s4 + ((ld(-4) + ld(-3)) + (ld(2) + ld(3)))
        s16 = s8 + (((ld(-8) + ld(-7)) + (ld(-6) + ld(-5))) + ((ld(4) + ld(5)) + (ld(6) + ld(7))))
        win = jnp.where(grp == 0, s2, jnp.where(grp == 1, s4, jnp.where(grp == 2, s8, s16)))
        t = row + c
        cnt = (jnp.minimum(t + half, l) - jnp.maximum(t - half, 0)).astype(F32)
        r = (win / cnt - cur).astype(BF16)
        y = jnp.dot(r, w_ref[...], preferred_element_type=F32) * ps_ref[...]
        o_ref[0, c:c + ch, :] = y.astype(BF16)


def _pool(u, wbd, ps):
    b, l, w = u.shape
    return pl.pallas_call(
        functools.partial(_pool_kernel, l=l),
        grid=(b,),
        in_specs=[pl.BlockSpec((1, l, w), lambda i: (i, 0, 0)),
                  pl.BlockSpec((w, w), lambda i: (0, 0)),
                  pl.BlockSpec((1, w), lambda i: (0, 0))],
        out_specs=pl.BlockSpec((1, l, w), lambda i: (i, 0, 0)),
        out_shape=jax.ShapeDtypeStruct((b, l, w), BF16),
        scratch_shapes=[pltpu.VMEM((l + 2 * POOL_PAD, w), F32)],
        compiler_params=_params("arbitrary"),
    )(u, wbd, ps)


def _diff_attn_kernel(*refs, n_src, tq, lam_init):
    q_ref = refs[0]
    srcs = [(refs[1 + 2 * i], refs[2 + 2 * i]) for i in range(n_src)]
    lamp_ref, g_ref, o_ref, s_scr, p_scr = refs[1 + 2 * n_src:]
    chunks = []
    col = 0
    for k_ref, _ in srcs:
        keys = k_ref.shape[1]
        tk = min(keys, KEY_CHUNK)
        for c in range(0, keys, tk):
            chunks.append((k_ref, c, tk, col))
            col += tk
    q32 = q_ref[0].astype(F32)
    lp = lamp_ref[...]
    lam = (jnp.exp(jnp.sum(lp[0:1] * lp[1:2], axis=-1, keepdims=True))
           - jnp.exp(jnp.sum(lp[2:3] * lp[3:4], axis=-1, keepdims=True)) + lam_init)
    lane_row = lax.broadcasted_iota(I32, (1, A_WIDTH), 1)
    lane = lax.broadcasted_iota(I32, (tq, A_WIDTH), 1)

    def scores(j):
        qm = (q32 * ((lane_row >> 5) == j).astype(F32)).astype(BF16)
        part = jnp.full((tq, LANES), NEG, F32)
        for k_ref, c, tk, col in chunks:
            s = _nt_dot(qm, k_ref[0, c:c + tk, :])
            s_scr[j % 2, :, col:col + tk] = s
            for i in range(0, tk, LANES):
                part = jnp.maximum(part, s[:, i:i + LANES])
        return jnp.max(part, axis=-1, keepdims=True)

    def weights(j, row_max):
        part = jnp.zeros((tq, LANES), F32)
        for _, _, tk, col in chunks:
            p = jnp.exp2(s_scr[j % 2, :, col:col + tk] - row_max)
            for i in range(0, tk, LANES):
                part = part + p[:, i:i + LANES]
            p_scr[j % 2, :, col:col + tk] = p.astype(BF16)
        return jnp.sum(part, axis=-1, keepdims=True)

    def values(j, den):
        acc, col = None, 0
        for _, v_ref in srcs:
            keys = v_ref.shape[1]
            part = jnp.dot(p_scr[j % 2, :, col:col + keys], v_ref[0], preferred_element_type=F32)
            acc = part if acc is None else acc + part
            col += keys
        return acc / den

    out = jnp.zeros((tq, A_WIDTH), F32)
    row_max = scores(0)
    for j in range(2 * A_HEADS):
        den = weights(j, row_max)
        if j + 1 < 2 * A_HEADS:
            row_max = scores(j + 1)
        if j % 2 == 0:
            first = values(j, den)
        else:
            out = out + jnp.where((lane >> 6) == j // 2, first - lam * values(j, den), 0.0)
    sq = out * out
    rs = jnp.zeros((tq, A_WIDTH), F32)
    for h in range(A_HEADS):
        msk = (lane >> 6) == h
        ms = jnp.sum(jnp.where(msk, sq, 0.0), axis=-1, keepdims=True) * (1.0 / (2 * A_HD))
        rs = rs + jnp.where(msk, lax.rsqrt(ms + NORM_EPS), 0.0)
    o_ref[0] = (((out * rs) * g_ref[...]) * (1.0 - lam_init)).astype(BF16)


def _diff_attn(q, srcs, lam_p, g_tiled, lam_init):
    b, l, w = q.shape
    s_len = sum(k.shape[1] for k, _ in srcs)
    tq = min(ROW_TILE, l)
    in_specs = [pl.BlockSpec((1, tq, w), lambda i, t: (i, t, 0))]
    args = [q]
    for k, v in srcs:
        for a in (k, v):
            in_specs.append(pl.BlockSpec((1,) + a.shape[1:], lambda i, t: (i, 0, 0)))
            args.append(a)
    in_specs += [pl.BlockSpec(lam_p.shape, lambda i, t: (0, 0)),
                 pl.BlockSpec(g_tiled.shape, lambda i, t: (0, 0))]
    return pl.pallas_call(
        functools.partial(_diff_attn_kernel, n_src=len(srcs), tq=tq, lam_init=lam_init),
        grid=(b, l // tq),
        in_specs=in_specs,
        out_specs=pl.BlockSpec((1, tq, w), lambda i, t: (i, t, 0)),
        out_shape=jax.ShapeDtypeStruct((b, l, w), BF16),
        scratch_shapes=[pltpu.VMEM((2, tq, s_len), F32), pltpu.VMEM((2, tq, s_len), BF16)],
        compiler_params=_params("arbitrary", "arbitrary"),
    )(*args, lam_p, g_tiled)


GQA_SLAB = C_GROUP * C_HD


def _gqa_kernel(*refs, windowed, tq, l):
    if windowed:
        sink_ref, q_ref, k_ref, v_ref, kc_ref, vc_ref, o_ref, s_scr, p_scr = refs
    else:
        sink_ref, q_ref, k_ref, v_ref, o_ref, s_scr, p_scr = refs
    i = pl.program_id(1)
    rows = C_GROUP * tq
    shift = int(math.log2(tq))
    q32 = q_ref[0].astype(F32)
    lane_row = lax.broadcasted_iota(I32, (1, GQA_SLAB), 1)
    lane = lax.broadcasted_iota(I32, (tq, GQA_SLAB), 1)
    rid = lax.broadcasted_iota(I32, (rows, 1), 0)
    if windowed:
        ws = pl.multiple_of(jnp.clip((i - 1) * tq, 0, l - 3 * tq), tq)
        qpos = i * tq + (rid & (tq - 1))
        kpos = ws + lax.broadcasted_iota(I32, (1, 3 * tq), 1)
        band = jnp.where(jnp.abs(kpos - qpos) <= WINDOW, 0.0, NEG)
    def slab(g):
        return slice(GQA_SLAB * g, GQA_SLAB * (g + 1))

    def key_sources(g):
        if windowed:
            return [(lambda: k_ref[0, pl.ds(ws, 3 * tq), slab(g)], lambda: v_ref[0, pl.ds(ws, 3 * tq), slab(g)],
                     band, 3 * tq),
                    (lambda: kc_ref[0, :, slab(g)], lambda: vc_ref[0, :, slab(g)], None, kc_ref.shape[1])]
        return [(lambda: k_ref[0, :, slab(g)], lambda: v_ref[0, :, slab(g)], None, l)]

    def sink_col(g):
        sk = jnp.zeros((rows, 1), F32)
        for hh in range(C_GROUP):
            sk = jnp.where((rid >> shift) == hh, sink_ref[C_GROUP * g + hh] * LOG2E, sk)
        return sk

    def scores(g):
        qg = q32[:, slab(g)]
        qs = jnp.concatenate(
            [(qg * ((lane_row >> 6) == hh).astype(F32)).astype(BF16) for hh in range(C_GROUP)], axis=0)
        part = jnp.full((rows, LANES), NEG, F32)
        col = 0
        for keys, _, mask, n in key_sources(g):
            s = _nt_dot(qs, keys())
            if mask is not None:
                s = s + mask
            s_scr[g, :, col:col + n] = s
            for c in range(0, n, LANES):
                part = jnp.maximum(part, s[:, c:c + LANES])
            col += n
        return jnp.maximum(sink_col(g), jnp.max(part, axis=-1, keepdims=True))

    def weights(g, m):
        part = jnp.zeros((rows, LANES), F32)
        col = 0
        for _, _, _, n in key_sources(g):
            p = jnp.exp2(s_scr[g, :, col:col + n] - m)
            for c in range(0, n, LANES):
                part = part + p[:, c:c + LANES]
            p_scr[g, :, col:col + n] = p.astype(BF16)
            col += n
        return jnp.exp2(sink_col(g) - m) + jnp.sum(part, axis=-1, keepdims=True)

    def values(g, den):
        pv, col = None, 0
        for _, vals, _, n in key_sources(g):
            part = jnp.dot(p_scr[g, :, col:col + n], vals(), preferred_element_type=F32)
            pv = part if pv is None else pv + part
            col += n
        o = pv / den
        og = jnp.zeros((tq, GQA_SLAB), F32)
        for hh in range(C_GROUP):
            og = og + jnp.where((lane >> 6) == hh, o[hh * tq:(hh + 1) * tq, :], 0.0)
        o_ref[0, :, slab(g)] = og.astype(BF16)

    maxes = [scores(g) for g in range(C_KV)]
    dens = [weights(g, maxes[g]) for g in range(C_KV)]
    for g in range(C_KV):
        values(g, dens[g])


def _gqa(sink, q, k, v, cache=None):
    b, l, w = q.shape
    windowed = cache is not None
    tq = QBLOCK
    whole = lambda a: pl.BlockSpec((1,) + a.shape[1:], lambda i, t: (i, 0, 0))
    in_specs = [pl.BlockSpec(memory_space=pltpu.SMEM),
                pl.BlockSpec((1, tq, w), lambda i, t: (i, t, 0)), whole(k), whole(v)]
    args = [sink, q, k, v]
    n_keys = l
    if windowed:
        in_specs += [whole(cache[0]), whole(cache[1])]
        args += list(cache)
        n_keys = 3 * tq + cache[0].shape[1]
    return pl.pallas_call(
        functools.partial(_gqa_kernel, windowed=windowed, tq=tq, l=l),
        grid=(b, l // tq),
        in_specs=in_specs,
        out_specs=pl.BlockSpec((1, tq, w), lambda i, t: (i, t, 0)),
        out_shape=jax.ShapeDtypeStruct((b, l, w), BF16),
        scratch_shapes=[pltpu.VMEM((C_KV, C_GROUP * tq, n_keys), F32),
                        pltpu.VMEM((C_KV, C_GROUP * tq, n_keys), BF16)],
        compiler_params=_params("arbitrary", "arbitrary"),
    )(*args)


def _finish_kernel(x_ref, oa_ref, ob_ref, oc_ref, wo_ref, g1_ref, sh_ref, sc_ref, n2_ref, wr_ref, br_ref,
                   x1_o, h2_o, idx_o, gate_o, rank_o, cnt_o, *, tm):
    y = (jnp.dot(oa_ref[0], wo_ref[0:256, :], preferred_element_type=F32)
         + jnp.dot(ob_ref[0], wo_ref[256:512, :], preferred_element_type=F32)
         + jnp.dot(oc_ref[0], wo_ref[512:1024, :], preferred_element_type=F32))
    x1 = x_ref[0] + g1_ref[0] * y
    x1_o[0] = x1
    h2 = (_rms(x1) * n2_ref[...]) * (1.0 + sc_ref[0]) + sh_ref[0]
    h2_o[0] = h2
    lane = lax.broadcasted_iota(I32, (tm, LANES), 1)
    lanef = lane.astype(F32)
    logits = jnp.where(lane < N_EXPERTS, _dot3(h2, wr_ref[...]) + br_ref[...], NEG)
    work = logits
    tops, sels = [], []
    for k in range(TOP_K):
        mk = jnp.max(work, axis=-1, keepdims=True)
        ik = jnp.min(jnp.where(work == mk, lanef, float(LANES)), axis=-1, keepdims=True)
        sel = lanef == ik
        work = jnp.where(sel, 2.0 * NEG, work)
        tops.append((mk, ik))
        sels.append(sel)
    es = [jnp.exp(mk - tops[0][0]) for mk, _ in tops]
    den = (es[0] + es[1]) + (es[2] + es[3])
    multi = jnp.zeros((tm, LANES), F32)
    for sel in sels:
        multi = multi + sel.astype(F32)
    r_i = lax.broadcasted_iota(I32, (tm, tm), 0)
    c_i = lax.broadcasted_iota(I32, (tm, tm), 1)
    before = (c_i < r_i).astype(BF16)
    prior = jnp.dot(before, multi.astype(BF16), preferred_element_type=F32)
    idx = jnp.zeros((tm, LANES), F32)
    gate = jnp.zeros((tm, LANES), F32)
    rank = jnp.zeros((tm, LANES), F32)
    for k in range(TOP_K):
        slot = lane == k
        idx = jnp.where(slot, tops[k][1], idx)
        gate = jnp.where(slot, es[k] / den, gate)
        rk = jnp.sum(jnp.where(sels[k], prior, 0.0), axis=-1, keepdims=True)
        rank = jnp.where(slot, rk, rank)
    idx_o[0] = idx.astype(I32)
    gate_o[0] = gate
    rank_o[0] = rank.astype(I32)
    cnt_o[0] = jnp.sum(multi, axis=0, keepdims=True)


def _finish(x, oa, ob, oc, wo, g1, sh2, sc2, n2, wr, br):
    b, l, d = x.shape
    tm = min(ROW_TILE, l)
    nt = l // tm
    per_batch = g1.shape[0] > 1
    bidx = (lambda i, t: (i, 0, 0)) if per_batch else (lambda i, t: (0, 0, 0))
    tok = lambda width: pl.BlockSpec((1, tm, width), lambda i, t: (i, t, 0))
    const2 = lambda a: pl.BlockSpec(a.shape, lambda i, t: (0, 0))
    vec = pl.BlockSpec((1, 1, d), bidx)
    return pl.pallas_call(
        functools.partial(_finish_kernel, tm=tm),
        grid=(b, l // tm),
        in_specs=[tok(d), tok(256), tok(256), tok(512), const2(wo), vec, vec, vec,
                  const2(n2), const2(wr), const2(br)],
        out_specs=[tok(d), tok(d), tok(LANES), tok(LANES), tok(LANES),
                   pl.BlockSpec((1, 1, LANES), lambda i, t: (i * nt + t, 0, 0))],
        out_shape=[jax.ShapeDtypeStruct((b, l, d), F32), jax.ShapeDtypeStruct((b, l, d), F32),
                   jax.ShapeDtypeStruct((b, l, LANES), I32), jax.ShapeDtypeStruct((b, l, LANES), F32),
                   jax.ShapeDtypeStruct((b, l, LANES), I32), jax.ShapeDtypeStruct((b * nt, 1, LANES), F32)],
        compiler_params=_params("arbitrary", "arbitrary"),
    )(x, oa, ob, oc, wo, g1, sh2, sc2, n2, wr, br)


CHUNK = 8
SORT_ROWS = ROW_TILE * TOP_K + N_EXPERTS * CHUNK
MAX_CHUNKS = SORT_ROWS // CHUNK
MIN_CHUNKS = ROW_TILE * TOP_K // CHUNK
HALF_SORT = SORT_ROWS // 2
HALF_CHUNKS = HALF_SORT // CHUNK
assert HALF_CHUNKS <= MIN_CHUNKS


def _slot_values(idx, rank, seg, k):
    lane = lax.broadcasted_iota(I32, idx.shape, 1)
    return jnp.where(lane == idx[:, k:k + 1], seg + rank[:, k:k + 1].astype(F32), 0.0)


def _dispatch_kernel(nch_ref, dst_ref, h_ref, idx_ref, rank_ref, seg_ref, xs_in_ref, xs_ref, srt_a, srt_b, sems,
                     *, tm, t0):
    del xs_in_ref
    step = pl.program_id(0)
    tile = t0 + step
    slot = step % 2
    idx, rank, seg = idx_ref[...], rank_ref[...], seg_ref[0]
    ones = jnp.ones((CHUNK, LANES), BF16)
    pos = []
    for k in range(TOP_K):
        hi, lo = _split(_slot_values(idx, rank, seg, k))
        pos.append((_nt_dot(ones, hi) + _nt_dot(ones, lo))[0:1, :])
    hb = h_ref[...].astype(BF16)

    def sorted_half(first_row):
        row = (lax.broadcasted_iota(I32, (HALF_SORT, tm), 0) + first_row).astype(F32)
        perm = jnp.zeros((HALF_SORT, tm), F32)
        for p in pos:
            perm = perm + (row == p).astype(F32)
        return _pack_rows(jnp.dot(perm.astype(BF16), hb, preferred_element_type=F32))

    def chunk_copy(tl, sl, j, second):
        src = pl.multiple_of(j * CHUNK - (HALF_SORT if second else 0), CHUNK)
        dst = pl.multiple_of(dst_ref[tl * MAX_CHUNKS + j], CHUNK)
        half = srt_b if second else srt_a
        return pltpu.make_async_copy(half.at[sl, pl.ds(src, CHUNK)], xs_ref.at[pl.ds(dst, CHUNK)], sems.at[sl])

    srt_a[slot] = sorted_half(0)
    for j in range(HALF_CHUNKS):
        chunk_copy(tile, slot, j, False).start()
    srt_b[slot] = sorted_half(HALF_SORT)

    def issue(j, carry):
        chunk_copy(tile, slot, j, True).start()
        return carry

    lax.fori_loop(HALF_CHUNKS, nch_ref[tile], issue, 0)

    def drain_tile(tl, sl):
        pltpu.make_async_copy(srt_a.at[sl], xs_ref.at[pl.ds(0, HALF_SORT)], sems.at[sl]).wait()

        def drain_b(j, carry):
            chunk_copy(tl, sl, j, True).wait()
            return carry

        lax.fori_loop(HALF_CHUNKS, nch_ref[tl], drain_b, 0)

    @pl.when(step > 0)
    def _():
        drain_tile(tile - 1, 1 - slot)

    @pl.when(step == pl.num_programs(0) - 1)
    def _():
        drain_tile(tile, slot)


def _dispatch(nch, dst, h2, idx, rank, seg, xs, t0):
    n, d = h2.shape
    tm = ROW_TILE
    tok = lambda width: pl.BlockSpec((tm, width), lambda i, a, b: (i, 0))
    return pl.pallas_call(
        functools.partial(_dispatch_kernel, tm=tm, t0=t0),
        grid_spec=pltpu.PrefetchScalarGridSpec(
            num_scalar_prefetch=2,
            grid=(n // tm,),
            in_specs=[tok(d), tok(LANES), tok(LANES),
                      pl.BlockSpec((1, 1, LANES), lambda i, a, b: (t0 + i, 0, 0)),
                      pl.BlockSpec(memory_space=pl.ANY)],
            out_specs=pl.BlockSpec(memory_space=pl.ANY),
            scratch_shapes=[pltpu.VMEM((2, HALF_SORT, HALF), U32), pltpu.VMEM((2, HALF_SORT, HALF), U32),
                            pltpu.SemaphoreType.DMA((2,))]),
        out_shape=jax.ShapeDtypeStruct(xs.shape, xs.dtype),
        input_output_aliases={6: 0},
        compiler_params=_params("arbitrary"),
    )(nch, dst, h2, idx, rank, seg, xs)


def _expert_kernel(start_ref, nblk_ref, xs_ref, w1_ref, b1_ref, w2_ref, b2_ref, o_ref,
                   w1b, w2b, xbuf, obuf, sin, sout):
    e = pl.program_id(0)
    n = nblk_ref[e]

    def rows(expert, j):
        return pl.ds(pl.multiple_of(start_ref[expert] + j * MOE_ROWS, MOE_ROWS), MOE_ROWS)

    def in_copy(expert, j, slot):
        return pltpu.make_async_copy(xs_ref.at[rows(expert, j)], xbuf.at[slot], sin.at[slot])

    def out_copy(j, slot):
        return pltpu.make_async_copy(obuf.at[slot], o_ref.at[rows(e, j)], sout.at[slot])

    @pl.when((e == 0) & (n > 0))
    def _():
        in_copy(e, 0, 0).start()

    @pl.when(n > 0)
    def _():
        w1b[...] = w1_ref[0].astype(BF16)
        w2b[...] = w2_ref[0].astype(BF16)

        def block(j, carry):
            slot = j % 2

            @pl.when(j + 1 < n)
            def _():
                in_copy(e, j + 1, 1 - slot).start()

            in_copy(e, j, slot).wait()

            @pl.when(j >= 2)
            def _():
                out_copy(j - 2, slot).wait()

            x_lo, x_hi = _unpack_rows(xbuf[slot])
            gu = (jnp.dot(x_lo, w1b[:HALF, :], preferred_element_type=F32)
                  + jnp.dot(x_hi, w1b[HALF:, :], preferred_element_type=F32)) + b1_ref[0]
            gt = jnp.minimum(gu[:, :D_FF], SWIGLU_LIMIT)
            up = jnp.clip(gu[:, D_FF:], -SWIGLU_LIMIT, SWIGLU_LIMIT)
            hid = (up + 1.0) * gt * (1.0 / (1.0 + jnp.exp(-SWIGLU_ALPHA * gt)))
            out = jnp.dot(hid.astype(BF16), w2b[...], preferred_element_type=F32) + b2_ref[0]
            obuf[slot] = _pack_rows(out.astype(BF16).astype(F32))
            out_copy(j, slot).start()
            return carry

        lax.fori_loop(0, n, block, 0)

        @pl.when(n >= 2)
        def _():
            out_copy(n - 2, n % 2).wait()

        out_copy(n - 1, (n - 1) % 2).wait()

    nxt = jnp.minimum(e + 1, pl.num_programs(0) - 1)

    @pl.when((e + 1 < pl.num_programs(0)) & (nblk_ref[nxt] > 0))
    def _():
        in_copy(nxt, 0, 0).start()


def _experts(row_start, n_blk, xs, w1, b1, w2, b2, layer):
    n_rows, half = xs.shape
    d = 2 * half
    f2 = w1.shape[-1]
    n_exp = w1.shape[1]
    ne = w1.shape[0] * n_exp
    w1 = w1.reshape(ne, d, f2)
    w2 = w2.reshape(ne, D_FF, d)
    pick = lambda e, st, nb: (layer * n_exp + e, 0, 0)
    return pl.pallas_call(
        _expert_kernel,
        grid_spec=pltpu.PrefetchScalarGridSpec(
            num_scalar_prefetch=2,
            grid=(n_exp,),
            in_specs=[pl.BlockSpec(memory_space=pl.ANY),
                      pl.BlockSpec((1, d, f2), pick), pl.BlockSpec((1, 1, f2), pick),
                      pl.BlockSpec((1, D_FF, d), pick), pl.BlockSpec((1, 1, d), pick)],
            out_specs=pl.BlockSpec(memory_space=pl.ANY),
            scratch_shapes=[pltpu.VMEM((d, f2), BF16), pltpu.VMEM((D_FF, d), BF16),
                            pltpu.VMEM((2, MOE_ROWS, half), U32), pltpu.VMEM((2, MOE_ROWS, half), U32),
                            pltpu.SemaphoreType.DMA((2,)), pltpu.SemaphoreType.DMA((2,))]),
        out_shape=jax.ShapeDtypeStruct((n_rows, half), U32),
        input_output_aliases={2: 0},
        compiler_params=_params("arbitrary"),
    )(row_start, n_blk, xs, w1, b1.reshape(ne, 1, f2), w2, b2.reshape(ne, 1, d))


def _combine_kernel(nch_ref, dst_ref, outs_ref, idx_ref, rank_ref, gate_ref, seg_ref, x1_ref, g2_ref, fg_ref,
                    o_ref, buf, sems, *, tm, nt, t0, final):
    step = pl.program_id(0) * nt + pl.program_id(1)
    tile = t0 + step
    slot = step % 2

    def chunk_copy(tl, sl, j):
        src = pl.multiple_of(dst_ref[tl * MAX_CHUNKS + j], CHUNK)
        dst = pl.multiple_of(j * CHUNK, CHUNK)
        return pltpu.make_async_copy(outs_ref.at[pl.ds(src, CHUNK)], buf.at[sl, pl.ds(dst, CHUNK)], sems.at[sl])

    def fetch(tl, sl, first):
        def issue(j, carry):
            chunk_copy(tl, sl, j).start()
            return carry
        lax.fori_loop(first, nch_ref[tl], issue, 0)

    @pl.when(step == 0)
    def _():
        buf[...] = jnp.zeros_like(buf)
        fetch(tile, slot, 0)

    last = pl.num_programs(0) * nt - 1
    nxt = t0 + jnp.minimum(step + 1, last)
    for j in range(MIN_CHUNKS):
        chunk_copy(nxt, 1 - slot, j).start()

    def drain(j, carry):
        chunk_copy(tile, slot, j).wait()
        return carry

    n = nch_ref[tile]
    idx, rank, gate, seg = idx_ref[0], rank_ref[0], gate_ref[0], seg_ref[0]
    col = lax.broadcasted_iota(I32, (tm, SORT_ROWS), 1).astype(F32)
    weights = jnp.zeros((tm, SORT_ROWS), F32)
    for k in range(TOP_K):
        pos = jnp.sum(_slot_values(idx, rank, seg, k), axis=-1, keepdims=True)
        weights = jnp.where(col == pos, gate[:, k:k + 1], weights)
    wh, wl = _split(weights)
    fetch(nxt, 1 - slot, MIN_CHUNKS)
    min_rows = MIN_CHUNKS * CHUNK
    pltpu.make_async_copy(outs_ref.at[pl.ds(0, min_rows)], buf.at[slot, pl.ds(0, min_rows)], sems.at[slot]).wait()
    lax.fori_loop(MIN_CHUNKS, n, drain, 0)
    y = jnp.concatenate(
        [jnp.dot(wh, rows, preferred_element_type=F32) + jnp.dot(wl, rows, preferred_element_type=F32)
         for rows in _unpack_rows(buf[slot])], axis=1)
    x2 = x1_ref[0] + g2_ref[0] * y
    if final:
        x2 = _rms(x2) * fg_ref[...]
    o_ref[0] = x2

    @pl.when(step == last)
    def _():
        def drain_extra(j, carry):
            chunk_copy(nxt, 1 - slot, j).wait()
            return carry
        lax.fori_loop(0, nch_ref[nxt], drain_extra, 0)


def _combine(nch, dst, outs, idx, rank, gate, seg, x1, g2, fg, t0, final):
    b, l, d = x1.shape
    tm = ROW_TILE
    nt = l // tm
    per_batch = g2.shape[0] > 1
    bidx = (lambda i, t, a, c: (i, 0, 0)) if per_batch else (lambda i, t, a, c: (0, 0, 0))
    tok = lambda width: pl.BlockSpec((1, tm, width), lambda i, t, a, c: (i, t, 0))
    return pl.pallas_call(
        functools.partial(_combine_kernel, tm=tm, nt=nt, t0=t0, final=final),
        grid_spec=pltpu.PrefetchScalarGridSpec(
            num_scalar_prefetch=2,
            grid=(b, nt),
            in_specs=[pl.BlockSpec(memory_space=pl.ANY), tok(LANES), tok(LANES), tok(LANES),
                      pl.BlockSpec((1, 1, LANES), lambda i, t, a, c: (t0 + i * nt + t, 0, 0)),
                      tok(d), pl.BlockSpec((1, 1, d), bidx),
                      pl.BlockSpec((1, d), lambda i, t, a, c: (0, 0))],
            out_specs=tok(d),
            scratch_shapes=[pltpu.VMEM((2, SORT_ROWS, HALF), U32), pltpu.SemaphoreType.DMA((2,))]),
        out_shape=jax.ShapeDtypeStruct((b, l, d), F32),
        compiler_params=_params("arbitrary", "arbitrary"),
    )(nch, dst, outs, idx, rank, gate, seg, x1, g2, fg)


def _layer_weights(w_in_l):
    cuts = np.cumsum([A_WIDTH, A_WIDTH, A_WIDTH, B_WIDTH, C_WIDTH, C_KV * C_HD]).tolist()
    front = w_in_l[:, :cuts[4]]
    kc = w_in_l[:, cuts[4]:cuts[5]]
    vc = w_in_l[:, cuts[5]:]

    def rep(w):
        return jnp.concatenate([w[:, C_HD * (j // C_GROUP):C_HD * (j // C_GROUP + 1)] for j in range(C_HEADS)], axis=1)

    lat = jnp.concatenate([front, rep(kc), rep(vc)], axis=1).astype(BF16)
    ctx = jnp.concatenate([front, rep(kc), rep(vc), kc, vc], axis=1).astype(BF16)
    return ctx, lat


def _rep_heads(a):
    return jnp.repeat(a, C_GROUP, axis=2).reshape(a.shape[0], a.shape[1], C_HEADS * C_HD)


def kernel(x_prompt, x_sample, c, cache_diff_k, cache_diff_v, cache_win_k, cache_win_v, c_ctx, norm1_g, norm2_g, w_mod, b_mod, w_in, diff_lambda, diff_subln_g, w_pool, pool_scale, sink, w_out, w_router, b_router, w1, b1, w2, b2, final_g):
    depth = w_in.shape[0]
    bc, lc, d = x_prompt.shape
    bl, ll, _ = x_sample.shape
    n_ctx, n_lat = bc * lc, bl * ll
    n_tok = n_ctx + n_lat
    n_tiles = n_tok // ROW_TILE
    n_blocks = -(-(n_tok * TOP_K + n_tiles * N_EXPERTS * (CHUNK - 1)) // MOE_ROWS) + N_EXPERTS
    n_rows = n_blocks * MOE_ROWS

    mod_rows = -(-(1 + bl) // 8) * 8
    cmat = jnp.zeros((mod_rows, d), F32).at[0].set(c_ctx).at[1:1 + bl].set(c)
    mod = _mod_vectors(cmat, w_mod, b_mod)
    tabs_a = _rope_tables(ll, A_HD, A_WIDTH)
    tabs_c = _rope_tables(ll, C_HD, C_WIDTH)
    tables = (tabs_a[0], tabs_a[1], tabs_c[0], tabs_c[1])
    fg = final_g.reshape(1, d)

    xp, xs_lat = x_prompt, x_sample
    new_cache = [[], [], [], []]
    for i in range(depth):
        lam_init = 0.8 - 0.6 * math.exp(-0.3 * i)
        mv = lambda rows, j: mod[i, rows, j * d:(j + 1) * d].reshape(-1, 1, d)
        ctx_rows, lat_rows = slice(0, 1), slice(1, 1 + bl)
        w_ctx, w_lat = _layer_weights(w_in[i])
        n1 = norm1_g[i].reshape(1, d)
        n2 = norm2_g[i].reshape(1, d)
        wbd = jax.scipy.linalg.block_diag(*[w_pool[i, g] for g in range(B_GROUPS)]).astype(BF16)
        ps = pool_scale[i].reshape(1, B_WIDTH)
        g_tiled = jnp.tile(diff_subln_g[i], A_HEADS).reshape(1, A_WIDTH)
        wo = w_out[i].astype(BF16)
        wr = jnp.zeros((d, LANES), F32).at[:, :N_EXPERTS].set(w_router[i])
        br = jnp.zeros((1, LANES), F32).at[0, :N_EXPERTS].set(b_router[i])

        qa, ka, va, u, qc, kr, vr, ka32, va32, kc32, vc32 = _inproj(
            xp, mv(ctx_rows, 0), mv(ctx_rows, 1), n1, w_ctx, None)
        new_cache[0].append(ka32.reshape(bc, lc, 2 * A_HEADS, A_HD))
        new_cache[1].append(va32.reshape(bc, lc, A_HEADS, 2 * A_HD))
        new_cache[2].append(kc32.reshape(bc, lc, C_KV, C_HD))
        new_cache[3].append(vc32.reshape(bc, lc, C_KV, C_HD))
        ob = _pool(u, wbd, ps)
        oa = _diff_attn(qa, [(ka, va)], diff_lambda[i], g_tiled, lam_init)
        oc = _gqa(sink[i], qc, kr, vr)
        x1_c, h2_c, idx_c, gate_c, rank_c, cnt_c = _finish(
            xp, oa, ob, oc, wo, mv(ctx_rows, 2), mv(ctx_rows, 3), mv(ctx_rows, 4), n2, wr, br)

        qa, ka, va, u, qc, kr, vr = _inproj(xs_lat, mv(lat_rows, 0), mv(lat_rows, 1), n1, w_lat, tables)
        ob = _pool(u, wbd, ps)
        dk = cache_diff_k[:, i].reshape(bl, -1, A_WIDTH).astype(BF16)
        dv = cache_diff_v[:, i].reshape(bl, -1, A_WIDTH).astype(BF16)
        oa = _diff_attn(qa, [(ka, va), (dk, dv)], diff_lambda[i], g_tiled, lam_init)
        wk = _rep_heads(cache_win_k[:, i]).astype(BF16)
        wv = _rep_heads(cache_win_v[:, i]).astype(BF16)
        oc = _gqa(sink[i], qc, kr, vr, (wk, wv))
        x1_l, h2_l, idx_l, gate_l, rank_l, cnt_l = _finish(
            xs_lat, oa, ob, oc, wo, mv(lat_rows, 2), mv(lat_rows, 3), mv(lat_rows, 4), n2, wr, br)

        cnt = jnp.concatenate([cnt_c, cnt_l], axis=0)[:, 0, :N_EXPERTS].astype(I32)
        c8 = (cnt + CHUNK - 1) // CHUNK * CHUNK
        seg_end = jnp.cumsum(c8, axis=1)
        seg = seg_end - c8
        padded = (jnp.sum(c8, axis=0) + MOE_ROWS - 1) // MOE_ROWS * MOE_ROWS
        pad_end = jnp.cumsum(padded)
        gbase = (pad_end - padded)[None, :] + jnp.cumsum(c8, axis=0) - c8
        nch = seg_end[:, -1] // CHUNK
        j8 = jnp.arange(MAX_CHUNKS, dtype=I32) * CHUNK
        chunk_e = jnp.minimum(jnp.sum((seg_end[:, None, :] <= j8[None, :, None]).astype(I32), axis=-1),
                              N_EXPERTS - 1)
        onehot = chunk_e[..., None] == jnp.arange(N_EXPERTS, dtype=I32)
        dst = (jnp.sum(jnp.where(onehot, (gbase - seg)[:, None, :], 0), axis=-1) + j8[None, :]).reshape(-1)
        seg_f = jnp.zeros((cnt.shape[0], 1, LANES), F32).at[:, 0, :N_EXPERTS].set(seg.astype(F32))
        tiles_c = cnt_c.shape[0]

        rows_in = jnp.zeros((n_rows, HALF), U32)
        rows_in = _dispatch(nch, dst, h2_c.reshape(n_ctx, d), idx_c.reshape(n_ctx, LANES),
                            rank_c.reshape(n_ctx, LANES), seg_f, rows_in, 0)
        rows_in = _dispatch(nch, dst, h2_l.reshape(n_lat, d), idx_l.reshape(n_lat, LANES),
                            rank_l.reshape(n_lat, LANES), seg_f, rows_in, tiles_c)
        rows_out = _experts(pad_end - padded, padded // MOE_ROWS, rows_in, w1, b1, w2, b2, i)
        final = i == depth - 1
        xp = _combine(nch, dst, rows_out, idx_c, rank_c, gate_c, seg_f, x1_c, mv(ctx_rows, 5), fg, 0, final)
        xs_lat = _combine(nch, dst, rows_out, idx_l, rank_l, gate_l, seg_f, x1_l, mv(lat_rows, 5), fg,
                          tiles_c, final)

    return (xp, xs_lat) + tuple(jnp.stack(parts, axis=1) for parts in new_cache)
```

```python
import functools
import math

import numpy as np
import jax
import jax.numpy as jnp
from jax import lax
from jax.experimental import pallas as pl
from jax.experimental.pallas import tpu as pltpu

F32 = jnp.float32
BF16 = jnp.bfloat16
I32 = jnp.int32
U32 = jnp.uint32

D_MODEL = 1024
GRID_W = 64
ROPE_BASE = 10000.0
NORM_EPS = 1e-6
A_HD = 32
A_HEADS = 4
A_WIDTH = 256
B_WIDTH = 256
B_GROUPS = 4
B_GC = 64
C_HD = 64
C_HEADS = 8
C_KV = 2
C_GROUP = 4
C_WIDTH = 512
WINDOW = 128
QBLOCK = 128
N_EXPERTS = 32
TOP_K = 4
D_FF = 1024
SWIGLU_LIMIT = 7.0
SWIGLU_ALPHA = 1.702

LANES = 128
ROW_TILE = 256
MOE_ROWS = 512
KEY_CHUNK = 512
NEG = -1e30
LOG2E = math.log2(math.e)
VMEM_LIMIT = 56 * 1024 * 1024


def _params(*sem):
    return pltpu.CompilerParams(dimension_semantics=sem, vmem_limit_bytes=VMEM_LIMIT)


def _split(x):
    hi = x.astype(BF16)
    lo = (x - hi.astype(F32)).astype(BF16)
    return hi, lo


def _dot3(a, b):
    ah, al = _split(a)
    bh, bl = _split(b)
    return (jnp.dot(ah, bh, preferred_element_type=F32)
            + (jnp.dot(ah, bl, preferred_element_type=F32)
               + jnp.dot(al, bh, preferred_element_type=F32)))


def _nt_dot(a, b):
    return lax.dot_general(a, b, (((1,), (1,)), ((), ())), preferred_element_type=F32)


def _rms(x):
    return x * lax.rsqrt(jnp.mean(x * x, axis=-1, keepdims=True) + NORM_EPS)


HALF = D_MODEL // 2
HIGH_BITS = 0xFFFF0000


def _pack_rows(x):
    lo = lax.bitcast_convert_type(x[:, :HALF], U32) >> 16
    hi = lax.bitcast_convert_type(x[:, HALF:], U32) & jnp.uint32(HIGH_BITS)
    return lo | hi


def _unpack_rows(w):
    lo = lax.bitcast_convert_type(w << 16, F32).astype(BF16)
    hi = lax.bitcast_convert_type(w & jnp.uint32(HIGH_BITS), F32).astype(BF16)
    return lo, hi


def _mod_kernel(c_ref, w_ref, b_ref, o_ref):
    c = c_ref[...]
    a = c * (1.0 / (1.0 + jnp.exp(-c)))
    o_ref[0] = _dot3(a, w_ref[0]) + b_ref[0]


def _mod_vectors(cmat, w_mod, b_mod):
    depth, d, e = w_mod.shape
    rows = cmat.shape[0]
    tn = 512
    return pl.pallas_call(
        _mod_kernel,
        grid=(depth, e // tn),
        in_specs=[pl.BlockSpec((rows, d), lambda l, j: (0, 0)),
                  pl.BlockSpec((1, d, tn), lambda l, j: (l, 0, j)),
                  pl.BlockSpec((1, 1, tn), lambda l, j: (l, 0, j))],
        out_specs=pl.BlockSpec((1, rows, tn), lambda l, j: (l, 0, j)),
        out_shape=jax.ShapeDtypeStruct((depth, rows, e), F32),
        compiler_params=_params("arbitrary", "arbitrary"),
    )(cmat, w_mod, b_mod.reshape(depth, 1, e))


def _rope(z, col_ref, row_ref, nf, r0):
    tm, w = z.shape
    outs = []
    for g in range(tm // GRID_W):
        zs = z[GRID_W * g:GRID_W * (g + 1), :]
        c = col_ref[0] + row_ref[0, pl.ds(r0 + g, 1), :]
        sm = col_ref[1] + row_ref[1, pl.ds(r0 + g, 1), :]
        sp = col_ref[2] + row_ref[2, pl.ds(r0 + g, 1), :]
        outs.append(zs * c + pltpu.roll(zs, w - nf, 1) * sm + pltpu.roll(zs, nf, 1) * sp)
    return jnp.concatenate(outs, axis=0)


def _inproj_kernel(*refs, rope, tm):
    x_ref, sh_ref, sc_ref, n1_ref, w_ref = refs[:5]
    if rope:
        ta_col, ta_row, tc_col, tc_row = refs[5:9]
        qa_o, ka_o, va_o, u_o, qc_o, kr_o, vr_o = refs[9:]
    else:
        qa_o, ka_o, va_o, u_o, qc_o, kr_o, vr_o, ka32_o, va32_o, kc32_o, vc32_o = refs[5:]
    x = x_ref[0]
    h = (_rms(x) * n1_ref[...]) * (1.0 + sc_ref[0]) + sh_ref[0]
    hb = h.astype(BF16)

    def seg(a, b):
        return jnp.dot(hb, w_ref[:, a:b], preferred_element_type=F32)

    qa, ka, va, u = seg(0, 256), seg(256, 512), seg(512, 768), seg(768, 1024)
    qc, kr, vr = seg(1024, 1536), seg(1536, 2048), seg(2048, 2560)
    if rope:
        r0 = pl.program_id(1) * (tm // GRID_W)
        qa = _rope(qa, ta_col, ta_row, A_HD // 4, r0)
        ka = _rope(ka, ta_col, ta_row, A_HD // 4, r0)
        qc = _rope(qc, tc_col, tc_row, C_HD // 4, r0)
        kr = _rope(kr, tc_col, tc_row, C_HD // 4, r0)
    else:
        ka32_o[0] = ka
        va32_o[0] = va
        kc32_o[0] = seg(2560, 2688)
        vc32_o[0] = seg(2688, 2816)
    qa_o[0] = (qa * (A_HD ** -0.5 * LOG2E)).astype(BF16)
    ka_o[0] = ka.astype(BF16)
    va_o[0] = va.astype(BF16)
    u_o[0] = u
    qc_o[0] = (qc * (C_HD ** -0.5 * LOG2E)).astype(BF16)
    kr_o[0] = kr.astype(BF16)
    vr_o[0] = vr.astype(BF16)


def _inproj(x, sh, sc, n1, w, tables):
    b, l, d = x.shape
    tm = min(ROW_TILE, l)
    rope = tables is not None
    per_batch = sh.shape[0] > 1
    bidx = (lambda i, t: (i, 0, 0)) if per_batch else (lambda i, t: (0, 0, 0))
    tok = lambda width: pl.BlockSpec((1, tm, width), lambda i, t: (i, t, 0))
    in_specs = [tok(d),
                pl.BlockSpec((1, 1, d), bidx), pl.BlockSpec((1, 1, d), bidx),
                pl.BlockSpec((1, d), lambda i, t: (0, 0)),
                pl.BlockSpec(w.shape, lambda i, t: (0, 0))]
    args = [x, sh, sc, n1, w]
    widths = [(256, BF16), (256, BF16), (256, BF16), (256, F32), (512, BF16), (512, BF16), (512, BF16)]
    if rope:
        for tab in tables:
            in_specs.append(pl.BlockSpec(tab.shape, lambda i, t: (0, 0, 0)))
            args.append(tab)
    else:
        widths += [(256, F32), (256, F32), (128, F32), (128, F32)]
    return pl.pallas_call(
        functools.partial(_inproj_kernel, rope=rope, tm=tm),
        grid=(b, l // tm),
        in_specs=in_specs,
        out_specs=[tok(wd) for wd, _ in widths],
        out_shape=[jax.ShapeDtypeStruct((b, l, wd), dt) for wd, dt in widths],
        compiler_params=_params("arbitrary", "arbitrary"),
    )(*args)


def _rope_tables(n_lat, head_dim, width):
    rows = n_lat // GRID_W
    nf = head_dim // 4
    inv = ROPE_BASE ** (-jnp.arange(nf, dtype=F32) / nf)
    lane = np.arange(width) % head_dim
    half = lane // (2 * nf)
    pair = (lane // nf) % 2
    f = lane % nf

    def part(pos, which):
        ang = pos[:, None] * inv[f][None, :]
        on = jnp.asarray(half == which, F32)[None, :]
        c = jnp.cos(ang) * on
        s = jnp.sin(ang) * on
        sm = -s * jnp.asarray(pair == 0, F32)[None, :]
        sp = s * jnp.asarray(pair == 1, F32)[None, :]
        return jnp.stack([c, sm, sp])

    return part(jnp.arange(GRID_W, dtype=F32), 1), part(jnp.arange(rows, dtype=F32), 0)


POOL_PAD = 8
POOL_CHUNK = 256


def _pool_kernel(u_ref, w_ref, ps_ref, o_ref, pad_ref, *, l):
    zeros = jnp.zeros((POOL_PAD, B_WIDTH), F32)
    pad_ref[0:POOL_PAD, :] = zeros
    pad_ref[POOL_PAD + l:2 * POOL_PAD + l, :] = zeros
    pad_ref[POOL_PAD:POOL_PAD + l, :] = u_ref[0]
    ch = min(POOL_CHUNK, l)
    lane = lax.broadcasted_iota(I32, (ch, B_WIDTH), 1)
    grp = lane >> 6
    half = jnp.where(grp == 0, 1, jnp.where(grp == 1, 2, jnp.where(grp == 2, 4, 8)))
    row = lax.broadcasted_iota(I32, (ch, B_WIDTH), 0)
    for c in range(0, l, ch):
        ld = lambda k: pad_ref[c + POOL_PAD + k:c + POOL_PAD + k + ch, :]
        cur = ld(0)
        s2 = ld(-1) + cur
        s4 = s2 + (ld(-2) + ld(1))
        s8 = s4 + ((ld(-4) + ld(-3)) + (ld(2) + ld(3)))
        s16 = s8 + (((ld(-8) + ld(-7)) + (ld(-6) + ld(-5))) + ((ld(4) + ld(5)) + (ld(6) + ld(7))))
        win = jnp.where(grp == 0, s2, jnp.where(grp == 1, s4, jnp.where(grp == 2, s8, s16)))
        t = row + c
        cnt = (jnp.minimum(t + half, l) - jnp.maximum(t - half, 0)).astype(F32)
        r = (win / cnt - cur).astype(BF16)
        y = jnp.dot(r, w_ref[...], preferred_element_type=F32) * ps_ref[...]
        o_ref[0, c:c + ch, :] = y.astype(BF16)


def _pool(u, wbd, ps):
    b, l, w = u.shape
    return pl.pallas_call(
        functools.partial(_pool_kernel, l=l),
        grid=(b,),
        in_specs=[pl.BlockSpec((1, l, w), lambda i: (i, 0, 0)),
                  pl.BlockSpec((w, w), lambda i: (0, 0)),
                  pl.BlockSpec((1, w), lambda i: (0, 0))],
        out_specs=pl.BlockSpec((1, l, w), lambda i: (i, 0, 0)),
        out_shape=jax.ShapeDtypeStruct((b, l, w), BF16),
        scratch_shapes=[pltpu.VMEM((l + 2 * POOL_PAD, w), F32)],
        compiler_params=_params("arbitrary"),
    )(u, wbd, ps)


def _diff_attn_kernel(*refs, n_src, tq, lam_init):
    q_ref = refs[0]
    srcs = [(refs[1 + 2 * i], refs[2 + 2 * i]) for i in range(n_src)]
    lamp_ref, g_ref, o_ref, s_scr, p_scr = refs[1 + 2 * n_src:]
    chunks = []
    col = 0
    for k_ref, _ in srcs:
        keys = k_ref.shape[1]
        tk = min(keys, KEY_CHUNK)
        for c in range(0, keys, tk):
            chunks.append((k_ref, c, tk, col))
            col += tk
    q32 = q_ref[0].astype(F32)
    lp = lamp_ref[...]
    lam = (jnp.exp(jnp.sum(lp[0:1] * lp[1:2], axis=-1, keepdims=True))
           - jnp.exp(jnp.sum(lp[2:3] * lp[3:4], axis=-1, keepdims=True)) + lam_init)
    lane_row = lax.broadcasted_iota(I32, (1, A_WIDTH), 1)
    lane = lax.broadcasted_iota(I32, (tq, A_WIDTH), 1)

    def scores(j):
        qm = (q32 * ((lane_row >> 5) == j).astype(F32)).astype(BF16)
        part = jnp.full((tq, LANES), NEG, F32)
        for k_ref, c, tk, col in chunks:
            s = _nt_dot(qm, k_ref[0, c:c + tk, :])
            s_scr[j % 2, :, col:col + tk] = s
            for i in range(0, tk, LANES):
                part = jnp.maximum(part, s[:, i:i + LANES])
        return jnp.max(part, axis=-1, keepdims=True)

    def weights(j, row_max):
        part = jnp.zeros((tq, LANES), F32)
        for _, _, tk, col in chunks:
            p = jnp.exp2(s_scr[j % 2, :, col:col + tk] - row_max)
            for i in range(0, tk, LANES):
                part = part + p[:, i:i + LANES]
            p_scr[j % 2, :, col:col + tk] = p.astype(BF16)
        return jnp.sum(part, axis=-1, keepdims=True)

    def values(j, den):
        acc, col = None, 0
        for _, v_ref in srcs:
            keys = v_ref.shape[1]
            part = jnp.dot(p_scr[j % 2, :, col:col + keys], v_ref[0], preferred_element_type=F32)
            acc = part if acc is None else acc + part
            col += keys
        return acc / den

    out = jnp.zeros((tq, A_WIDTH), F32)
    row_max = scores(0)
    for j in range(2 * A_HEADS):
        den = weights(j, row_max)
        if j + 1 < 2 * A_HEADS:
            row_max = scores(j + 1)
        if j % 2 == 0:
            first = values(j, den)
        else:
            out = out + jnp.where((lane >> 6) == j // 2, first - lam * values(j, den), 0.0)
    sq = out * out
    rs = jnp.zeros((tq, A_WIDTH), F32)
    for h in range(A_HEADS):
        msk = (lane >> 6) == h
        ms = jnp.sum(jnp.where(msk, sq, 0.0), axis=-1, keepdims=True) * (1.0 / (2 * A_HD))
        rs = rs + jnp.where(msk, lax.rsqrt(ms + NORM_EPS), 0.0)
    o_ref[0] = (((out * rs) * g_ref[...]) * (1.0 - lam_init)).astype(BF16)


def _diff_attn(q, srcs, lam_p, g_tiled, lam_init):
    b, l, w = q.shape
    s_len = sum(k.shape[1] for k, _ in srcs)
    tq = min(ROW_TILE, l)
    in_specs = [pl.BlockSpec((1, tq, w), lambda i, t: (i, t, 0))]
    args = [q]
    for k, v in srcs:
        for a in (k, v):
            in_specs.append(pl.BlockSpec((1,) + a.shape[1:], lambda i, t: (i, 0, 0)))
            args.append(a)
    in_specs += [pl.BlockSpec(lam_p.shape, lambda i, t: (0, 0)),
                 pl.BlockSpec(g_tiled.shape, lambda i, t: (0, 0))]
    return pl.pallas_call(
        functools.partial(_diff_attn_kernel, n_src=len(srcs), tq=tq, lam_init=lam_init),
        grid=(b, l // tq),
        in_specs=in_specs,
        out_specs=pl.BlockSpec((1, tq, w), lambda i, t: (i, t, 0)),
        out_shape=jax.ShapeDtypeStruct((b, l, w), BF16),
        scratch_shapes=[pltpu.VMEM((2, tq, s_len), F32), pltpu.VMEM((2, tq, s_len), BF16)],
        compiler_params=_params("arbitrary", "arbitrary"),
    )(*args, lam_p, g_tiled)


GQA_SLAB = C_GROUP * C_HD


def _gqa_kernel(*refs, windowed, tq, l):
    if windowed:
        sink_ref, q_ref, k_ref, v_ref, kc_ref, vc_ref, o_ref, s_scr, p_scr = refs
    else:
        sink_ref, q_ref, k_ref, v_ref, o_ref, s_scr, p_scr = refs
    i = pl.program_id(1)
    rows = C_GROUP * tq
    shift = int(math.log2(tq))
    q32 = q_ref[0].astype(F32)
    lane_row = lax.broadcasted_iota(I32, (1, GQA_SLAB), 1)
    lane = lax.broadcasted_iota(I32, (tq, GQA_SLAB), 1)
    rid = lax.broadcasted_iota(I32, (rows, 1), 0)
    if windowed:
        ws = pl.multiple_of(jnp.clip((i - 1) * tq, 0, l - 3 * tq), tq)
        qpos = i * tq + (rid & (tq - 1))
        kpos = ws + lax.broadcasted_iota(I32, (1, 3 * tq), 1)
        band = jnp.where(jnp.abs(kpos - qpos) <= WINDOW, 0.0, NEG)
    def slab(g):
        return slice(GQA_SLAB * g, GQA_SLAB * (g + 1))

    def key_sources(g):
        if windowed:
            return [(lambda: k_ref[0, pl.ds(ws, 3 * tq), slab(g)], lambda: v_ref[0, pl.ds(ws, 3 * tq), slab(g)],
                     band, 3 * tq),
                    (lambda: kc_ref[0, :, slab(g)], lambda: vc_ref[0, :, slab(g)], None, kc_ref.shape[1])]
        return [(lambda: k_ref[0, :, slab(g)], lambda: v_ref[0, :, slab(g)], None, l)]

    def sink_col(g):
        sk = jnp.zeros((rows, 1), F32)
        for hh in range(C_GROUP):
            sk = jnp.where((rid >> shift) == hh, sink_ref[C_GROUP * g + hh] * LOG2E, sk)
        return sk

    def scores(g):
        qg = q32[:, slab(g)]
        qs = jnp.concatenate(
            [(qg * ((lane_row >> 6) == hh).astype(F32)).astype(BF16) for hh in range(C_GROUP)], axis=0)
        part = jnp.full((rows, LANES), NEG, F32)
        col = 0
        for keys, _, mask, n in key_sources(g):
            s = _nt_dot(qs, keys())
            if mask is not None:
                s = s + mask
            s_scr[g, :, col:col + n] = s
            for c in range(0, n, LANES):
                part = jnp.maximum(part, s[:, c:c + LANES])
            col += n
        return jnp.maximum(sink_col(g), jnp.max(part, axis=-1, keepdims=True))

    def weights(g, m):
        part = jnp.zeros((rows, LANES), F32)
        col = 0
        for _, _, _, n in key_sources(g):
            p = jnp.exp2(s_scr[g, :, col:col + n] - m)
            for c in range(0, n, LANES):
                part = part + p[:, c:c + LANES]
            p_scr[g, :, col:col + n] = p.astype(BF16)
            col += n
        return jnp.exp2(sink_col(g) - m) + jnp.sum(part, axis=-1, keepdims=True)

    def values(g, den):
        pv, col = None, 0
        for _, vals, _, n in key_sources(g):
            part = jnp.dot(p_scr[g, :, col:col + n], vals(), preferred_element_type=F32)
            pv = part if pv is None else pv + part
            col += n
        o = pv / den
        og = jnp.zeros((tq, GQA_SLAB), F32)
        for hh in range(C_GROUP):
            og = og + jnp.where((lane >> 6) == hh, o[hh * tq:(hh + 1) * tq, :], 0.0)
        o_ref[0, :, slab(g)] = og.astype(BF16)

    maxes = [scores(g) for g in range(C_KV)]
    dens = [weights(g, maxes[g]) for g in range(C_KV)]
    for g in range(C_KV):
        values(g, dens[g])


def _gqa(sink, q, k, v, cache=None):
    b, l, w = q.shape
    windowed = cache is not None
    tq = QBLOCK
    whole = lambda a: pl.BlockSpec((1,) + a.shape[1:], lambda i, t: (i, 0, 0))
    in_specs = [pl.BlockSpec(memory_space=pltpu.SMEM),
                pl.BlockSpec((1, tq, w), lambda i, t: (i, t, 0)), whole(k), whole(v)]
    args = [sink, q, k, v]
    n_keys = l
    if windowed:
        in_specs += [whole(cache[0]), whole(cache[1])]
        args += list(cache)
        n_keys = 3 * tq + cache[0].shape[1]
    return pl.pallas_call(
        functools.partial(_gqa_kernel, windowed=windowed, tq=tq, l=l),
        grid=(b, l // tq),
        in_specs=in_specs,
        out_specs=pl.BlockSpec((1, tq, w), lambda i, t: (i, t, 0)),
        out_shape=jax.ShapeDtypeStruct((b, l, w), BF16),
        scratch_shapes=[pltpu.VMEM((C_KV, C_GROUP * tq, n_keys), F32),
                        pltpu.VMEM((C_KV, C_GROUP * tq, n_keys), BF16)],
        compiler_params=_params("arbitrary", "arbitrary"),
    )(*args)


def _finish_kernel(x_ref, oa_ref, ob_ref, oc_ref, wo_ref, g1_ref, sh_ref, sc_ref, n2_ref, wr_ref, br_ref,
                   x1_o, h2_o, idx_o, gate_o, rank_o, cnt_o, *, tm):
    y = (jnp.dot(oa_ref[0], wo_ref[0:256, :], preferred_element_type=F32)
         + jnp.dot(ob_ref[0], wo_ref[256:512, :], preferred_element_type=F32)
         + jnp.dot(oc_ref[0], wo_ref[512:1024, :], preferred_element_type=F32))
    x1 = x_ref[0] + g1_ref[0] * y
    x1_o[0] = x1
    h2 = (_rms(x1) * n2_ref[...]) * (1.0 + sc_ref[0]) + sh_ref[0]
    h2_o[0] = h2
    lane = lax.broadcasted_iota(I32, (tm, LANES), 1)
    lanef = lane.astype(F32)
    logits = jnp.where(lane < N_EXPERTS, _dot3(h2, wr_ref[...]) + br_ref[...], NEG)
    work = logits
    tops, sels = [], []
    for k in range(TOP_K):
        mk = jnp.max(work, axis=-1, keepdims=True)
        ik = jnp.min(jnp.where(work == mk, lanef, float(LANES)), axis=-1, keepdims=True)
        sel = lanef == ik
        work = jnp.where(sel, 2.0 * NEG, work)
        tops.append((mk, ik))
        sels.append(sel)
    es = [jnp.exp(mk - tops[0][0]) for mk, _ in tops]
    den = (es[0] + es[1]) + (es[2] + es[3])
    multi = jnp.zeros((tm, LANES), F32)
    for sel in sels:
        multi = multi + sel.astype(F32)
    r_i = lax.broadcasted_iota(I32, (tm, tm), 0)
    c_i = lax.broadcasted_iota(I32, (tm, tm), 1)
    before = (c_i < r_i).astype(BF16)
    prior = jnp.dot(before, multi.astype(BF16), preferred_element_type=F32)
    idx = jnp.zeros((tm, LANES), F32)
    gate = jnp.zeros((tm, LANES), F32)
    rank = jnp.zeros((tm, LANES), F32)
    for k in range(TOP_K):
        slot = lane == k
        idx = jnp.where(slot, tops[k][1], idx)
        gate = jnp.where(slot, es[k] / den, gate)
        rk = jnp.sum(jnp.where(sels[k], prior, 0.0), axis=-1, keepdims=True)
        rank = jnp.where(slot, rk, rank)
    idx_o[0] = idx.astype(I32)
    gate_o[0] = gate
    rank_o[0] = rank.astype(I32)
    cnt_o[0] = jnp.sum(multi, axis=0, keepdims=True)


def _finish(x, oa, ob, oc, wo, g1, sh2, sc2, n2, wr, br):
    b, l, d = x.shape
    tm = min(ROW_TILE, l)
    nt = l // tm
    per_batch = g1.shape[0] > 1
    bidx = (lambda i, t: (i, 0, 0)) if per_batch else (lambda i, t: (0, 0, 0))
    tok = lambda width: pl.BlockSpec((1, tm, width), lambda i, t: (i, t, 0))
    const2 = lambda a: pl.BlockSpec(a.shape, lambda i, t: (0, 0))
    vec = pl.BlockSpec((1, 1, d), bidx)
    return pl.pallas_call(
        functools.partial(_finish_kernel, tm=tm),
        grid=(b, l // tm),
        in_specs=[tok(d), tok(256), tok(256), tok(512), const2(wo), vec, vec, vec,
                  const2(n2), const2(wr), const2(br)],
        out_specs=[tok(d), tok(d), tok(LANES), tok(LANES), tok(LANES),
                   pl.BlockSpec((1, 1, LANES), lambda i, t: (i * nt + t, 0, 0))],
        out_shape=[jax.ShapeDtypeStruct((b, l, d), F32), jax.ShapeDtypeStruct((b, l, d), F32),
                   jax.ShapeDtypeStruct((b, l, LANES), I32), jax.ShapeDtypeStruct((b, l, LANES), F32),
                   jax.ShapeDtypeStruct((b, l, LANES), I32), jax.ShapeDtypeStruct((b * nt, 1, LANES), F32)],
        compiler_params=_params("arbitrary", "arbitrary"),
    )(x, oa, ob, oc, wo, g1, sh2, sc2, n2, wr, br)


CHUNK = 8
SORT_ROWS = ROW_TILE * TOP_K + N_EXPERTS * CHUNK
MAX_CHUNKS = SORT_ROWS // CHUNK
MIN_CHUNKS = ROW_TILE * TOP_K // CHUNK
HALF_SORT = SORT_ROWS // 2
HALF_CHUNKS = HALF_SORT // CHUNK
assert HALF_CHUNKS <= MIN_CHUNKS


def _slot_values(idx, rank, seg, k):
    lane = lax.broadcasted_iota(I32, idx.shape, 1)
    return jnp.where(lane == idx[:, k:k + 1], seg + rank[:, k:k + 1].astype(F32), 0.0)


def _dispatch_kernel(nch_ref, dst_ref, h_ref, idx_ref, rank_ref, seg_ref, xs_in_ref, xs_ref, srt_a, srt_b, sems,
                     *, tm, t0):
    del xs_in_ref
    step = pl.program_id(0)
    tile = t0 + step
    slot = step % 2
    idx, rank, seg = idx_ref[...], rank_ref[...], seg_ref[0]
    ones = jnp.ones((CHUNK, LANES), BF16)
    pos = []
    for k in range(TOP_K):
        hi, lo = _split(_slot_values(idx, rank, seg, k))
        pos.append((_nt_dot(ones, hi) + _nt_dot(ones, lo))[0:1, :])
    hb = h_ref[...].astype(BF16)

    def sorted_half(first_row):
        row = (lax.broadcasted_iota(I32, (HALF_SORT, tm), 0) + first_row).astype(F32)
        perm = jnp.zeros((HALF_SORT, tm), F32)
        for p in pos:
            perm = perm + (row == p).astype(F32)
        return _pack_rows(jnp.dot(perm.astype(BF16), hb, preferred_element_type=F32))

    def chunk_copy(tl, sl, j, second):
        src = pl.multiple_of(j * CHUNK - (HALF_SORT if second else 0), CHUNK)
        dst = pl.multiple_of(dst_ref[tl * MAX_CHUNKS + j], CHUNK)
        half = srt_b if second else srt_a
        return pltpu.make_async_copy(half.at[sl, pl.ds(src, CHUNK)], xs_ref.at[pl.ds(dst, CHUNK)], sems.at[sl])

    srt_a[slot] = sorted_half(0)
    for j in range(HALF_CHUNKS):
        chunk_copy(tile, slot, j, False).start()
    srt_b[slot] = sorted_half(HALF_SORT)
    for j in range(HALF_CHUNKS, MIN_CHUNKS):
        chunk_copy(tile, slot, j, True).start()

    def issue(j, carry):
        chunk_copy(tile, slot, j, True).start()
        return carry

    lax.fori_loop(MIN_CHUNKS, nch_ref[tile], issue, 0)

    def drain_tile(tl, sl):
        rest = (MIN_CHUNKS - HALF_CHUNKS) * CHUNK
        pltpu.make_async_copy(srt_a.at[sl], xs_ref.at[pl.ds(0, HALF_SORT)], sems.at[sl]).wait()
        pltpu.make_async_copy(srt_b.at[sl, pl.ds(0, rest)], xs_ref.at[pl.ds(0, rest)], sems.at[sl]).wait()

        def drain_b(j, carry):
            chunk_copy(tl, sl, j, True).wait()
            return carry

        lax.fori_loop(MIN_CHUNKS, nch_ref[tl], drain_b, 0)

    @pl.when(step > 0)
    def _():
        drain_tile(tile - 1, 1 - slot)

    @pl.when(step == pl.num_programs(0) - 1)
    def _():
        drain_tile(tile, slot)


def _dispatch(nch, dst, h2, idx, rank, seg, xs, t0):
    n, d = h2.shape
    tm = ROW_TILE
    tok = lambda width: pl.BlockSpec((tm, width), lambda i, a, b: (i, 0))
    return pl.pallas_call(
        functools.partial(_dispatch_kernel, tm=tm, t0=t0),
        grid_spec=pltpu.PrefetchScalarGridSpec(
            num_scalar_prefetch=2,
            grid=(n // tm,),
            in_specs=[tok(d), tok(LANES), tok(LANES),
                      pl.BlockSpec((1, 1, LANES), lambda i, a, b: (t0 + i, 0, 0)),
                      pl.BlockSpec(memory_space=pl.ANY)],
            out_specs=pl.BlockSpec(memory_space=pl.ANY),
            scratch_shapes=[pltpu.VMEM((2, HALF_SORT, HALF), U32), pltpu.VMEM((2, HALF_SORT, HALF), U32),
                            pltpu.SemaphoreType.DMA((2,))]),
        out_shape=jax.ShapeDtypeStruct(xs.shape, xs.dtype),
        input_output_aliases={6: 0},
        compiler_params=_params("arbitrary"),
    )(nch, dst, h2, idx, rank, seg, xs)


def _expert_kernel(start_ref, nblk_ref, xs_ref, w1_ref, b1_ref, w2_ref, b2_ref, o_ref,
                   w1b, w2b, xbuf, obuf, sin, sout):
    e = pl.program_id(0)
    n = nblk_ref[e]

    def rows(expert, j):
        return pl.ds(pl.multiple_of(start_ref[expert] + j * MOE_ROWS, MOE_ROWS), MOE_ROWS)

    def in_copy(expert, j, slot):
        return pltpu.make_async_copy(xs_ref.at[rows(expert, j)], xbuf.at[slot], sin.at[slot])

    def out_copy(j, slot):
        return pltpu.make_async_copy(obuf.at[slot], o_ref.at[rows(e, j)], sout.at[slot])

    @pl.when((e == 0) & (n > 0))
    def _():
        in_copy(e, 0, 0).start()

    @pl.when(n > 0)
    def _():
        w1b[...] = w1_ref[0].astype(BF16)
        w2b[...] = w2_ref[0].astype(BF16)

        def block(j, carry):
            slot = j % 2

            @pl.when(j + 1 < n)
            def _():
                in_copy(e, j + 1, 1 - slot).start()

            in_copy(e, j, slot).wait()

            @pl.when(j >= 2)
            def _():
                out_copy(j - 2, slot).wait()

            x_lo, x_hi = _unpack_rows(xbuf[slot])
            gu = (jnp.dot(x_lo, w1b[:HALF, :], preferred_element_type=F32)
                  + jnp.dot(x_hi, w1b[HALF:, :], preferred_element_type=F32)) + b1_ref[0]
            gt = jnp.minimum(gu[:, :D_FF], SWIGLU_LIMIT)
            up = jnp.clip(gu[:, D_FF:], -SWIGLU_LIMIT, SWIGLU_LIMIT)
            hid = (up + 1.0) * gt * (1.0 / (1.0 + jnp.exp(-SWIGLU_ALPHA * gt)))
            out = jnp.dot(hid.astype(BF16), w2b[...], preferred_element_type=F32) + b2_ref[0]
            obuf[slot] = _pack_rows(out.astype(BF16).astype(F32))
            out_copy(j, slot).start()
            return carry

        lax.fori_loop(0, n, block, 0)

        @pl.when(n >= 2)
        def _():
            out_copy(n - 2, n % 2).wait()

        out_copy(n - 1, (n - 1) % 2).wait()

    nxt = jnp.minimum(e + 1, pl.num_programs(0) - 1)

    @pl.when((e + 1 < pl.num_programs(0)) & (nblk_ref[nxt] > 0))
    def _():
        in_copy(nxt, 0, 0).start()


def _experts(row_start, n_blk, xs, w1, b1, w2, b2, layer):
    n_rows, half = xs.shape
    d = 2 * half
    f2 = w1.shape[-1]
    n_exp = w1.shape[1]
    ne = w1.shape[0] * n_exp
    w1 = w1.reshape(ne, d, f2)
    w2 = w2.reshape(ne, D_FF, d)
    pick = lambda e, st, nb: (layer * n_exp + e, 0, 0)
    return pl.pallas_call(
        _expert_kernel,
        grid_spec=pltpu.PrefetchScalarGridSpec(
            num_scalar_prefetch=2,
            grid=(n_exp,),
            in_specs=[pl.BlockSpec(memory_space=pl.ANY),
                      pl.BlockSpec((1, d, f2), pick), pl.BlockSpec((1, 1, f2), pick),
                      pl.BlockSpec((1, D_FF, d), pick), pl.BlockSpec((1, 1, d), pick)],
            out_specs=pl.BlockSpec(memory_space=pl.ANY),
            scratch_shapes=[pltpu.VMEM((d, f2), BF16), pltpu.VMEM((D_FF, d), BF16),
                            pltpu.VMEM((2, MOE_ROWS, half), U32), pltpu.VMEM((2, MOE_ROWS, half), U32),
                            pltpu.SemaphoreType.DMA((2,)), pltpu.SemaphoreType.DMA((2,))]),
        out_shape=jax.ShapeDtypeStruct((n_rows, half), U32),
        input_output_aliases={2: 0},
        compiler_params=_params("arbitrary"),
    )(row_start, n_blk, xs, w1, b1.reshape(ne, 1, f2), w2, b2.reshape(ne, 1, d))


def _combine_kernel(nch_ref, dst_ref, outs_ref, idx_ref, rank_ref, gate_ref, seg_ref, x1_ref, g2_ref, fg_ref,
                    o_ref, buf, sems, *, tm, nt, t0, final):
    step = pl.program_id(0) * nt + pl.program_id(1)
    tile = t0 + step
    slot = step % 2

    def chunk_copy(tl, sl, j):
        src = pl.multiple_of(dst_ref[tl * MAX_CHUNKS + j], CHUNK)
        dst = pl.multiple_of(j * CHUNK, CHUNK)
        return pltpu.make_async_copy(outs_ref.at[pl.ds(src, CHUNK)], buf.at[sl, pl.ds(dst, CHUNK)], sems.at[sl])

    def fetch(tl, sl, first):
        def issue(j, carry):
            chunk_copy(tl, sl, j).start()
            return carry
        lax.fori_loop(first, nch_ref[tl], issue, 0)

    @pl.when(step == 0)
    def _():
        buf[...] = jnp.zeros_like(buf)
        fetch(tile, slot, 0)

    last = pl.num_programs(0) * nt - 1
    nxt = t0 + jnp.minimum(step + 1, last)
    for j in range(MIN_CHUNKS):
        chunk_copy(nxt, 1 - slot, j).start()

    def drain(j, carry):
        chunk_copy(tile, slot, j).wait()
        return carry

    n = nch_ref[tile]
    idx, rank, gate, seg = idx_ref[0], rank_ref[0], gate_ref[0], seg_ref[0]
    col = lax.broadcasted_iota(I32, (tm, SORT_ROWS), 1).astype(F32)
    weights = jnp.zeros((tm, SORT_ROWS), F32)
    for k in range(TOP_K):
        pos = jnp.sum(_slot_values(idx, rank, seg, k), axis=-1, keepdims=True)
        weights = jnp.where(col == pos, gate[:, k:k + 1], weights)
    wh, wl = _split(weights)
    fetch(nxt, 1 - slot, MIN_CHUNKS)
    min_rows = MIN_CHUNKS * CHUNK
    pltpu.make_async_copy(outs_ref.at[pl.ds(0, min_rows)], buf.at[slot, pl.ds(0, min_rows)], sems.at[slot]).wait()
    lax.fori_loop(MIN_CHUNKS, n, drain, 0)
    y = jnp.concatenate(
        [jnp.dot(wh, rows, preferred_element_type=F32) + jnp.dot(wl, rows, preferred_element_type=F32)
         for rows in _unpack_rows(buf[slot])], axis=1)
    x2 = x1_ref[0] + g2_ref[0] * y
    if final:
        x2 = _rms(x2) * fg_ref[...]
    o_ref[0] = x2

    @pl.when(step == last)
    def _():
        def drain_extra(j, carry):
            chunk_copy(nxt, 1 - slot, j).wait()
            return carry
        lax.fori_loop(0, nch_ref[nxt], drain_extra, 0)


def _combine(nch, dst, outs, idx, rank, gate, seg, x1, g2, fg, t0, final):
    b, l, d = x1.shape
    tm = ROW_TILE
    nt = l // tm
    per_batch = g2.shape[0] > 1
    bidx = (lambda i, t, a, c: (i, 0, 0)) if per_batch else (lambda i, t, a, c: (0, 0, 0))
    tok = lambda width: pl.BlockSpec((1, tm, width), lambda i, t, a, c: (i, t, 0))
    return pl.pallas_call(
        functools.partial(_combine_kernel, tm=tm, nt=nt, t0=t0, final=final),
        grid_spec=pltpu.PrefetchScalarGridSpec(
            num_scalar_prefetch=2,
            grid=(b, nt),
            in_specs=[pl.BlockSpec(memory_space=pl.ANY), tok(LANES), tok(LANES), tok(LANES),
                      pl.BlockSpec((1, 1, LANES), lambda i, t, a, c: (t0 + i * nt + t, 0, 0)),
                      tok(d), pl.BlockSpec((1, 1, d), bidx),
                      pl.BlockSpec((1, d), lambda i, t, a, c: (0, 0))],
            out_specs=tok(d),
            scratch_shapes=[pltpu.VMEM((2, SORT_ROWS, HALF), U32), pltpu.SemaphoreType.DMA((2,))]),
        out_shape=jax.ShapeDtypeStruct((b, l, d), F32),
        compiler_params=_params("arbitrary", "arbitrary"),
    )(nch, dst, outs, idx, rank, gate, seg, x1, g2, fg)


def _layer_weights(w_in_l):
    cuts = np.cumsum([A_WIDTH, A_WIDTH, A_WIDTH, B_WIDTH, C_WIDTH, C_KV * C_HD]).tolist()
    front = w_in_l[:, :cuts[4]]
    kc = w_in_l[:, cuts[4]:cuts[5]]
    vc = w_in_l[:, cuts[5]:]

    def rep(w):
        return jnp.concatenate([w[:, C_HD * (j // C_GROUP):C_HD * (j // C_GROUP + 1)] for j in range(C_HEADS)], axis=1)

    lat = jnp.concatenate([front, rep(kc), rep(vc)], axis=1).astype(BF16)
    ctx = jnp.concatenate([front, rep(kc), rep(vc), kc, vc], axis=1).astype(BF16)
    return ctx, lat


def _rep_heads(a):
    return jnp.repeat(a, C_GROUP, axis=2).reshape(a.shape[0], a.shape[1], C_HEADS * C_HD)


def kernel(x_prompt, x_sample, c, cache_diff_k, cache_diff_v, cache_win_k, cache_win_v, c_ctx, norm1_g, norm2_g, w_mod, b_mod, w_in, diff_lambda, diff_subln_g, w_pool, pool_scale, sink, w_out, w_router, b_router, w1, b1, w2, b2, final_g):
    depth = w_in.shape[0]
    bc, lc, d = x_prompt.shape
    bl, ll, _ = x_sample.shape
    n_ctx, n_lat = bc * lc, bl * ll
    n_tok = n_ctx + n_lat
    n_tiles = n_tok // ROW_TILE
    n_blocks = -(-(n_tok * TOP_K + n_tiles * N_EXPERTS * (CHUNK - 1)) // MOE_ROWS) + N_EXPERTS
    n_rows = n_blocks * MOE_ROWS

    mod_rows = -(-(1 + bl) // 8) * 8
    cmat = jnp.zeros((mod_rows, d), F32).at[0].set(c_ctx).at[1:1 + bl].set(c)
    mod = _mod_vectors(cmat, w_mod, b_mod)
    tabs_a = _rope_tables(ll, A_HD, A_WIDTH)
    tabs_c = _rope_tables(ll, C_HD, C_WIDTH)
    tables = (tabs_a[0], tabs_a[1], tabs_c[0], tabs_c[1])
    fg = final_g.reshape(1, d)

    xp, xs_lat = x_prompt, x_sample
    new_cache = [[], [], [], []]
    for i in range(depth):
        lam_init = 0.8 - 0.6 * math.exp(-0.3 * i)
        mv = lambda rows, j: mod[i, rows, j * d:(j + 1) * d].reshape(-1, 1, d)
        ctx_rows, lat_rows = slice(0, 1), slice(1, 1 + bl)
        w_ctx, w_lat = _layer_weights(w_in[i])
        n1 = norm1_g[i].reshape(1, d)
        n2 = norm2_g[i].reshape(1, d)
        wbd = jax.scipy.linalg.block_diag(*[w_pool[i, g] for g in range(B_GROUPS)]).astype(BF16)
        ps = pool_scale[i].reshape(1, B_WIDTH)
        g_tiled = jnp.tile(diff_subln_g[i], A_HEADS).reshape(1, A_WIDTH)
        wo = w_out[i].astype(BF16)
        wr = jnp.zeros((d, LANES), F32).at[:, :N_EXPERTS].set(w_router[i])
        br = jnp.zeros((1, LANES), F32).at[0, :N_EXPERTS].set(b_router[i])

        qa, ka, va, u, qc, kr, vr, ka32, va32, kc32, vc32 = _inproj(
            xp, mv(ctx_rows, 0), mv(ctx_rows, 1), n1, w_ctx, None)
        new_cache[0].append(ka32.reshape(bc, lc, 2 * A_HEADS, A_HD))
        new_cache[1].append(va32.reshape(bc, lc, A_HEADS, 2 * A_HD))
        new_cache[2].append(kc32.reshape(bc, lc, C_KV, C_HD))
        new_cache[3].append(vc32.reshape(bc, lc, C_KV, C_HD))
        ob = _pool(u, wbd, ps)
        oa = _diff_attn(qa, [(ka, va)], diff_lambda[i], g_tiled, lam_init)
        oc = _gqa(sink[i], qc, kr, vr)
        x1_c, h2_c, idx_c, gate_c, rank_c, cnt_c = _finish(
            xp, oa, ob, oc, wo, mv(ctx_rows, 2), mv(ctx_rows, 3), mv(ctx_rows, 4), n2, wr, br)

        qa, ka, va, u, qc, kr, vr = _inproj(xs_lat, mv(lat_rows, 0), mv(lat_rows, 1), n1, w_lat, tables)
        ob = _pool(u, wbd, ps)
        dk = cache_diff_k[:, i].reshape(bl, -1, A_WIDTH).astype(BF16)
        dv = cache_diff_v[:, i].reshape(bl, -1, A_WIDTH).astype(BF16)
        oa = _diff_attn(qa, [(ka, va), (dk, dv)], diff_lambda[i], g_tiled, lam_init)
        wk = _rep_heads(cache_win_k[:, i]).astype(BF16)
        wv = _rep_heads(cache_win_v[:, i]).astype(BF16)
        oc = _gqa(sink[i], qc, kr, vr, (wk, wv))
        x1_l, h2_l, idx_l, gate_l, rank_l, cnt_l = _finish(
            xs_lat, oa, ob, oc, wo, mv(lat_rows, 2), mv(lat_rows, 3), mv(lat_rows, 4), n2, wr, br)

        cnt = jnp.concatenate([cnt_c, cnt_l], axis=0)[:, 0, :N_EXPERTS].astype(I32)
        c8 = (cnt + CHUNK - 1) // CHUNK * CHUNK
        seg_end = jnp.cumsum(c8, axis=1)
        seg = seg_end - c8
        padded = (jnp.sum(c8, axis=0) + MOE_ROWS - 1) // MOE_ROWS * MOE_ROWS
        pad_end = jnp.cumsum(padded)
        gbase = (pad_end - padded)[None, :] + jnp.cumsum(c8, axis=0) - c8
        nch = seg_end[:, -1] // CHUNK
        j8 = jnp.arange(MAX_CHUNKS, dtype=I32) * CHUNK
        chunk_e = jnp.minimum(jnp.sum((seg_end[:, None, :] <= j8[None, :, None]).astype(I32), axis=-1),
                              N_EXPERTS - 1)
        onehot = chunk_e[..., None] == jnp.arange(N_EXPERTS, dtype=I32)
        dst = (jnp.sum(jnp.where(onehot, (gbase - seg)[:, None, :], 0), axis=-1) + j8[None, :]).reshape(-1)
        seg_f = jnp.zeros((cnt.shape[0], 1, LANES), F32).at[:, 0, :N_EXPERTS].set(seg.astype(F32))
        tiles_c = cnt_c.shape[0]

        rows_in = jnp.zeros((n_rows, HALF), U32)
        rows_in = _dispatch(nch, dst, h2_c.reshape(n_ctx, d), idx_c.reshape(n_ctx, LANES),
                            rank_c.reshape(n_ctx, LANES), seg_f, rows_in, 0)
        rows_in = _dispatch(nch, dst, h2_l.reshape(n_lat, d), idx_l.reshape(n_lat, LANES),
                            rank_l.reshape(n_lat, LANES), seg_f, rows_in, tiles_c)
        rows_out = _experts(pad_end - padded, padded // MOE_ROWS, rows_in, w1, b1, w2, b2, i)
        final = i == depth - 1
        xp = _combine(nch, dst, rows_out, idx_c, rank_c, gate_c, seg_f, x1_c, mv(ctx_rows, 5), fg, 0, final)
        xs_lat = _combine(nch, dst, rows_out, idx_l, rank_l, gate_l, seg_f, x1_l, mv(lat_rows, 5), fg,
                          tiles_c, final)

    return (xp, xs_lat) + tuple(jnp.stack(parts, axis=1) for parts in new_cache)
```

```python
import functools
import math

import numpy as np
import jax
import jax.numpy as jnp
from jax import lax
from jax.experimental import pallas as pl
from jax.experimental.pallas import tpu as pltpu

F32 = jnp.float32
BF16 = jnp.bfloat16
I32 = jnp.int32
U32 = jnp.uint32

D_MODEL = 1024
GRID_W = 64
ROPE_BASE = 10000.0
NORM_EPS = 1e-6
A_HD = 32
A_HEADS = 4
A_WIDTH = 256
B_WIDTH = 256
B_GROUPS = 4
B_GC = 64
C_HD = 64
C_HEADS = 8
C_KV = 2
C_GROUP = 4
C_WIDTH = 512
WINDOW = 128
QBLOCK = 128
N_EXPERTS = 32
TOP_K = 4
D_FF = 1024
SWIGLU_LIMIT = 7.0
SWIGLU_ALPHA = 1.702

LANES = 128
ROW_TILE = 256
MOE_ROWS = 512
KEY_CHUNK = 512
NEG = -1e30
LOG2E = math.log2(math.e)
VMEM_LIMIT = 56 * 1024 * 1024


def _params(*sem):
    return pltpu.CompilerParams(dimension_semantics=sem, vmem_limit_bytes=VMEM_LIMIT)


def _split(x):
    hi = x.astype(BF16)
    lo = (x - hi.astype(F32)).astype(BF16)
    return hi, lo


def _dot3(a, b):
    ah, al = _split(a)
    bh, bl = _split(b)
    return (jnp.dot(ah, bh, preferred_element_type=F32)
            + (jnp.dot(ah, bl, preferred_element_type=F32)
               + jnp.dot(al, bh, preferred_element_type=F32)))


def _nt_dot(a, b):
    return lax.dot_general(a, b, (((1,), (1,)), ((), ())), preferred_element_type=F32)


def _rms(x):
    return x * lax.rsqrt(jnp.mean(x * x, axis=-1, keepdims=True) + NORM_EPS)


HALF = D_MODEL // 2
HIGH_BITS = 0xFFFF0000


def _pack_rows(x):
    lo = lax.bitcast_convert_type(x[:, :HALF], U32) >> 16
    hi = lax.bitcast_convert_type(x[:, HALF:], U32) & jnp.uint32(HIGH_BITS)
    return lo | hi


def _unpack_rows(w):
    lo = lax.bitcast_convert_type(w << 16, F32).astype(BF16)
    hi = lax.bitcast_convert_type(w & jnp.uint32(HIGH_BITS), F32).astype(BF16)
    return lo, hi


def _mod_kernel(c_ref, w_ref, b_ref, o_ref):
    c = c_ref[...]
    a = c * (1.0 / (1.0 + jnp.exp(-c)))
    o_ref[0] = _dot3(a, w_ref[0]) + b_ref[0]


def _mod_vectors(cmat, w_mod, b_mod):
    depth, d, e = w_mod.shape
    rows = cmat.shape[0]
    tn = 512
    return pl.pallas_call(
        _mod_kernel,
        grid=(depth, e // tn),
        in_specs=[pl.BlockSpec((rows, d), lambda l, j: (0, 0)),
                  pl.BlockSpec((1, d, tn), lambda l, j: (l, 0, j)),
                  pl.BlockSpec((1, 1, tn), lambda l, j: (l, 0, j))],
        out_specs=pl.BlockSpec((1, rows, tn), lambda l, j: (l, 0, j)),
        out_shape=jax.ShapeDtypeStruct((depth, rows, e), F32),
        compiler_params=_params("arbitrary", "arbitrary"),
    )(cmat, w_mod, b_mod.reshape(depth, 1, e))


def _rope(z, col_ref, row_ref, nf, r0):
    tm, w = z.shape
    outs = []
    for g in range(tm // GRID_W):
        zs = z[GRID_W * g:GRID_W * (g + 1), :]
        c = col_ref[0] + row_ref[0, pl.ds(r0 + g, 1), :]
        sm = col_ref[1] + row_ref[1, pl.ds(r0 + g, 1), :]
        sp = col_ref[2] + row_ref[2, pl.ds(r0 + g, 1), :]
        outs.append(zs * c + pltpu.roll(zs, w - nf, 1) * sm + pltpu.roll(zs, nf, 1) * sp)
    return jnp.concatenate(outs, axis=0)


def _inproj_kernel(*refs, rope, tm):
    x_ref, sh_ref, sc_ref, n1_ref, w_ref = refs[:5]
    if rope:
        ta_col, ta_row, tc_col, tc_row = refs[5:9]
        qa_o, ka_o, va_o, u_o, qc_o, kr_o, vr_o = refs[9:]
    else:
        qa_o, ka_o, va_o, u_o, qc_o, kr_o, vr_o, ka32_o, va32_o, kc32_o, vc32_o = refs[5:]
    x = x_ref[0]
    h = (_rms(x) * n1_ref[...]) * (1.0 + sc_ref[0]) + sh_ref[0]
    hb = h.astype(BF16)

    def seg(a, b):
        return jnp.dot(hb, w_ref[:, a:b], preferred_element_type=F32)

    qa, ka, va, u = seg(0, 256), seg(256, 512), seg(512, 768), seg(768, 1024)
    qc, kr, vr = seg(1024, 1536), seg(1536, 2048), seg(2048, 2560)
    if rope:
        r0 = pl.program_id(1) * (tm // GRID_W)
        qa = _rope(qa, ta_col, ta_row, A_HD // 4, r0)
        ka = _rope(ka, ta_col, ta_row, A_HD // 4, r0)
        qc = _rope(qc, tc_col, tc_row, C_HD // 4, r0)
        kr = _rope(kr, tc_col, tc_row, C_HD // 4, r0)
    else:
        ka32_o[0] = ka
        va32_o[0] = va
        kc32_o[0] = seg(2560, 2688)
        vc32_o[0] = seg(2688, 2816)
    qa_o[0] = (qa * (A_HD ** -0.5 * LOG2E)).astype(BF16)
    ka_o[0] = ka.astype(BF16)
    va_o[0] = va.astype(BF16)
    u_o[0] = u
    qc_o[0] = (qc * (C_HD ** -0.5 * LOG2E)).astype(BF16)
    kr_o[0] = kr.astype(BF16)
    vr_o[0] = vr.astype(BF16)


def _inproj(x, sh, sc, n1, w, tables):
    b, l, d = x.shape
    tm = min(ROW_TILE, l)
    rope = tables is not None
    per_batch = sh.shape[0] > 1
    bidx = (lambda i, t: (i, 0, 0)) if per_batch else (lambda i, t: (0, 0, 0))
    tok = lambda width: pl.BlockSpec((1, tm, width), lambda i, t: (i, t, 0))
    in_specs = [tok(d),
                pl.BlockSpec((1, 1, d), bidx), pl.BlockSpec((1, 1, d), bidx),
                pl.BlockSpec((1, d), lambda i, t: (0, 0)),
                pl.BlockSpec(w.shape, lambda i, t: (0, 0))]
    args = [x, sh, sc, n1, w]
    widths = [(256, BF16), (256, BF16), (256, BF16), (256, F32), (512, BF16), (512, BF16), (512, BF16)]
    if rope:
        for tab in tables:
            in_specs.append(pl.BlockSpec(tab.shape, lambda i, t: (0, 0, 0)))
            args.append(tab)
    else:
        widths += [(256, F32), (256, F32), (128, F32), (128, F32)]
    return pl.pallas_call(
        functools.partial(_inproj_kernel, rope=rope, tm=tm),
        grid=(b, l // tm),
        in_specs=in_specs,
        out_specs=[tok(wd) for wd, _ in widths],
        out_shape=[jax.ShapeDtypeStruct((b, l, wd), dt) for wd, dt in widths],
        compiler_params=_params("arbitrary", "arbitrary"),
    )(*args)


def _rope_tables(n_lat, head_dim, width):
    rows = n_lat // GRID_W
    nf = head_dim // 4
    inv = ROPE_BASE ** (-jnp.arange(nf, dtype=F32) / nf)
    lane = np.arange(width) % head_dim
    half = lane // (2 * nf)
    pair = (lane // nf) % 2
    f = lane % nf

    def part(pos, which):
        ang = pos[:, None] * inv[f][None, :]
        on = jnp.asarray(half == which, F32)[None, :]
        c = jnp.cos(ang) * on
        s = jnp.sin(ang) * on
        sm = -s * jnp.asarray(pair == 0, F32)[None, :]
        sp = s * jnp.asarray(pair == 1, F32)[None, :]
        return jnp.stack([c, sm, sp])

    return part(jnp.arange(GRID_W, dtype=F32), 1), part(jnp.arange(rows, dtype=F32), 0)


POOL_PAD = 8
POOL_CHUNK = 256


def _pool_kernel(u_ref, w_ref, ps_ref, o_ref, pad_ref, *, l):
    zeros = jnp.zeros((POOL_PAD, B_WIDTH), F32)
    pad_ref[0:POOL_PAD, :] = zeros
    pad_ref[POOL_PAD + l:2 * POOL_PAD + l, :] = zeros
    pad_ref[POOL_PAD:POOL_PAD + l, :] = u_ref[0]
    ch = min(POOL_CHUNK, l)
    lane = lax.broadcasted_iota(I32, (ch, B_WIDTH), 1)
    grp = lane >> 6
    half = jnp.where(grp == 0, 1, jnp.where(grp == 1, 2, jnp.where(grp == 2, 4, 8)))
    row = lax.broadcasted_iota(I32, (ch, B_WIDTH), 0)
    for c in range(0, l, ch):
        ld = lambda k: pad_ref[c + POOL_PAD + k:c + POOL_PAD + k + ch, :]
        cur = ld(0)
        s2 = ld(-1) + cur
        s4 = s2 + (ld(-2) + ld(1))
        s8 = s4 + ((ld(-4) + ld(-3)) + (ld(2) + ld(3)))
        s16 = s8 + (((ld(-8) + ld(-7)) + (ld(-6) + ld(-5))) + ((ld(4) + ld(5)) + (ld(6) + ld(7))))
        win = jnp.where(grp == 0, s2, jnp.where(grp == 1, s4, jnp.where(grp == 2, s8, s16)))
        t = row + c
        cnt = (jnp.minimum(t + half, l) - jnp.maximum(t - half, 0)).astype(F32)
        r = (win / cnt - cur).astype(BF16)
        y = jnp.dot(r, w_ref[...], preferred_element_type=F32) * ps_ref[...]
        o_ref[0, c:c + ch, :] = y.astype(BF16)


def _pool(u, wbd, ps):
    b, l, w = u.shape
    return pl.pallas_call(
        functools.partial(_pool_kernel, l=l),
        grid=(b,),
        in_specs=[pl.BlockSpec((1, l, w), lambda i: (i, 0, 0)),
                  pl.BlockSpec((w, w), lambda i: (0, 0)),
                  pl.BlockSpec((1, w), lambda i: (0, 0))],
        out_specs=pl.BlockSpec((1, l, w), lambda i: (i, 0, 0)),
        out_shape=jax.ShapeDtypeStruct((b, l, w), BF16),
        scratch_shapes=[pltpu.VMEM((l + 2 * POOL_PAD, w), F32)],
        compiler_params=_params("arbitrary"),
    )(u, wbd, ps)


def _diff_attn_kernel(*refs, n_src, tq, lam_init):
    q_ref = refs[0]
    srcs = [(refs[1 + 2 * i], refs[2 + 2 * i]) for i in range(n_src)]
    lamp_ref, g_ref, o_ref, s_scr, p_scr = refs[1 + 2 * n_src:]
    chunks = []
    col = 0
    for k_ref, _ in srcs:
        keys = k_ref.shape[1]
        tk = min(keys, KEY_CHUNK)
        for c in range(0, keys, tk):
            chunks.append((k_ref, c, tk, col))
            col += tk
    q32 = q_ref[0].astype(F32)
    lp = lamp_ref[...]
    lam = (jnp.exp(jnp.sum(lp[0:1] * lp[1:2], axis=-1, keepdims=True))
           - jnp.exp(jnp.sum(lp[2:3] * lp[3:4], axis=-1, keepdims=True)) + lam_init)
    lane_row = lax.broadcasted_iota(I32, (1, A_WIDTH), 1)
    lane = lax.broadcasted_iota(I32, (tq, A_WIDTH), 1)

    def scores(j):
        qm = (q32 * ((lane_row >> 5) == j).astype(F32)).astype(BF16)
        part = jnp.full((tq, LANES), NEG, F32)
        for k_ref, c, tk, col in chunks:
            s = _nt_dot(qm, k_ref[0, c:c + tk, :])
            s_scr[j % 2, :, col:col + tk] = s
            for i in range(0, tk, LANES):
                part = jnp.maximum(part, s[:, i:i + LANES])
        return jnp.max(part, axis=-1, keepdims=True)

    def weights(j, row_max):
        part = jnp.zeros((tq, LANES), F32)
        for _, _, tk, col in chunks:
            p = jnp.exp2(s_scr[j % 2, :, col:col + tk] - row_max)
            for i in range(0, tk, LANES):
                part = part + p[:, i:i + LANES]
            p_scr[j % 2, :, col:col + tk] = p.astype(BF16)
        return jnp.sum(part, axis=-1, keepdims=True)

    def values(j, den):
        acc, col = None, 0
        for _, v_ref in srcs:
            keys = v_ref.shape[1]
            part = jnp.dot(p_scr[j % 2, :, col:col + keys], v_ref[0], preferred_element_type=F32)
            acc = part if acc is None else acc + part
            col += keys
        return acc / den

    out = jnp.zeros((tq, A_WIDTH), F32)
    row_max = scores(0)
    for j in range(2 * A_HEADS):
        den = weights(j, row_max)
        if j + 1 < 2 * A_HEADS:
            row_max = scores(j + 1)
        if j % 2 == 0:
            first = values(j, den)
        else:
            out = out + jnp.where((lane >> 6) == j // 2, first - lam * values(j, den), 0.0)
    sq = out * out
    rs = jnp.zeros((tq, A_WIDTH), F32)
    for h in range(A_HEADS):
        msk = (lane >> 6) == h
        ms = jnp.sum(jnp.where(msk, sq, 0.0), axis=-1, keepdims=True) * (1.0 / (2 * A_HD))
        rs = rs + jnp.where(msk, lax.rsqrt(ms + NORM_EPS), 0.0)
    o_ref[0] = (((out * rs) * g_ref[...]) * (1.0 - lam_init)).astype(BF16)


def _diff_attn(q, srcs, lam_p, g_tiled, lam_init):
    b, l, w = q.shape
    s_len = sum(k.shape[1] for k, _ in srcs)
    tq = min(ROW_TILE, l)
    in_specs = [pl.BlockSpec((1, tq, w), lambda i, t: (i, t, 0))]
    args = [q]
    for k, v in srcs:
        for a in (k, v):
            in_specs.append(pl.BlockSpec((1,) + a.shape[1:], lambda i, t: (i, 0, 0)))
            args.append(a)
    in_specs += [pl.BlockSpec(lam_p.shape, lambda i, t: (0, 0)),
                 pl.BlockSpec(g_tiled.shape, lambda i, t: (0, 0))]
    return pl.pallas_call(
        functools.partial(_diff_attn_kernel, n_src=len(srcs), tq=tq, lam_init=lam_init),
        grid=(b, l // tq),
        in_specs=in_specs,
        out_specs=pl.BlockSpec((1, tq, w), lambda i, t: (i, t, 0)),
        out_shape=jax.ShapeDtypeStruct((b, l, w), BF16),
        scratch_shapes=[pltpu.VMEM((2, tq, s_len), F32), pltpu.VMEM((2, tq, s_len), BF16)],
        compiler_params=_params("arbitrary", "arbitrary"),
    )(*args, lam_p, g_tiled)


GQA_SLAB = C_GROUP * C_HD


def _gqa_kernel(*refs, windowed, tq, l):
    if windowed:
        sink_ref, q_ref, k_ref, v_ref, kc_ref, vc_ref, o_ref, s_scr, p_scr = refs
    else:
        sink_ref, q_ref, k_ref, v_ref, o_ref, s_scr, p_scr = refs
    i = pl.program_id(1)
    rows = C_GROUP * tq
    shift = int(math.log2(tq))
    q32 = q_ref[0].astype(F32)
    lane_row = lax.broadcasted_iota(I32, (1, GQA_SLAB), 1)
    lane = lax.broadcasted_iota(I32, (tq, GQA_SLAB), 1)
    rid = lax.broadcasted_iota(I32, (rows, 1), 0)
    if windowed:
        ws = pl.multiple_of(jnp.clip((i - 1) * tq, 0, l - 3 * tq), tq)
        qpos = i * tq + (rid & (tq - 1))
        kpos = ws + lax.broadcasted_iota(I32, (1, 3 * tq), 1)
        band = jnp.where(jnp.abs(kpos - qpos) <= WINDOW, 0.0, NEG)
    def slab(g):
        return slice(GQA_SLAB * g, GQA_SLAB * (g + 1))

    def key_sources(g):
        if windowed:
            return [(lambda: k_ref[0, pl.ds(ws, 3 * tq), slab(g)], lambda: v_ref[0, pl.ds(ws, 3 * tq), slab(g)],
                     band, 3 * tq),
                    (lambda: kc_ref[0, :, slab(g)], lambda: vc_ref[0, :, slab(g)], None, kc_ref.shape[1])]
        return [(lambda: k_ref[0, :, slab(g)], lambda: v_ref[0, :, slab(g)], None, l)]

    def sink_col(g):
        sk = jnp.zeros((rows, 1), F32)
        for hh in range(C_GROUP):
            sk = jnp.where((rid >> shift) == hh, sink_ref[C_GROUP * g + hh] * LOG2E, sk)
        return sk

    def scores(g):
        qg = q32[:, slab(g)]
        qs = jnp.concatenate(
            [(qg * ((lane_row >> 6) == hh).astype(F32)).astype(BF16) for hh in range(C_GROUP)], axis=0)
        part = jnp.full((rows, LANES), NEG, F32)
        col = 0
        for keys, _, mask, n in key_sources(g):
            s = _nt_dot(qs, keys())
            if mask is not None:
                s = s + mask
            s_scr[g, :, col:col + n] = s
            for c in range(0, n, LANES):
                part = jnp.maximum(part, s[:, c:c + LANES])
            col += n
        return jnp.maximum(sink_col(g), jnp.max(part, axis=-1, keepdims=True))

    def weights(g, m):
        part = jnp.zeros((rows, LANES), F32)
        col = 0
        for _, _, _, n in key_sources(g):
            p = jnp.exp2(s_scr[g, :, col:col + n] - m)
            for c in range(0, n, LANES):
                part = part + p[:, c:c + LANES]
            p_scr[g, :, col:col + n] = p.astype(BF16)
            col += n
        return jnp.exp2(sink_col(g) - m) + jnp.sum(part, axis=-1, keepdims=True)

    def values(g, den):
        pv, col = None, 0
        for _, vals, _, n in key_sources(g):
            part = jnp.dot(p_scr[g, :, col:col + n], vals(), preferred_element_type=F32)
            pv = part if pv is None else pv + part
            col += n
        o = pv / den
        og = jnp.zeros((tq, GQA_SLAB), F32)
        for hh in range(C_GROUP):
            og = og + jnp.where((lane >> 6) == hh, o[hh * tq:(hh + 1) * tq, :], 0.0)
        o_ref[0, :, slab(g)] = og.astype(BF16)

    maxes = [scores(g) for g in range(C_KV)]
    dens = [weights(g, maxes[g]) for g in range(C_KV)]
    for g in range(C_KV):
        values(g, dens[g])


def _gqa(sink, q, k, v, cache=None):
    b, l, w = q.shape
    windowed = cache is not None
    tq = QBLOCK
    whole = lambda a: pl.BlockSpec((1,) + a.shape[1:], lambda i, t: (i, 0, 0))
    in_specs = [pl.BlockSpec(memory_space=pltpu.SMEM),
                pl.BlockSpec((1, tq, w), lambda i, t: (i, t, 0)), whole(k), whole(v)]
    args = [sink, q, k, v]
    n_keys = l
    if windowed:
        in_specs += [whole(cache[0]), whole(cache[1])]
        args += list(cache)
        n_keys = 3 * tq + cache[0].shape[1]
    return pl.pallas_call(
        functools.partial(_gqa_kernel, windowed=windowed, tq=tq, l=l),
        grid=(b, l // tq),
        in_specs=in_specs,
        out_specs=pl.BlockSpec((1, tq, w), lambda i, t: (i, t, 0)),
        out_shape=jax.ShapeDtypeStruct((b, l, w), BF16),
        scratch_shapes=[pltpu.VMEM((C_KV, C_GROUP * tq, n_keys), F32),
                        pltpu.VMEM((C_KV, C_GROUP * tq, n_keys), BF16)],
        compiler_params=_params("arbitrary", "arbitrary"),
    )(*args)


def _finish_kernel(x_ref, oa_ref, ob_ref, oc_ref, wo_ref, g1_ref, sh_ref, sc_ref, n2_ref, wr_ref, br_ref,
                   x1_o, h2_o, idx_o, gate_o, rank_o, cnt_o, *, tm):
    y = (jnp.dot(oa_ref[0], wo_ref[0:256, :], preferred_element_type=F32)
         + jnp.dot(ob_ref[0], wo_ref[256:512, :], preferred_element_type=F32)
         + jnp.dot(oc_ref[0], wo_ref[512:1024, :], preferred_element_type=F32))
    x1 = x_ref[0] + g1_ref[0] * y
    x1_o[0] = x1
    h2 = (_rms(x1) * n2_ref[...]) * (1.0 + sc_ref[0]) + sh_ref[0]
    h2_o[0] = h2
    lane = lax.broadcasted_iota(I32, (tm, LANES), 1)
    lanef = lane.astype(F32)
    logits = jnp.where(lane < N_EXPERTS, _dot3(h2, wr_ref[...]) + br_ref[...], NEG)
    work = logits
    tops, sels = [], []
    for k in range(TOP_K):
        mk = jnp.max(work, axis=-1, keepdims=True)
        ik = jnp.min(jnp.where(work == mk, lanef, float(LANES)), axis=-1, keepdims=True)
        sel = lanef == ik
        work = jnp.where(sel, 2.0 * NEG, work)
        tops.append((mk, ik))
        sels.append(sel)
    es = [jnp.exp(mk - tops[0][0]) for mk, _ in tops]
    den = (es[0] + es[1]) + (es[2] + es[3])
    multi = jnp.zeros((tm, LANES), F32)
    for sel in sels:
        multi = multi + sel.astype(F32)
    r_i = lax.broadcasted_iota(I32, (tm, tm), 0)
    c_i = lax.broadcasted_iota(I32, (tm, tm), 1)
    before = (c_i < r_i).astype(BF16)
    prior = jnp.dot(before, multi.astype(BF16), preferred_element_type=F32)
    idx = jnp.zeros((tm, LANES), F32)
    gate = jnp.zeros((tm, LANES), F32)
    rank = jnp.zeros((tm, LANES), F32)
    for k in range(TOP_K):
        slot = lane == k
        idx = jnp.where(slot, tops[k][1], idx)
        gate = jnp.where(slot, es[k] / den, gate)
        rk = jnp.sum(jnp.where(sels[k], prior, 0.0), axis=-1, keepdims=True)
        rank = jnp.where(slot, rk, rank)
    idx_o[0] = idx.astype(I32)
    gate_o[0] = gate
    rank_o[0] = rank.astype(I32)
    cnt_o[0] = jnp.sum(multi, axis=0, keepdims=True)


def _finish(x, oa, ob, oc, wo, g1, sh2, sc2, n2, wr, br):
    b, l, d = x.shape
    tm = min(ROW_TILE, l)
    nt = l // tm
    per_batch = g1.shape[0] > 1
    bidx = (lambda i, t: (i, 0, 0)) if per_batch else (lambda i, t: (0, 0, 0))
    tok = lambda width: pl.BlockSpec((1, tm, width), lambda i, t: (i, t, 0))
    const2 = lambda a: pl.BlockSpec(a.shape, lambda i, t: (0, 0))
    vec = pl.BlockSpec((1, 1, d), bidx)
    return pl.pallas_call(
        functools.partial(_finish_kernel, tm=tm),
        grid=(b, l // tm),
        in_specs=[tok(d), tok(256), tok(256), tok(512), const2(wo), vec, vec, vec,
                  const2(n2), const2(wr), const2(br)],
        out_specs=[tok(d), tok(d), tok(LANES), tok(LANES), tok(LANES),
                   pl.BlockSpec((1, 1, LANES), lambda i, t: (i * nt + t, 0, 0))],
        out_shape=[jax.ShapeDtypeStruct((b, l, d), F32), jax.ShapeDtypeStruct((b, l, d), F32),
                   jax.ShapeDtypeStruct((b, l, LANES), I32), jax.ShapeDtypeStruct((b, l, LANES), F32),
                   jax.ShapeDtypeStruct((b, l, LANES), I32), jax.ShapeDtypeStruct((b * nt, 1, LANES), F32)],
        compiler_params=_params("arbitrary", "arbitrary"),
    )(x, oa, ob, oc, wo, g1, sh2, sc2, n2, wr, br)


CHUNK = 8
SORT_ROWS = ROW_TILE * TOP_K + N_EXPERTS * CHUNK
MAX_CHUNKS = SORT_ROWS // CHUNK
MIN_CHUNKS = ROW_TILE * TOP_K // CHUNK
HALF_SORT = SORT_ROWS // 2
HALF_CHUNKS = HALF_SORT // CHUNK
assert HALF_CHUNKS <= MIN_CHUNKS


def _slot_values(idx, rank, seg, k):
    lane = lax.broadcasted_iota(I32, idx.shape, 1)
    return jnp.where(lane == idx[:, k:k + 1], seg + rank[:, k:k + 1].astype(F32), 0.0)


def _dispatch_kernel(nch_ref, dst_ref, h_ref, idx_ref, rank_ref, seg_ref, xs_in_ref, xs_ref, srt_a, srt_b, sems,
                     *, tm, t0):
    del xs_in_ref
    step = pl.program_id(0)
    tile = t0 + step
    slot = step % 2
    idx, rank, seg = idx_ref[...], rank_ref[...], seg_ref[0]
    ones = jnp.ones((CHUNK, LANES), BF16)
    pos = []
    for k in range(TOP_K):
        hi, lo = _split(_slot_values(idx, rank, seg, k))
        pos.append((_nt_dot(ones, hi) + _nt_dot(ones, lo))[0:1, :])
    hb = h_ref[...].astype(BF16)

    def sorted_half(first_row):
        row = (lax.broadcasted_iota(I32, (HALF_SORT, tm), 0) + first_row).astype(F32)
        perm = jnp.zeros((HALF_SORT, tm), F32)
        for p in pos:
            perm = perm + (row == p).astype(F32)
        return _pack_rows(jnp.dot(perm.astype(BF16), hb, preferred_element_type=F32))

    def chunk_copy(tl, sl, j, second):
        src = pl.multiple_of(j * CHUNK - (HALF_SORT if second else 0), CHUNK)
        dst = pl.multiple_of(dst_ref[tl * MAX_CHUNKS + j], CHUNK)
        half = srt_b if second else srt_a
        return pltpu.make_async_copy(half.at[sl, pl.ds(src, CHUNK)], xs_ref.at[pl.ds(dst, CHUNK)], sems.at[sl])

    srt_a[slot] = sorted_half(0)
    for j in range(HALF_CHUNKS):
        chunk_copy(tile, slot, j, False).start(priority=j % 2)
    srt_b[slot] = sorted_half(HALF_SORT)
    for j in range(HALF_CHUNKS, MIN_CHUNKS):
        chunk_copy(tile, slot, j, True).start(priority=j % 2)

    def issue(j, carry):
        chunk_copy(tile, slot, j, True).start()
        return carry

    lax.fori_loop(MIN_CHUNKS, nch_ref[tile], issue, 0)

    def drain_tile(tl, sl):
        rest = (MIN_CHUNKS - HALF_CHUNKS) * CHUNK
        pltpu.make_async_copy(srt_a.at[sl], xs_ref.at[pl.ds(0, HALF_SORT)], sems.at[sl]).wait()
        pltpu.make_async_copy(srt_b.at[sl, pl.ds(0, rest)], xs_ref.at[pl.ds(0, rest)], sems.at[sl]).wait()

        def drain_b(j, carry):
            chunk_copy(tl, sl, j, True).wait()
            return carry

        lax.fori_loop(MIN_CHUNKS, nch_ref[tl], drain_b, 0)

    @pl.when(step > 0)
    def _():
        drain_tile(tile - 1, 1 - slot)

    @pl.when(step == pl.num_programs(0) - 1)
    def _():
        drain_tile(tile, slot)


def _dispatch(nch, dst, h2, idx, rank, seg, xs, t0):
    n, d = h2.shape
    tm = ROW_TILE
    tok = lambda width: pl.BlockSpec((tm, width), lambda i, a, b: (i, 0))
    return pl.pallas_call(
        functools.partial(_dispatch_kernel, tm=tm, t0=t0),
        grid_spec=pltpu.PrefetchScalarGridSpec(
            num_scalar_prefetch=2,
            grid=(n // tm,),
            in_specs=[tok(d), tok(LANES), tok(LANES),
                      pl.BlockSpec((1, 1, LANES), lambda i, a, b: (t0 + i, 0, 0)),
                      pl.BlockSpec(memory_space=pl.ANY)],
            out_specs=pl.BlockSpec(memory_space=pl.ANY),
            scratch_shapes=[pltpu.VMEM((2, HALF_SORT, HALF), U32), pltpu.VMEM((2, HALF_SORT, HALF), U32),
                            pltpu.SemaphoreType.DMA((2,))]),
        out_shape=jax.ShapeDtypeStruct(xs.shape, xs.dtype),
        input_output_aliases={6: 0},
        compiler_params=_params("arbitrary"),
    )(nch, dst, h2, idx, rank, seg, xs)


def _expert_kernel(start_ref, nblk_ref, xs_ref, w1_ref, b1_ref, w2_ref, b2_ref, o_ref,
                   w1b, w2b, xbuf, obuf, sin, sout):
    e = pl.program_id(0)
    n = nblk_ref[e]

    def rows(expert, j):
        return pl.ds(pl.multiple_of(start_ref[expert] + j * MOE_ROWS, MOE_ROWS), MOE_ROWS)

    def in_copy(expert, j, slot):
        return pltpu.make_async_copy(xs_ref.at[rows(expert, j)], xbuf.at[slot], sin.at[slot])

    def out_copy(j, slot):
        return pltpu.make_async_copy(obuf.at[slot], o_ref.at[rows(e, j)], sout.at[slot])

    @pl.when((e == 0) & (n > 0))
    def _():
        in_copy(e, 0, 0).start()

    @pl.when(n > 0)
    def _():
        w1b[...] = w1_ref[0].astype(BF16)
        w2b[...] = w2_ref[0].astype(BF16)

        def block(j, carry):
            slot = j % 2

            @pl.when(j + 1 < n)
            def _():
                in_copy(e, j + 1, 1 - slot).start()

            in_copy(e, j, slot).wait()

            @pl.when(j >= 2)
            def _():
                out_copy(j - 2, slot).wait()

            x_lo, x_hi = _unpack_rows(xbuf[slot])
            gu = (jnp.dot(x_lo, w1b[:HALF, :], preferred_element_type=F32)
                  + jnp.dot(x_hi, w1b[HALF:, :], preferred_element_type=F32)) + b1_ref[0]
            gt = jnp.minimum(gu[:, :D_FF], SWIGLU_LIMIT)
            up = jnp.clip(gu[:, D_FF:], -SWIGLU_LIMIT, SWIGLU_LIMIT)
            hid = (up + 1.0) * gt * (1.0 / (1.0 + jnp.exp(-SWIGLU_ALPHA * gt)))
            out = jnp.dot(hid.astype(BF16), w2b[...], preferred_element_type=F32) + b2_ref[0]
            obuf[slot] = _pack_rows(out.astype(BF16).astype(F32))
            out_copy(j, slot).start()
            return carry

        lax.fori_loop(0, n, block, 0)

        @pl.when(n >= 2)
        def _():
            out_copy(n - 2, n % 2).wait()

        out_copy(n - 1, (n - 1) % 2).wait()

    nxt = jnp.minimum(e + 1, pl.num_programs(0) - 1)

    @pl.when((e + 1 < pl.num_programs(0)) & (nblk_ref[nxt] > 0))
    def _():
        in_copy(nxt, 0, 0).start()


def _experts(row_start, n_blk, xs, w1, b1, w2, b2, layer):
    n_rows, half = xs.shape
    d = 2 * half
    f2 = w1.shape[-1]
    n_exp = w1.shape[1]
    ne = w1.shape[0] * n_exp
    w1 = w1.reshape(ne, d, f2)
    w2 = w2.reshape(ne, D_FF, d)
    pick = lambda e, st, nb: (layer * n_exp + e, 0, 0)
    return pl.pallas_call(
        _expert_kernel,
        grid_spec=pltpu.PrefetchScalarGridSpec(
            num_scalar_prefetch=2,
            grid=(n_exp,),
            in_specs=[pl.BlockSpec(memory_space=pl.ANY),
                      pl.BlockSpec((1, d, f2), pick), pl.BlockSpec((1, 1, f2), pick),
                      pl.BlockSpec((1, D_FF, d), pick), pl.BlockSpec((1, 1, d), pick)],
            out_specs=pl.BlockSpec(memory_space=pl.ANY),
            scratch_shapes=[pltpu.VMEM((d, f2), BF16), pltpu.VMEM((D_FF, d), BF16),
                            pltpu.VMEM((2, MOE_ROWS, half), U32), pltpu.VMEM((2, MOE_ROWS, half), U32),
                            pltpu.SemaphoreType.DMA((2,)), pltpu.SemaphoreType.DMA((2,))]),
        out_shape=jax.ShapeDtypeStruct((n_rows, half), U32),
        input_output_aliases={2: 0},
        compiler_params=_params("arbitrary"),
    )(row_start, n_blk, xs, w1, b1.reshape(ne, 1, f2), w2, b2.reshape(ne, 1, d))


def _combine_kernel(nch_ref, dst_ref, outs_ref, idx_ref, rank_ref, gate_ref, seg_ref, x1_ref, g2_ref, fg_ref,
                    o_ref, buf, sems, *, tm, nt, t0, final):
    step = pl.program_id(0) * nt + pl.program_id(1)
    tile = t0 + step
    slot = step % 2

    def chunk_copy(tl, sl, j):
        src = pl.multiple_of(dst_ref[tl * MAX_CHUNKS + j], CHUNK)
        dst = pl.multiple_of(j * CHUNK, CHUNK)
        return pltpu.make_async_copy(outs_ref.at[pl.ds(src, CHUNK)], buf.at[sl, pl.ds(dst, CHUNK)], sems.at[sl])

    def fetch(tl, sl, first):
        def issue(j, carry):
            chunk_copy(tl, sl, j).start()
            return carry
        lax.fori_loop(first, nch_ref[tl], issue, 0)

    @pl.when(step == 0)
    def _():
        buf[...] = jnp.zeros_like(buf)
        fetch(tile, slot, 0)

    last = pl.num_programs(0) * nt - 1
    nxt = t0 + jnp.minimum(step + 1, last)
    for j in range(MIN_CHUNKS):
        chunk_copy(nxt, 1 - slot, j).start(priority=j % 2)

    def drain(j, carry):
        chunk_copy(tile, slot, j).wait()
        return carry

    n = nch_ref[tile]
    idx, rank, gate, seg = idx_ref[0], rank_ref[0], gate_ref[0], seg_ref[0]
    col = lax.broadcasted_iota(I32, (tm, SORT_ROWS), 1).astype(F32)
    weights = jnp.zeros((tm, SORT_ROWS), F32)
    for k in range(TOP_K):
        pos = jnp.sum(_slot_values(idx, rank, seg, k), axis=-1, keepdims=True)
        weights = jnp.where(col == pos, gate[:, k:k + 1], weights)
    wh, wl = _split(weights)
    fetch(nxt, 1 - slot, MIN_CHUNKS)
    min_rows = MIN_CHUNKS * CHUNK
    pltpu.make_async_copy(outs_ref.at[pl.ds(0, min_rows)], buf.at[slot, pl.ds(0, min_rows)], sems.at[slot]).wait()
    lax.fori_loop(MIN_CHUNKS, n, drain, 0)
    y = jnp.concatenate(
        [jnp.dot(wh, rows, preferred_element_type=F32) + jnp.dot(wl, rows, preferred_element_type=F32)
         for rows in _unpack_rows(buf[slot])], axis=1)
    x2 = x1_ref[0] + g2_ref[0] * y
    if final:
        x2 = _rms(x2) * fg_ref[...]
    o_ref[0] = x2

    @pl.when(step == last)
    def _():
        def drain_extra(j, carry):
            chunk_copy(nxt, 1 - slot, j).wait()
            return carry
        lax.fori_loop(0, nch_ref[nxt], drain_extra, 0)


def _combine(nch, dst, outs, idx, rank, gate, seg, x1, g2, fg, t0, final):
    b, l, d = x1.shape
    tm = ROW_TILE
    nt = l // tm
    per_batch = g2.shape[0] > 1
    bidx = (lambda i, t, a, c: (i, 0, 0)) if per_batch else (lambda i, t, a, c: (0, 0, 0))
    tok = lambda width: pl.BlockSpec((1, tm, width), lambda i, t, a, c: (i, t, 0))
    return pl.pallas_call(
        functools.partial(_combine_kernel, tm=tm, nt=nt, t0=t0, final=final),
        grid_spec=pltpu.PrefetchScalarGridSpec(
            num_scalar_prefetch=2,
            grid=(b, nt),
            in_specs=[pl.BlockSpec(memory_space=pl.ANY), tok(LANES), tok(LANES), tok(LANES),
                      pl.BlockSpec((1, 1, LANES), lambda i, t, a, c: (t0 + i * nt + t, 0, 0)),
                      tok(d), pl.BlockSpec((1, 1, d), bidx),
                      pl.BlockSpec((1, d), lambda i, t, a, c: (0, 0))],
            out_specs=tok(d),
            scratch_shapes=[pltpu.VMEM((2, SORT_ROWS, HALF), U32), pltpu.SemaphoreType.DMA((2,))]),
        out_shape=jax.ShapeDtypeStruct((b, l, d), F32),
        compiler_params=_params("arbitrary", "arbitrary"),
    )(nch, dst, outs, idx, rank, gate, seg, x1, g2, fg)


def _layer_weights(w_in_l):
    cuts = np.cumsum([A_WIDTH, A_WIDTH, A_WIDTH, B_WIDTH, C_WIDTH, C_KV * C_HD]).tolist()
    front = w_in_l[:, :cuts[4]]
    kc = w_in_l[:, cuts[4]:cuts[5]]
    vc = w_in_l[:, cuts[5]:]

    def rep(w):
        return jnp.concatenate([w[:, C_HD * (j // C_GROUP):C_HD * (j // C_GROUP + 1)] for j in range(C_HEADS)], axis=1)

    lat = jnp.concatenate([front, rep(kc), rep(vc)], axis=1).astype(BF16)
    ctx = jnp.concatenate([front, rep(kc), rep(vc), kc, vc], axis=1).astype(BF16)
    return ctx, lat


def _rep_heads(a):
    return jnp.repeat(a, C_GROUP, axis=2).reshape(a.shape[0], a.shape[1], C_HEADS * C_HD)


def kernel(x_prompt, x_sample, c, cache_diff_k, cache_diff_v, cache_win_k, cache_win_v, c_ctx, norm1_g, norm2_g, w_mod, b_mod, w_in, diff_lambda, diff_subln_g, w_pool, pool_scale, sink, w_out, w_router, b_router, w1, b1, w2, b2, final_g):
    depth = w_in.shape[0]
    bc, lc, d = x_prompt.shape
    bl, ll, _ = x_sample.shape
    n_ctx, n_lat = bc * lc, bl * ll
    n_tok = n_ctx + n_lat
    n_tiles = n_tok // ROW_TILE
    n_blocks = -(-(n_tok * TOP_K + n_tiles * N_EXPERTS * (CHUNK - 1)) // MOE_ROWS) + N_EXPERTS
    n_rows = n_blocks * MOE_ROWS

    mod_rows = -(-(1 + bl) // 8) * 8
    cmat = jnp.zeros((mod_rows, d), F32).at[0].set(c_ctx).at[1:1 + bl].set(c)
    mod = _mod_vectors(cmat, w_mod, b_mod)
    tabs_a = _rope_tables(ll, A_HD, A_WIDTH)
    tabs_c = _rope_tables(ll, C_HD, C_WIDTH)
    tables = (tabs_a[0], tabs_a[1], tabs_c[0], tabs_c[1])
    fg = final_g.reshape(1, d)

    xp, xs_lat = x_prompt, x_sample
    new_cache = [[], [], [], []]
    for i in range(depth):
        lam_init = 0.8 - 0.6 * math.exp(-0.3 * i)
        mv = lambda rows, j: mod[i, rows, j * d:(j + 1) * d].reshape(-1, 1, d)
        ctx_rows, lat_rows = slice(0, 1), slice(1, 1 + bl)
        w_ctx, w_lat = _layer_weights(w_in[i])
        n1 = norm1_g[i].reshape(1, d)
        n2 = norm2_g[i].reshape(1, d)
        wbd = jax.scipy.linalg.block_diag(*[w_pool[i, g] for g in range(B_GROUPS)]).astype(BF16)
        ps = pool_scale[i].reshape(1, B_WIDTH)
        g_tiled = jnp.tile(diff_subln_g[i], A_HEADS).reshape(1, A_WIDTH)
        wo = w_out[i].astype(BF16)
        wr = jnp.zeros((d, LANES), F32).at[:, :N_EXPERTS].set(w_router[i])
        br = jnp.zeros((1, LANES), F32).at[0, :N_EXPERTS].set(b_router[i])

        qa, ka, va, u, qc, kr, vr, ka32, va32, kc32, vc32 = _inproj(
            xp, mv(ctx_rows, 0), mv(ctx_rows, 1), n1, w_ctx, None)
        new_cache[0].append(ka32.reshape(bc, lc, 2 * A_HEADS, A_HD))
        new_cache[1].append(va32.reshape(bc, lc, A_HEADS, 2 * A_HD))
        new_cache[2].append(kc32.reshape(bc, lc, C_KV, C_HD))
        new_cache[3].append(vc32.reshape(bc, lc, C_KV, C_HD))
        ob = _pool(u, wbd, ps)
        oa = _diff_attn(qa, [(ka, va)], diff_lambda[i], g_tiled, lam_init)
        oc = _gqa(sink[i], qc, kr, vr)
        x1_c, h2_c, idx_c, gate_c, rank_c, cnt_c = _finish(
            xp, oa, ob, oc, wo, mv(ctx_rows, 2), mv(ctx_rows, 3), mv(ctx_rows, 4), n2, wr, br)

        qa, ka, va, u, qc, kr, vr = _inproj(xs_lat, mv(lat_rows, 0), mv(lat_rows, 1), n1, w_lat, tables)
        ob = _pool(u, wbd, ps)
        dk = cache_diff_k[:, i].reshape(bl, -1, A_WIDTH).astype(BF16)
        dv = cache_diff_v[:, i].reshape(bl, -1, A_WIDTH).astype(BF16)
        oa = _diff_attn(qa, [(ka, va), (dk, dv)], diff_lambda[i], g_tiled, lam_init)
        wk = _rep_heads(cache_win_k[:, i]).astype(BF16)
        wv = _rep_heads(cache_win_v[:, i]).astype(BF16)
        oc = _gqa(sink[i], qc, kr, vr, (wk, wv))
        x1_l, h2_l, idx_l, gate_l, rank_l, cnt_l = _finish(
            xs_lat, oa, ob, oc, wo, mv(lat_rows, 2), mv(lat_rows, 3), mv(lat_rows, 4), n2, wr, br)

        cnt = jnp.concatenate([cnt_c, cnt_l], axis=0)[:, 0, :N_EXPERTS].astype(I32)
        c8 = (cnt + CHUNK - 1) // CHUNK * CHUNK
        seg_end = jnp.cumsum(c8, axis=1)
        seg = seg_end - c8
        padded = (jnp.sum(c8, axis=0) + MOE_ROWS - 1) // MOE_ROWS * MOE_ROWS
        pad_end = jnp.cumsum(padded)
        gbase = (pad_end - padded)[None, :] + jnp.cumsum(c8, axis=0) - c8
        nch = seg_end[:, -1] // CHUNK
        j8 = jnp.arange(MAX_CHUNKS, dtype=I32) * CHUNK
        chunk_e = jnp.minimum(jnp.sum((seg_end[:, None, :] <= j8[None, :, None]).astype(I32), axis=-1),
                              N_EXPERTS - 1)
        onehot = chunk_e[..., None] == jnp.arange(N_EXPERTS, dtype=I32)
        dst = (jnp.sum(jnp.where(onehot, (gbase - seg)[:, None, :], 0), axis=-1) + j8[None, :]).reshape(-1)
        seg_f = jnp.zeros((cnt.shape[0], 1, LANES), F32).at[:, 0, :N_EXPERTS].set(seg.astype(F32))
        tiles_c = cnt_c.shape[0]

        rows_in = jnp.zeros((n_rows, HALF), U32)
        rows_in = _dispatch(nch, dst, h2_c.reshape(n_ctx, d), idx_c.reshape(n_ctx, LANES),
                            rank_c.reshape(n_ctx, LANES), seg_f, rows_in, 0)
        rows_in = _dispatch(nch, dst, h2_l.reshape(n_lat, d), idx_l.reshape(n_lat, LANES),
                            rank_l.reshape(n_lat, LANES), seg_f, rows_in, tiles_c)
        rows_out = _experts(pad_end - padded, padded // MOE_ROWS, rows_in, w1, b1, w2, b2, i)
        final = i == depth - 1
        xp = _combine(nch, dst, rows_out, idx_c, rank_c, gate_c, seg_f, x1_c, mv(ctx_rows, 5), fg, 0, final)
        xs_lat = _combine(nch, dst, rows_out, idx_l, rank_l, gate_l, seg_f, x1_l, mv(lat_rows, 5), fg,
                          tiles_c, final)

    return (xp, xs_lat) + tuple(jnp.stack(parts, axis=1) for parts in new_cache)
```
